```python
import math, functools
import jax, jax.numpy as jnp
from jax import lax
import numpy as np

D_MODEL = 1024
BATCH = 4
SEQ = 4096
DEPTH = 2

N_HEADS_A = 8
HEAD_DIM = 64
N_KV_GROUPS = 2
Q_PER_GROUP = N_HEADS_A // N_KV_GROUPS
D_ATTN = N_HEADS_A * HEAD_DIM
D_KV = N_KV_GROUPS * HEAD_DIM
CMP_BLOCK = 32
CMP_STRIDE = 16
CMP_HIDDEN = 256
SLC_BLOCK = 64
N_SELECT = 16
N_LOCAL = 2
WINDOW = 512
Q_CHUNK = 128
ATTN_SCALE = HEAD_DIM ** -0.5
NEG = -1e30
FORCE_SCORE = 1e4
D_RNN = D_MODEL
N_RNN_BLOCKS = 16
RNN_BLOCK = D_RNN // N_RNN_BLOCKS
CONV_WIDTH = 4
LRU_C = 8.0
D_FF = 4 * D_MODEL
EPS = 1e-6

IN_SIZES = (D_ATTN, D_KV, D_KV, D_KV, D_KV, D_KV, D_KV, N_HEADS_A * 3, D_RNN, D_RNN, D_MODEL, D_MODEL)
IN_SPLITS = tuple(int(v) for v in np.cumsum(IN_SIZES)[:-1])
D_IN = int(sum(IN_SIZES))

kernel_name = "hybrid_nsa_rglru_gated_block"


def rms_norm(x, w):
    xf = x.astype(jnp.float32)
    var = jnp.mean(xf * xf, axis=-1, keepdims=True)
    return (xf * lax.rsqrt(var + EPS) * w.astype(jnp.float32)).astype(x.dtype)


def masked_softmax(s, mask):
    s = jnp.where(mask, s.astype(jnp.float32), NEG)
    m = jnp.max(s, axis=-1, keepdims=True)
    e = jnp.where(mask, jnp.exp(s - m), 0.0)
    return e / jnp.maximum(jnp.sum(e, axis=-1, keepdims=True), 1e-20)


def compress_blocks(k, pos, w1, w2):
    B, S = k.shape[0], k.shape[1]
    kb = k.reshape(B, S // CMP_STRIDE, CMP_STRIDE, N_KV_GROUPS, HEAD_DIM)
    blocks = jnp.concatenate([kb[:, :-1], kb[:, 1:]], axis=2) + pos[:, None, :]
    nc = blocks.shape[1]
    flat = blocks.transpose(0, 1, 3, 2, 4).reshape(B, nc, N_KV_GROUPS, CMP_BLOCK * HEAD_DIM)
    c = jax.nn.gelu(flat @ w1) @ w2
    return c.transpose(0, 2, 1, 3)


def nsa_attention(q, k_cmp, v_cmp, k_slc, v_slc, k_win, v_win, gates, pos_k, pos_v, ck_w1, ck_w2, cv_w1, cv_w2):
    B, S = q.shape[0], q.shape[1]
    G, R, dh = N_KV_GROUPS, Q_PER_GROUP, HEAD_DIM
    nb = S // SLC_BLOCK
    nc = S // CMP_STRIDE - 1
    nqb = S // Q_CHUNK
    k_sel = min(N_SELECT, nb)
    r_s = SLC_BLOCK // CMP_STRIDE
    r_c = CMP_BLOCK // CMP_STRIDE
    off = r_s + r_c - 2

    qh = q.reshape(B, S, G, R, dh).transpose(0, 2, 3, 1, 4)
    gh = gates.reshape(B, S, G, R, 3).transpose(0, 2, 3, 1, 4)
    kc = compress_blocks(k_cmp, pos_k, ck_w1, ck_w2)
    vc = compress_blocks(v_cmp, pos_v, cv_w1, cv_w2)
    ks_blk = k_slc.transpose(0, 2, 1, 3).reshape(B, G, nb, SLC_BLOCK, dh)
    vs_blk = v_slc.transpose(0, 2, 1, 3).reshape(B, G, nb, SLC_BLOCK, dh)
    pad_w = ((0, 0), (0, 0), (WINDOW, 0), (0, 0))
    kw_pad = jnp.pad(k_win.transpose(0, 2, 1, 3), pad_w)
    vw_pad = jnp.pad(v_win.transpose(0, 2, 1, 3), pad_w)
    cmp_end = jnp.arange(nc) * CMP_STRIDE + CMP_BLOCK - 1
    blk = jnp.arange(nb)
    gather = jax.vmap(jax.vmap(lambda blocks, ix: blocks[ix]))

    def chunk(qb):
        t0 = qb * Q_CHUNK
        qc = lax.dynamic_slice_in_dim(qh, t0, Q_CHUNK, axis=3)
        gc = lax.dynamic_slice_in_dim(gh, t0, Q_CHUNK, axis=3)
        tpos = t0 + jnp.arange(Q_CHUNK)
        s = jnp.einsum('bgrcd,bgnd->bgrcn', qc, kc) * ATTN_SCALE
        p_cmp = masked_softmax(s, cmp_end[None, :] <= tpos[:, None])
        o_cmp = jnp.einsum('bgrcn,bgnd->bgrcd', p_cmp, vc.astype(jnp.float32))
        ps = jnp.pad(p_cmp.sum(axis=2), ((0, 0), (0, 0), (0, 0), (off, 0)))
        imp = 0.0
        for m in range(r_s):
            for n in range(r_c):
                st = off - m - n
                imp = imp + ps[..., st:st + r_s * (nb - 1) + 1:r_s]
        cur = tpos // SLC_BLOCK
        valid_b = blk[None, :] <= cur[:, None]
        forced = (blk[None, :] == 0) | (blk[None, :] > cur[:, None] - N_LOCAL)
        score = jnp.where(valid_b, jnp.where(forced, FORCE_SCORE, imp), NEG)
        top_s, idx = lax.top_k(score, k_sel)
        sel_ok = top_s > NEG / 2
        ksel = gather(ks_blk, idx).reshape(B, G, Q_CHUNK, k_sel * SLC_BLOCK, dh)
        vsel = gather(vs_blk, idx).reshape(B, G, Q_CHUNK, k_sel * SLC_BLOCK, dh)
        kpos = idx[..., None] * SLC_BLOCK + jnp.arange(SLC_BLOCK)
        m_slc = (sel_ok[..., None] & (kpos <= tpos[:, None, None])).reshape(B, G, 1, Q_CHUNK, k_sel * SLC_BLOCK)
        s = jnp.einsum('bgrcd,bgcnd->bgrcn', qc, ksel) * ATTN_SCALE
        o_slc = jnp.einsum('bgrcn,bgcnd->bgrcd', masked_softmax(s, m_slc), vsel.astype(jnp.float32))
        kw = lax.dynamic_slice_in_dim(kw_pad, t0, Q_CHUNK + WINDOW, axis=2)
        vw = lax.dynamic_slice_in_dim(vw_pad, t0, Q_CHUNK + WINDOW, axis=2)
        kpos_w = t0 - WINDOW + jnp.arange(Q_CHUNK + WINDOW)
        diff = tpos[:, None] - kpos_w[None, :]
        m_win = (diff >= 0) & (diff < WINDOW) & (kpos_w[None, :] >= 0)
        s = jnp.einsum('bgrcd,bgnd->bgrcn', qc, kw) * ATTN_SCALE
        o_win = jnp.einsum('bgrcn,bgnd->bgrcd', masked_softmax(s, m_win), vw.astype(jnp.float32))
        out = gc[..., 0:1] * o_cmp + gc[..., 1:2] * o_slc + gc[..., 2:3] * o_win
        return out.astype(q.dtype)

    res = lax.map(chunk, jnp.arange(nqb))
    return res.transpose(1, 0, 4, 2, 3, 5).reshape(B, S, D_ATTN)


def rg_lru_branch(xr, gate, conv_w, conv_b, w_a, b_a, w_i, b_i, lam):
    B, S = xr.shape[0], xr.shape[1]
    xp = jnp.pad(xr, ((0, 0), (CONV_WIDTH - 1, 0), (0, 0)))
    xc = conv_b
    for i in range(CONV_WIDTH):
        xc = xc + xp[:, i:i + S] * conv_w[i]
    xb = xc.reshape(B, S, N_RNN_BLOCKS, RNN_BLOCK)
    r = jax.nn.sigmoid(jnp.einsum('bshi,hij->bshj', xb, w_a).reshape(B, S, D_RNN) + b_a)
    ig = jax.nn.sigmoid(jnp.einsum('bshi,hij->bshj', xb, w_i).reshape(B, S, D_RNN) + b_i)
    log_a = -LRU_C * r.astype(jnp.float32) * jax.nn.softplus(-lam.astype(jnp.float32))
    a = jnp.exp(log_a)
    bt = jnp.sqrt(-jnp.expm1(2.0 * log_a)) * (ig * xc).astype(jnp.float32)

    def combine(lhs, rhs):
        a1, b1 = lhs
        a2, b2 = rhs
        return a1 * a2, a2 * b1 + b2

    _, h = lax.associative_scan(combine, (a, bt), axis=1)
    return h.astype(xr.dtype) * jax.nn.gelu(gate)


def setup_inputs(seed: int = 0) -> dict:
    key = jax.random.key(seed)
    ks = jax.random.split(key, 24)
    f32 = jnp.float32

    def nrm(k, shape, fan_in):
        return jax.random.normal(k, shape, f32) * (fan_in ** -0.5)

    a_c = jax.random.uniform(ks[17], (DEPTH, D_RNN), f32, 0.9, 0.999)
    s_l = a_c ** (1.0 / LRU_C)
    lam = jnp.log(s_l) - jnp.log1p(-s_l)
    return {
        "x": jax.random.normal(ks[0], (BATCH, SEQ, D_MODEL), f32),
        "norm1_w": 1.0 + 0.02 * jax.random.normal(ks[1], (DEPTH, D_MODEL), f32),
        "w_in": nrm(ks[2], (DEPTH, D_MODEL, D_IN), D_MODEL),
        "cmp_pos_k": 0.1 * jax.random.normal(ks[3], (DEPTH, CMP_BLOCK, HEAD_DIM), f32),
        "cmp_pos_v": 0.1 * jax.random.normal(ks[4], (DEPTH, CMP_BLOCK, HEAD_DIM), f32),
        "cmp_k_w1": nrm(ks[5], (DEPTH, CMP_BLOCK * HEAD_DIM, CMP_HIDDEN), CMP_BLOCK * HEAD_DIM),
        "cmp_k_w2": nrm(ks[6], (DEPTH, CMP_HIDDEN, HEAD_DIM), CMP_HIDDEN),
        "cmp_v_w1": nrm(ks[7], (DEPTH, CMP_BLOCK * HEAD_DIM, CMP_HIDDEN), CMP_BLOCK * HEAD_DIM),
        "cmp_v_w2": nrm(ks[8], (DEPTH, CMP_HIDDEN, HEAD_DIM), CMP_HIDDEN),
        "conv_w": nrm(ks[9], (DEPTH, CONV_WIDTH, D_RNN), CONV_WIDTH),
        "conv_b": 0.01 * jax.random.normal(ks[10], (DEPTH, D_RNN), f32),
        "lru_w_a": nrm(ks[11], (DEPTH, N_RNN_BLOCKS, RNN_BLOCK, RNN_BLOCK), RNN_BLOCK),
        "lru_b_a": 0.01 * jax.random.normal(ks[12], (DEPTH, D_RNN), f32),
        "lru_w_i": nrm(ks[13], (DEPTH, N_RNN_BLOCKS, RNN_BLOCK, RNN_BLOCK), RNN_BLOCK),
        "lru_b_i": 0.01 * jax.random.normal(ks[14], (DEPTH, D_RNN), f32),
        "lru_lambda": lam,
        "w_up_attn": nrm(ks[15], (DEPTH, D_ATTN, D_MODEL), D_ATTN),
        "w_up_rnn": nrm(ks[16], (DEPTH, D_RNN, D_MODEL), D_RNN),
        "w_out": nrm(ks[18], (DEPTH, D_MODEL, D_MODEL), D_MODEL),
        "norm2_w": 1.0 + 0.02 * jax.random.normal(ks[19], (DEPTH, D_MODEL), f32),
        "mlp_w1": nrm(ks[20], (DEPTH, D_MODEL, D_FF), D_MODEL),
        "mlp_w2": nrm(ks[21], (DEPTH, D_FF, D_MODEL), D_FF),
        "final_norm_w": 1.0 + 0.02 * jax.random.normal(ks[22], (D_MODEL,), f32),
    }


def reference(x, norm1_w, w_in, cmp_pos_k, cmp_pos_v, cmp_k_w1, cmp_k_w2, cmp_v_w1, cmp_v_w2,
              conv_w, conv_b, lru_w_a, lru_b_a, lru_w_i, lru_b_i, lru_lambda,
              w_up_attn, w_up_rnn, w_out, norm2_w, mlp_w1, mlp_w2, final_norm_w):
    B, S = x.shape[0], x.shape[1]
    for l in range(DEPTH):
        xn = rms_norm(x, norm1_w[l])
        z = xn @ w_in[l]
        (q, k_c, v_c, k_s, v_s, k_w, v_w, g_nsa, xr, gr, g_a, g_b) = jnp.split(z, IN_SPLITS, axis=-1)
        kv_shape = (B, S, N_KV_GROUPS, HEAD_DIM)
        attn = nsa_attention(
            q.reshape(B, S, N_HEADS_A, HEAD_DIM),
            k_c.reshape(kv_shape), v_c.reshape(kv_shape),
            k_s.reshape(kv_shape), v_s.reshape(kv_shape),
            k_w.reshape(kv_shape), v_w.reshape(kv_shape),
            jax.nn.sigmoid(g_nsa).reshape(B, S, N_HEADS_A, 3),
            cmp_pos_k[l], cmp_pos_v[l], cmp_k_w1[l], cmp_k_w2[l], cmp_v_w1[l], cmp_v_w2[l])
        rnn = rg_lru_branch(xr, gr, conv_w[l], conv_b[l], lru_w_a[l], lru_b_a[l],
                            lru_w_i[l], lru_b_i[l], lru_lambda[l])
        merged = jax.nn.sigmoid(g_a) * (attn @ w_up_attn[l]) + jax.nn.sigmoid(g_b) * (rnn @ w_up_rnn[l])
        x = x + merged @ w_out[l]
        hn = rms_norm(x, norm2_w[l])
        x = x + jnp.square(jax.nn.relu(hn @ mlp_w1[l])) @ mlp_w2[l]
    return rms_norm(x, final_norm_w)
```

```python
import functools

import numpy as np
import jax
import jax.numpy as jnp
from jax import lax
from jax.experimental import pallas as pl
from jax.experimental.pallas import tpu as pltpu

F32 = jnp.float32
BF16 = jnp.bfloat16

N_HEADS = 8
HEAD_DIM = 64
N_GROUPS = 2
HEADS_PER_GROUP = N_HEADS // N_GROUPS
D_ATTN = N_HEADS * HEAD_DIM
D_KV = N_GROUPS * HEAD_DIM
CMP_BLOCK = 32
CMP_STRIDE = 16
CMP_HIDDEN = 256
SLC_BLOCK = 64
N_SELECT = 16
N_LOCAL = 2
WINDOW = 512
ATTN_SCALE = HEAD_DIM ** -0.5
NEG = -1e30
FORCE_SCORE = 1e4
N_RNN_BLOCKS = 16
CONV_WIDTH = 4
LRU_C = 8.0
EPS = 1e-6

VMEM_LIMIT_BYTES = 56 * 1024 * 1024
PROJ_ROWS = 256
MLP_ROWS = 256
MLP_FF_CHUNK = 1024
ATTN_Q = 256
ATTN_KV = 256
RNN_ROWS = 256
LRU_PACK = 256
SUBLANES = 8
GATE_LANES = 128


def _dot(a, b):
    return jnp.dot(a, b, preferred_element_type=F32)


def _split(a):
    hi = a.astype(BF16)
    lo = (a - hi.astype(F32)).astype(BF16)
    return hi, lo


def _dot3(ah, al, bh, bl):
    return _dot(ah, bh) + _dot(ah, bl) + _dot(al, bh)


def _params(*sem):
    return pltpu.CompilerParams(dimension_semantics=sem, vmem_limit_bytes=VMEM_LIMIT_BYTES)


def _resident(shape):
    nd = len(shape)
    return pl.BlockSpec(shape, lambda *_: (0,) * nd, pipeline_mode=pl.Buffered(1))


def _norm_proj_kernel(x_ref, nw_ref, wh_ref, wl_ref, *out_refs, layout, n_precise):
    x = x_ref[...]
    var = jnp.mean(x * x, axis=-1, keepdims=True)
    xn = x * lax.rsqrt(var + EPS) * nw_ref[...]
    xh, xl = _split(xn)
    for o_ref, (start, width) in zip(out_refs, layout):
        wh = wh_ref[:, start:start + width]
        acc = _dot(xh, wh)
        if start < n_precise:
            acc = acc + _dot(xl, wh) + _dot(xh, wl_ref[:, start:start + width])
        o_ref[...] = acc.astype(o_ref.dtype)


def _norm_proj(x2, norm_w, wh, wl, layout, dtypes, n_precise):
    t, d = x2.shape
    grid = (t // PROJ_ROWS,)
    out_shape = [jax.ShapeDtypeStruct((t, w), dt) for (_, w), dt in zip(layout, dtypes)]
    out_specs = [pl.BlockSpec((PROJ_ROWS, w), lambda i: (i, 0)) for (_, w) in layout]
    return pl.pallas_call(
        functools.partial(_norm_proj_kernel, layout=tuple(layout), n_precise=n_precise),
        grid=grid,
        in_specs=[
            pl.BlockSpec((PROJ_ROWS, d), lambda i: (i, 0)),
            _resident(norm_w.shape),
            _resident(wh.shape),
            _resident(wl.shape),
        ],
        out_specs=out_specs,
        out_shape=out_shape,
        compiler_params=_params("parallel"),
        name="norm_proj",
    )(x2, norm_w, wh, wl)


def _compress_kernel(x_ref, pos_ref, w1h_ref, w1l_ref, w2h_ref, w2l_ref, c_ref, ct_ref):
    x = x_ref[0, 0, 0]
    ns = x.shape[0]
    half = x.shape[1]
    xh, xl = _split(x)
    w1h = w1h_ref[0]
    w1l = w1l_ref[0]
    first = _dot3(xh, xl, w1h[:half], w1l[:half])
    second = _dot3(xh, xl, w1h[half:], w1l[half:])
    ph, plo = _split(pos_ref[0])
    posb = _dot3(ph, plo, w1h, w1l)[0:1]
    pre = first + pltpu.roll(second, ns - 1, 0) + posb
    hid = jax.nn.gelu(pre)
    hh, hl = _split(hid)
    c = _dot3(hh, hl, w2h_ref[0], w2l_ref[0])
    row = lax.broadcasted_iota(jnp.int32, c.shape, 0)
    c = jnp.where(row < ns - 1, c, 0.0)
    c_ref[0, 0, 0] = c
    wide = jnp.concatenate([c, jnp.zeros_like(c)], axis=1)
    ct_ref[0, 0, 0] = wide.T[:HEAD_DIM]


def _compress(xkv, pos, w1h, w1l, w2h, w2l):
    _, b, g, ns, width = xkv.shape
    grid = (2, b, g)
    return pl.pallas_call(
        _compress_kernel,
        grid=grid,
        in_specs=[
            pl.BlockSpec((1, 1, 1, ns, width), lambda s, i, j: (s, i, j, 0, 0)),
            pl.BlockSpec((1, SUBLANES, 2 * width), lambda s, i, j: (s, 0, 0)),
            pl.BlockSpec((1, 2 * width, CMP_HIDDEN), lambda s, i, j: (s, 0, 0)),
            pl.BlockSpec((1, 2 * width, CMP_HIDDEN), lambda s, i, j: (s, 0, 0)),
            pl.BlockSpec((1, CMP_HIDDEN, HEAD_DIM), lambda s, i, j: (s, 0, 0)),
            pl.BlockSpec((1, CMP_HIDDEN, HEAD_DIM), lambda s, i, j: (s, 0, 0)),
        ],
        out_specs=[
            pl.BlockSpec((1, 1, 1, ns, HEAD_DIM), lambda s, i, j: (s, i, j, 0, 0)),
            pl.BlockSpec((1, 1, 1, HEAD_DIM, ns), lambda s, i, j: (s, i, j, 0, 0)),
        ],
        out_shape=[
            jax.ShapeDtypeStruct((2, b, g, ns, HEAD_DIM), F32),
            jax.ShapeDtypeStruct((2, b, g, HEAD_DIM, ns), F32),
        ],
        compiler_params=_params("parallel", "parallel", "parallel"),
        name="compress",
    )(xkv, pos, w1h, w1l, w2h, w2l)


def _importance_matrix(nc_pad, nb):
    r_s = SLC_BLOCK // CMP_STRIDE
    r_c = CMP_BLOCK // CMP_STRIDE
    mat = np.zeros((nb, nc_pad), np.float32)
    for j in range(nb):
        for m in range(r_s):
            for n in range(r_c):
                i = r_s * j - m - n
                if i >= 0:
                    mat[j, i] += 1.0
    return mat


def _select_kernel(q_ref, kc_ref, vct_ref, mt_ref, oc_ref, selb_ref):
    qi = pl.program_id(2)
    tq = q_ref.shape[1]
    ncp = kc_ref.shape[3]
    nb = mt_ref.shape[0]
    qt = q_ref[0].T
    kch, kcl = _split(kc_ref[0, 0, 0])
    vct = vct_ref[0, 0, 0].astype(BF16)
    t = qi * tq + lax.broadcasted_iota(jnp.int32, (1, tq), 1)
    ci = lax.broadcasted_iota(jnp.int32, (ncp, 1), 0)
    mask = (ci * CMP_STRIDE + (CMP_BLOCK - 1)) <= t
    ps = jnp.zeros((ncp, tq), F32)
    for r in range(HEADS_PER_GROUP):
        qh, ql = _split(qt[r * HEAD_DIM:(r + 1) * HEAD_DIM])
        s = _dot(kch, qh) + _dot(kch, ql) + _dot(kcl, qh)
        s = jnp.where(mask, s, NEG)
        m = jnp.max(s, axis=0, keepdims=True)
        e = jnp.where(mask, jnp.exp(s - m), 0.0)
        den = jnp.maximum(jnp.sum(e, axis=0, keepdims=True), 1e-20)
        p = e * (1.0 / den)
        ps = ps + p
        oc_ref[0, r * HEAD_DIM:(r + 1) * HEAD_DIM, :] = _dot(vct, p.astype(BF16))
    p1 = ps.astype(BF16)
    r1 = ps - p1.astype(F32)
    p2 = r1.astype(BF16)
    p3 = (r1 - p2.astype(F32)).astype(BF16)
    mt = mt_ref[...]
    imp = _dot(mt, p1) + _dot(mt, p2) + _dot(mt, p3)
    blk = lax.broadcasted_iota(jnp.int32, (nb, 1), 0)
    cur = t // SLC_BLOCK
    valid = blk <= cur
    forced = (blk == 0) | (blk > cur - N_LOCAL)
    score = jnp.where(valid, jnp.where(forced, FORCE_SCORE, imp), NEG)
    rank = jnp.zeros((nb, tq), jnp.int32)
    for i in range(nb):
        si = score[i:i + 1, :]
        ge = jnp.where(si >= score, 1, 0)
        gt = jnp.where(si > score, 1, 0)
        rank = rank + jnp.where(blk > i, ge, gt)
    sel = valid & (rank < N_SELECT)
    selb_ref[0, 0] = jnp.where(sel, 0.0, NEG)


def _select(q, kc, vct, mt):
    b, s, _ = q.shape
    g = N_GROUPS
    ncp = kc.shape[3]
    nb = mt.shape[0]
    gw = HEADS_PER_GROUP * HEAD_DIM
    grid = (b, g, s // ATTN_Q)
    return pl.pallas_call(
        _select_kernel,
        grid=grid,
        in_specs=[
            pl.BlockSpec((1, ATTN_Q, gw), lambda i, j, k: (i, k, j)),
            pl.BlockSpec((1, 1, 1, ncp, HEAD_DIM), lambda i, j, k: (0, i, j, 0, 0)),
            pl.BlockSpec((1, 1, 1, HEAD_DIM, ncp), lambda i, j, k: (1, i, j, 0, 0)),
            pl.BlockSpec((nb, ncp), lambda i, j, k: (0, 0)),
        ],
        out_specs=[
            pl.BlockSpec((1, gw, ATTN_Q), lambda i, j, k: (i, j, k)),
            pl.BlockSpec((1, 1, nb, ATTN_Q), lambda i, j, k: (i, j, 0, k)),
        ],
        out_shape=[
            jax.ShapeDtypeStruct((b, D_ATTN, s), F32),
            jax.ShapeDtypeStruct((b, g, nb, s), F32),
        ],
        compiler_params=_params("parallel", "parallel", "parallel"),
        name="cmp_select",
    )(q, kc, vct, mt)


def _attend_tile(k_t, v_t, qt, bias, m_ref, l_ref, acc_ref):
    for r in range(HEADS_PER_GROUP):
        s = _dot(k_t, qt[r * HEAD_DIM:(r + 1) * HEAD_DIM])
        if bias is not None:
            s = s + bias
        m_old = m_ref[r]
        m_new = jnp.maximum(m_old, jnp.max(s, axis=0, keepdims=True))
        alpha = jnp.exp(m_old - m_new)
        p = jnp.exp(s - m_new)
        l_ref[r] = alpha * l_ref[r] + jnp.sum(p, axis=0, keepdims=True)
        pv = lax.dot_general(v_t, p.astype(BF16), (((0,), (0,)), ((), ())),
                             preferred_element_type=F32)
        acc_ref[r] = alpha * acc_ref[r] + pv
        m_ref[r] = m_new


def _attn_kernel(q_ref, ks_ref, vs_ref, kw_ref, vw_ref, selb_ref, g_ref, oc_ref, o_ref,
                 m_s, l_s, acc_s, m_w, l_w, acc_w):
    qi = pl.program_id(2)
    tq = q_ref.shape[1]
    kv = ATTN_KV
    blocks_per_tile = kv // SLC_BLOCK
    qt = q_ref[0].T.astype(BF16)

    for ref in (m_s, m_w):
        ref[...] = jnp.full(ref.shape, NEG, F32)
    for ref in (l_s, l_w, acc_s, acc_w):
        ref[...] = jnp.zeros(ref.shape, F32)

    row = lax.broadcasted_iota(jnp.int32, (kv, tq), 0)
    lane = lax.broadcasted_iota(jnp.int32, (kv, tq), 1)
    causal = jnp.where(row <= lane, 0.0, NEG)
    far = jnp.where(lane < row, 0.0, NEG)

    def sel_bias(j):
        rows = selb_ref[0, 0, j]
        return jnp.concatenate(
            [jnp.broadcast_to(rows[i:i + 1], (SLC_BLOCK, tq)) for i in range(blocks_per_tile)], axis=0)

    def tile(ref, j):
        return ref[0, 0, pl.ds(pl.multiple_of(j * kv, kv), kv), :]

    def body(j, carry):
        _attend_tile(tile(ks_ref, j), tile(vs_ref, j), qt, sel_bias(j), m_s, l_s, acc_s)
        return carry

    lax.fori_loop(0, qi, body, 0)
    _attend_tile(tile(ks_ref, qi), tile(vs_ref, qi), qt, sel_bias(qi) + causal, m_s, l_s, acc_s)

    _attend_tile(tile(kw_ref, qi), tile(vw_ref, qi), qt, causal, m_w, l_w, acc_w)

    @pl.when(qi >= 1)
    def _():
        _attend_tile(tile(kw_ref, qi - 1), tile(vw_ref, qi - 1), qt, None, m_w, l_w, acc_w)

    @pl.when(qi >= 2)
    def _():
        _attend_tile(tile(kw_ref, qi - 2), tile(vw_ref, qi - 2), qt, far, m_w, l_w, acc_w)

    gt = jax.nn.sigmoid(g_ref[0]).T
    outs = []
    for r in range(HEADS_PER_GROUP):
        o_cmp = oc_ref[0, r * HEAD_DIM:(r + 1) * HEAD_DIM, :]
        o_slc = acc_s[r] * (1.0 / l_s[r])
        o_win = acc_w[r] * (1.0 / l_w[r])
        outs.append(gt[3 * r:3 * r + 1] * o_cmp + gt[3 * r + 1:3 * r + 2] * o_slc
                    + gt[3 * r + 2:3 * r + 3] * o_win)
    o_ref[0] = jnp.concatenate(outs, axis=0).T.astype(o_ref.dtype)


def _attention(q, ks, vs, kw, vw, selb, gates, oc):
    b, s, _ = q.shape
    g = N_GROUPS
    gw = HEADS_PER_GROUP * HEAD_DIM
    tq = ATTN_Q
    n_tiles = s // ATTN_KV
    bpt = ATTN_KV // SLC_BLOCK
    selb5 = selb.reshape(b, g, n_tiles, bpt, s)
    kv_spec = pl.BlockSpec((1, 1, s, HEAD_DIM), lambda i, j, k: (i, j, 0, 0))
    return pl.pallas_call(
        _attn_kernel,
        grid=(b, g, s // tq),
        in_specs=[
            pl.BlockSpec((1, tq, gw), lambda i, j, k: (i, k, j)),
            kv_spec, kv_spec, kv_spec, kv_spec,
            pl.BlockSpec((1, 1, n_tiles, bpt, tq), lambda i, j, k: (i, j, 0, 0, k)),
            pl.BlockSpec((1, tq, GATE_LANES), lambda i, j, k: (i, k, j)),
            pl.BlockSpec((1, gw, tq), lambda i, j, k: (i, j, k)),
        ],
        out_specs=pl.BlockSpec((1, tq, gw), lambda i, j, k: (i, k, j)),
        out_shape=jax.ShapeDtypeStruct((b, s, D_ATTN), BF16),
        scratch_shapes=[
            pltpu.VMEM((HEADS_PER_GROUP, 1, tq), F32),
            pltpu.VMEM((HEADS_PER_GROUP, 1, tq), F32),
            pltpu.VMEM((HEADS_PER_GROUP, HEAD_DIM, tq), F32),
            pltpu.VMEM((HEADS_PER_GROUP, 1, tq), F32),
            pltpu.VMEM((HEADS_PER_GROUP, 1, tq), F32),
            pltpu.VMEM((HEADS_PER_GROUP, HEAD_DIM, tq), F32),
        ],
        compiler_params=_params("parallel", "parallel", "arbitrary"),
        name="slc_win_attn",
    )(q, ks, vs, kw, vw, selb5, gates, oc)


def _rnn_kernel(xr_ref, gr_ref, cw_ref, cb_ref, wa_ref, wi_ref, ba_ref, bi_ref, lam_ref, o_ref,
                xprev, hcarry):
    si = pl.program_id(1)
    ts = xr_ref.shape[1]
    d = xr_ref.shape[2]

    @pl.when(si == 0)
    def _():
        xprev[...] = jnp.zeros(xprev.shape, F32)
        hcarry[...] = jnp.zeros(hcarry.shape, F32)

    x = xr_ref[0]
    xm = jnp.concatenate([xprev[...], x], axis=0)
    xc = cb_ref[...]
    for i in range(CONV_WIDTH):
        back = CONV_WIDTH - 1 - i
        xc = xc + xm[SUBLANES - back:SUBLANES - back + ts] * cw_ref[i:i + 1, :]
    xprev[...] = x[ts - SUBLANES:]

    xcb = xc.astype(BF16)
    n_pack = d // LRU_PACK
    ra = jnp.concatenate(
        [_dot(xcb[:, c * LRU_PACK:(c + 1) * LRU_PACK], wa_ref[c]) for c in range(n_pack)], axis=1)
    ri = jnp.concatenate(
        [_dot(xcb[:, c * LRU_PACK:(c + 1) * LRU_PACK], wi_ref[c]) for c in range(n_pack)], axis=1)
    r = jax.nn.sigmoid(ra + ba_ref[...])
    ig = jax.nn.sigmoid(ri + bi_ref[...])
    log_a = -LRU_C * r * jax.nn.softplus(-lam_ref[...])
    a = jnp.exp(log_a)
    bt = jnp.sqrt(-jnp.tanh(log_a) * (a * a + 1.0)) * (ig * xc)

    ng = ts // SUBLANES
    a3 = a.reshape(ng, SUBLANES, d)
    b3 = bt.reshape(ng, SUBLANES, d)
    sub = lax.broadcasted_iota(jnp.int32, (ng, SUBLANES, d), 1)
    step = 1
    while step < SUBLANES:
        keep = sub >= step
        a_prev = jnp.where(keep, pltpu.roll(a3, step, 1), 1.0)
        b_prev = jnp.where(keep, pltpu.roll(b3, step, 1), 0.0)
        b3 = b3 + a3 * b_prev
        a3 = a3 * a_prev
        step *= 2
    h = hcarry[0:1, :]
    hs = []
    for k in range(ng):
        hk = b3[k] + a3[k] * h
        hs.append(hk)
        h = hk[SUBLANES - 1:SUBLANES]
    hcarry[0:1, :] = h
    hfull = jnp.concatenate(hs, axis=0)
    o_ref[0] = (hfull * jax.nn.gelu(gr_ref[0])).astype(o_ref.dtype)


def _rnn(xr, gr, cw, cb, wa, wi, ba, bi, lam):
    b, s, d = xr.shape
    ts = RNN_ROWS
    n_pack = d // LRU_PACK
    row = lambda a: _resident(a.shape)
    return pl.pallas_call(
        _rnn_kernel,
        grid=(b, s // ts),
        in_specs=[
            pl.BlockSpec((1, ts, d), lambda i, j: (i, j, 0)),
            pl.BlockSpec((1, ts, d), lambda i, j: (i, j, 0)),
            row(cw), row(cb), row(wa), row(wi), row(ba), row(bi), row(lam),
        ],
        out_specs=pl.BlockSpec((1, ts, d), lambda i, j: (i, j, 0)),
        out_shape=jax.ShapeDtypeStruct((b, s, d), BF16),
        scratch_shapes=[pltpu.VMEM((SUBLANES, d), F32), pltpu.VMEM((SUBLANES, d), F32)],
        compiler_params=_params("parallel", "arbitrary"),
        name="rg_lru",
    )(xr, gr, cw, cb, wa, wi, ba, bi, lam)


def _merge_mlp_kernel(x_ref, attn_ref, rnn_ref, ga_ref, gb_ref, wua_ref, wur_ref, wo_ref,
                      n2_ref, w1_ref, w2_ref, fn_ref, o_ref, *, final):
    up_a = _dot(attn_ref[...], wua_ref[...])
    up_r = _dot(rnn_ref[...], wur_ref[...])
    merged = jax.nn.sigmoid(ga_ref[...]) * up_a + jax.nn.sigmoid(gb_ref[...]) * up_r
    x = x_ref[...] + _dot(merged.astype(BF16), wo_ref[...])
    var = jnp.mean(x * x, axis=-1, keepdims=True)
    hn = (x * lax.rsqrt(var + EPS) * n2_ref[...]).astype(BF16)
    d_ff = w1_ref.shape[1]
    acc = jnp.zeros_like(x)
    for c in range(d_ff // MLP_FF_CHUNK):
        lo, hi = c * MLP_FF_CHUNK, (c + 1) * MLP_FF_CHUNK
        h1 = jnp.maximum(_dot(hn, w1_ref[:, lo:hi]), 0.0)
        acc = acc + _dot((h1 * h1).astype(BF16), w2_ref[lo:hi, :])
    x = x + acc
    if final:
        var = jnp.mean(x * x, axis=-1, keepdims=True)
        x = x * lax.rsqrt(var + EPS) * fn_ref[...]
    o_ref[...] = x


def _merge_mlp(x2, attn, rnn, ga, gb, wua, wur, wo, n2, w1, w2, fn, final):
    t, d = x2.shape
    tm = MLP_ROWS
    rows = lambda w: pl.BlockSpec((tm, w), lambda i: (i, 0))
    return pl.pallas_call(
        functools.partial(_merge_mlp_kernel, final=final),
        grid=(t // tm,),
        in_specs=[
            rows(d), rows(attn.shape[1]), rows(rnn.shape[1]), rows(d), rows(d),
            _resident(wua.shape), _resident(wur.shape), _resident(wo.shape),
            _resident(n2.shape), _resident(w1.shape), _resident(w2.shape), _resident(fn.shape),
        ],
        out_specs=rows(d),
        out_shape=jax.ShapeDtypeStruct((t, d), F32),
        compiler_params=_params("parallel"),
        name="merge_mlp",
    )(x2, attn, rnn, ga, gb, wua, wur, wo, n2, w1, w2, fn)


def _block_diag_pack(w, pack):
    nblk, k, _ = w.shape
    per = pack // k
    w4 = w.reshape(nblk // per, per, k, k)
    eye = jnp.eye(per, dtype=w.dtype)
    out = jnp.einsum("cpij,pq->cpiqj", w4, eye)
    return out.reshape(nblk // per, pack, pack)


def _in_proj_weights(w_in):
    d = w_in.shape[0]
    sizes = (D_ATTN, D_KV, D_KV, D_KV, D_KV, D_KV, D_KV, N_HEADS * 3, d, d, d, d)
    offs = np.concatenate([[0], np.cumsum(sizes)])
    col = lambda i: w_in[:, offs[i]:offs[i + 1]]
    q, k_c, v_c, k_s, v_s, k_w, v_w, g_nsa, xr, gr, g_a, g_b = [col(i) for i in range(12)]
    per_group = HEADS_PER_GROUP * 3
    gates = jnp.concatenate(
        [jnp.pad(g_nsa[:, g * per_group:(g + 1) * per_group], ((0, 0), (0, GATE_LANES - per_group)))
         for g in range(N_GROUPS)], axis=1)
    precise = [q * ATTN_SCALE, gates, k_c]
    fast = [v_c, k_s, v_s, k_w, v_w, xr, gr, g_a, g_b]
    dtypes = [F32, F32, F32, F32, BF16, BF16, BF16, BF16, F32, F32, F32, F32]
    wcat = jnp.concatenate(precise + fast, axis=1)
    n_precise = sum(p.shape[1] for p in precise)
    wh = wcat.astype(BF16)
    wl = (wcat[:, :n_precise] - wh[:, :n_precise].astype(F32)).astype(BF16)
    layout, start = [], 0
    for p in precise + fast:
        layout.append((start, p.shape[1]))
        start += p.shape[1]
    return wh, wl, layout, dtypes, n_precise


def _by_group(a, b, s):
    return a.reshape(b, s, N_GROUPS, HEAD_DIM).transpose(0, 2, 1, 3)


def kernel(x, norm1_w, w_in, cmp_pos_k, cmp_pos_v, cmp_k_w1, cmp_k_w2, cmp_v_w1, cmp_v_w2, conv_w, conv_b, lru_w_a, lru_b_a, lru_w_i, lru_b_i, lru_lambda, w_up_attn, w_up_rnn, w_out, norm2_w, mlp_w1, mlp_w2, final_norm_w):
    b, s, d = x.shape
    depth = w_in.shape[0]
    t = b * s
    ns = s // CMP_STRIDE
    nb = s // SLC_BLOCK
    assert s % ATTN_Q == 0 and s % RNN_ROWS == 0 and t % PROJ_ROWS == 0 and t % MLP_ROWS == 0
    assert ATTN_Q == ATTN_KV and WINDOW == 2 * ATTN_KV and nb >= N_SELECT
    mt = jnp.asarray(_importance_matrix(ns, nb), BF16)
    x2 = x.reshape(t, d)
    fn = final_norm_w.reshape(1, d)
    for l in range(depth):
        wh, wl, layout, dtypes, n_precise = _in_proj_weights(w_in[l])
        (q, gates, k_c, v_c, k_s, v_s, k_w, v_w, xr, gr, g_a, g_b) = _norm_proj(
            x2, norm1_w[l].reshape(1, d), wh, wl, layout, dtypes, n_precise)

        xkv = jnp.stack([_by_group(k_c, b, s), _by_group(v_c, b, s)])
        xkv = xkv.reshape(2, b, N_GROUPS, ns, CMP_STRIDE * HEAD_DIM)
        pos = jnp.stack([cmp_pos_k[l], cmp_pos_v[l]]).reshape(2, 1, CMP_BLOCK * HEAD_DIM)
        pos = jnp.broadcast_to(pos, (2, SUBLANES, CMP_BLOCK * HEAD_DIM))
        w1 = jnp.stack([cmp_k_w1[l], cmp_v_w1[l]])
        w2 = jnp.stack([cmp_k_w2[l], cmp_v_w2[l]])
        w1h, w1l = _split(w1)
        w2h, w2l = _split(w2)
        kvc, kvct = _compress(xkv, pos, w1h, w1l, w2h, w2l)

        q3 = q.reshape(b, s, D_ATTN)
        oc, selb = _select(q3, kvc, kvct, mt)
        attn = _attention(q3, _by_group(k_s, b, s), _by_group(v_s, b, s), _by_group(k_w, b, s),
                          _by_group(v_w, b, s), selb, gates.reshape(b, s, N_GROUPS * GATE_LANES), oc)

        rnn = _rnn(xr.reshape(b, s, d), gr.reshape(b, s, d),
                   jnp.pad(conv_w[l], ((0, SUBLANES - CONV_WIDTH), (0, 0))), conv_b[l].reshape(1, d),
                   _block_diag_pack(lru_w_a[l], LRU_PACK).astype(BF16),
                   _block_diag_pack(lru_w_i[l], LRU_PACK).astype(BF16),
                   lru_b_a[l].reshape(1, d), lru_b_i[l].reshape(1, d), lru_lambda[l].reshape(1, d))

        x2 = _merge_mlp(x2, attn.reshape(t, D_ATTN), rnn.reshape(t, d), g_a, g_b,
                        w_up_attn[l].astype(BF16), w_up_rnn[l].astype(BF16), w_out[l].astype(BF16),
                        norm2_w[l].reshape(1, d), mlp_w1[l].astype(BF16), mlp_w2[l].astype(BF16),
                        fn, final=(l == depth - 1))
    return x2.reshape(b, s, d)
```

```python
import functools

import numpy as np
import jax
import jax.numpy as jnp
from jax import lax
from jax.experimental import pallas as pl
from jax.experimental.pallas import tpu as pltpu

F32 = jnp.float32
BF16 = jnp.bfloat16

N_HEADS = 8
HEAD_DIM = 64
N_GROUPS = 2
HEADS_PER_GROUP = N_HEADS // N_GROUPS
D_ATTN = N_HEADS * HEAD_DIM
D_KV = N_GROUPS * HEAD_DIM
CMP_BLOCK = 32
CMP_STRIDE = 16
CMP_HIDDEN = 256
SLC_BLOCK = 64
N_SELECT = 16
N_LOCAL = 2
WINDOW = 512
ATTN_SCALE = HEAD_DIM ** -0.5
NEG = -1e30
FORCE_SCORE = 1e4
N_RNN_BLOCKS = 16
CONV_WIDTH = 4
LRU_C = 8.0
EPS = 1e-6
LOG2_E = 1.4426950408889634

VMEM_LIMIT_BYTES = 56 * 1024 * 1024
PROJ_ROWS = 256
MLP_ROWS = 256
MLP_FF_CHUNK = 1024
ATTN_Q = 256
ATTN_KV = 256
RNN_ROWS = 256
LRU_PACK = 256
SUBLANES = 8
GATE_LANES = 128
PAT_ZERO, PAT_CAUSAL, PAT_FAR, PAT_NONE = 0, 1, 2, 3
SEL_ROWS = 16
K_AUG = 2 * HEAD_DIM
V_AUG = HEAD_DIM + 16


def _dot(a, b):
    return jnp.dot(a, b, preferred_element_type=F32)


def _sigmoid(x):
    return 0.5 * jnp.tanh(0.5 * x) + 0.5


def _split(a):
    hi = a.astype(BF16)
    lo = (a - hi.astype(F32)).astype(BF16)
    return hi, lo


def _dot3(ah, al, bh, bl):
    return _dot(ah, bh) + _dot(ah, bl) + _dot(al, bh)


def _params(*sem):
    return pltpu.CompilerParams(dimension_semantics=sem, vmem_limit_bytes=VMEM_LIMIT_BYTES)


def _resident(shape):
    nd = len(shape)
    return pl.BlockSpec(shape, lambda *_: (0,) * nd, pipeline_mode=pl.Buffered(1))


def _norm_proj_kernel(x_ref, nw_ref, wh_ref, wl_ref, *out_refs, layout, n_precise):
    x = x_ref[...]
    var = jnp.mean(x * x, axis=-1, keepdims=True)
    xn = x * lax.rsqrt(var + EPS) * nw_ref[...]
    xh, xl = _split(xn)
    for o_ref, (start, width) in zip(out_refs, layout):
        wh = wh_ref[:, start:start + width]
        acc = _dot(xh, wh)
        if start < n_precise:
            acc = acc + _dot(xl, wh) + _dot(xh, wl_ref[:, start:start + width])
        o_ref[...] = acc.astype(o_ref.dtype)


def _norm_proj(x2, norm_w, wh, wl, layout, dtypes, n_precise):
    t, d = x2.shape
    grid = (t // PROJ_ROWS,)
    out_shape = [jax.ShapeDtypeStruct((t, w), dt) for (_, w), dt in zip(layout, dtypes)]
    out_specs = [pl.BlockSpec((PROJ_ROWS, w), lambda i: (i, 0)) for (_, w) in layout]
    return pl.pallas_call(
        functools.partial(_norm_proj_kernel, layout=tuple(layout), n_precise=n_precise),
        grid=grid,
        in_specs=[
            pl.BlockSpec((PROJ_ROWS, d), lambda i: (i, 0)),
            _resident(norm_w.shape),
            _resident(wh.shape),
            _resident(wl.shape),
        ],
        out_specs=out_specs,
        out_shape=out_shape,
        compiler_params=_params("parallel"),
        name="norm_proj",
    )(x2, norm_w, wh, wl)


def _compress_kernel(x_ref, pos_ref, w1h_ref, w1l_ref, w2h_ref, w2l_ref, c3_ref, ct_ref):
    x = x_ref[0, 0, 0]
    ns = x.shape[0]
    half = x.shape[1]
    xh, xl = _split(x)
    w1h = w1h_ref[0]
    w1l = w1l_ref[0]
    first = _dot3(xh, xl, w1h[:half], w1l[:half])
    second = _dot3(xh, xl, w1h[half:], w1l[half:])
    ph, plo = _split(pos_ref[0])
    posb = _dot3(ph, plo, w1h, w1l)[0:1]
    pre = first + pltpu.roll(second, ns - 1, 0) + posb
    hid = jax.nn.gelu(pre)
    hh, hl = _split(hid)
    c = _dot3(hh, hl, w2h_ref[0], w2l_ref[0])
    row = lax.broadcasted_iota(jnp.int32, c.shape, 0)
    c = jnp.where(row < ns - 1, c, 0.0)
    ch, cl = _split(c)
    c3_ref[0, 0, 0] = jnp.concatenate([ch, ch, cl, jnp.zeros_like(ch)], axis=1)
    wide = jnp.concatenate([c, jnp.zeros_like(c)], axis=1)
    ct_ref[0, 0, 0] = wide.T[:HEAD_DIM].astype(ct_ref.dtype)


def _compress(xkv, pos, w1h, w1l, w2h, w2l):
    _, b, g, ns, width = xkv.shape
    grid = (2, b, g)
    return pl.pallas_call(
        _compress_kernel,
        grid=grid,
        in_specs=[
            pl.BlockSpec((1, 1, 1, ns, width), lambda s, i, j: (s, i, j, 0, 0)),
            pl.BlockSpec((1, SUBLANES, 2 * width), lambda s, i, j: (s, 0, 0)),
            pl.BlockSpec((1, 2 * width, CMP_HIDDEN), lambda s, i, j: (s, 0, 0)),
            pl.BlockSpec((1, 2 * width, CMP_HIDDEN), lambda s, i, j: (s, 0, 0)),
            pl.BlockSpec((1, CMP_HIDDEN, HEAD_DIM), lambda s, i, j: (s, 0, 0)),
            pl.BlockSpec((1, CMP_HIDDEN, HEAD_DIM), lambda s, i, j: (s, 0, 0)),
        ],
        out_specs=[
            pl.BlockSpec((1, 1, 1, ns, 4 * HEAD_DIM), lambda s, i, j: (s, i, j, 0, 0)),
            pl.BlockSpec((1, 1, 1, HEAD_DIM, ns), lambda s, i, j: (s, i, j, 0, 0)),
        ],
        out_shape=[
            jax.ShapeDtypeStruct((2, b, g, ns, 4 * HEAD_DIM), BF16),
            jax.ShapeDtypeStruct((2, b, g, HEAD_DIM, ns), BF16),
        ],
        compiler_params=_params("parallel", "parallel", "parallel"),
        name="compress",
    )(xkv, pos, w1h, w1l, w2h, w2l)


def _importance_matrix(nc_pad, nb):
    r_s = SLC_BLOCK // CMP_STRIDE
    r_c = CMP_BLOCK // CMP_STRIDE
    mat = np.zeros((nb, nc_pad), np.float32)
    for j in range(nb):
        for m in range(r_s):
            for n in range(r_c):
                i = r_s * j - m - n
                if i >= 0:
                    mat[j, i] += 1.0
    return mat


def _select_kernel(q_ref, kc3_ref, vct_ref, mt_ref, oc_ref, selx_ref):
    qi = pl.program_id(2)
    tq = q_ref.shape[1]
    ncp = kc3_ref.shape[3]
    nb = mt_ref.shape[0]
    nh = HEADS_PER_GROUP
    bpt = ATTN_KV // SLC_BLOCK

    def lanes4(x):
        return jnp.concatenate([x] * nh, axis=1)

    qt4 = q_ref[0].T
    qt = jnp.concatenate([qt4[r * HEAD_DIM:(r + 1) * HEAD_DIM] for r in range(nh)], axis=1)
    qh, ql = _split(qt)
    rhs = jnp.concatenate([qh, ql, qh, jnp.zeros_like(qh)], axis=0)
    s = _dot(kc3_ref[0, 0, 0], rhs)
    t1 = qi * tq + lax.broadcasted_iota(jnp.int32, (1, tq), 1)
    ci = lax.broadcasted_iota(jnp.int32, (ncp, 1), 0)
    mask = lanes4((ci * CMP_STRIDE + (CMP_BLOCK - 1)) <= t1)
    s = jnp.where(mask, s, NEG)
    m = jnp.max(s, axis=0, keepdims=True)
    e = jnp.exp2(s - m)
    den = jnp.sum(e, axis=0, keepdims=True)
    inv = jnp.where(m > 0.5 * NEG, 1.0 / den, 0.0)
    p = e * inv
    oc = _dot(vct_ref[0, 0, 0], p.astype(BF16))
    ps = p[:, 0:tq]
    for r in range(1, nh):
        ps = ps + p[:, r * tq:(r + 1) * tq]
        oc_ref[0, r * HEAD_DIM:(r + 1) * HEAD_DIM, :] = oc[:, r * tq:(r + 1) * tq]
    oc_ref[0, 0:HEAD_DIM, :] = oc[:, 0:tq]

    blk = lax.broadcasted_iota(jnp.int32, (nb, 1), 0)
    cur = t1 // SLC_BLOCK
    valid = blk <= cur

    def store_rows(selb):
        for jt in range(nb // bpt):
            rows = jnp.concatenate([selb[jt * bpt:(jt + 1) * bpt], jnp.zeros((SEL_ROWS - bpt, tq), F32)], axis=0)
            selx_ref[0, 0, jt] = rows.astype(selx_ref.dtype)
        selx_ref[0, 0, nb // bpt] = jnp.zeros((SEL_ROWS, tq), selx_ref.dtype)

    need_rank = (qi + 1) * tq > N_SELECT * SLC_BLOCK

    @pl.when(jnp.logical_not(need_rank))
    def _():
        store_rows(jnp.where(valid, 0.0, NEG))

    @pl.when(need_rank)
    def _():
        p1 = ps.astype(BF16)
        r1 = ps - p1.astype(F32)
        p2 = r1.astype(BF16)
        p3 = (r1 - p2.astype(F32)).astype(BF16)
        mt = mt_ref[...]
        imp = _dot(mt, p1) + _dot(mt, p2) + _dot(mt, p3)
        forced = (blk == 0) | (blk > cur - N_LOCAL)
        score = jnp.where(valid, jnp.where(forced, FORCE_SCORE, imp), NEG)
        nchunk = nb // SUBLANES
        chunks = [score[SUBLANES * v:SUBLANES * (v + 1)] for v in range(nchunk)]
        ranks = [jnp.zeros((SUBLANES, tq), jnp.int32) for _ in range(nchunk)]
        sub = lax.broadcasted_iota(jnp.int32, (SUBLANES, 1), 0)
        for i in range(nb):
            si = score[i:i + 1, :]
            vi = i // SUBLANES
            for v in range(nchunk):
                if v > vi:
                    beats = jnp.where(si >= chunks[v], 1, 0)
                elif v < vi:
                    beats = jnp.where(si > chunks[v], 1, 0)
                else:
                    beats = jnp.where(sub > i % SUBLANES, jnp.where(si >= chunks[v], 1, 0),
                                      jnp.where(si > chunks[v], 1, 0))
                ranks[v] = ranks[v] + beats
        rank = jnp.concatenate(ranks, axis=0)
        sel = valid & (rank < N_SELECT)
        store_rows(jnp.where(sel, 0.0, NEG))


def _select(q, kc3, vct, mt):
    b, s, _ = q.shape
    g = N_GROUPS
    ncp = kc3.shape[3]
    nb = mt.shape[0]
    gw = HEADS_PER_GROUP * HEAD_DIM
    n_sel = nb // (ATTN_KV // SLC_BLOCK) + 1
    grid = (b, g, s // ATTN_Q)
    return pl.pallas_call(
        _select_kernel,
        grid=grid,
        in_specs=[
            pl.BlockSpec((1, ATTN_Q, gw), lambda i, j, k: (i, k, j)),
            pl.BlockSpec((1, 1, 1, ncp, 4 * HEAD_DIM), lambda i, j, k: (0, i, j, 0, 0)),
            pl.BlockSpec((1, 1, 1, HEAD_DIM, ncp), lambda i, j, k: (1, i, j, 0, 0)),
            pl.BlockSpec((nb, ncp), lambda i, j, k: (0, 0)),
        ],
        out_specs=[
            pl.BlockSpec((1, gw, ATTN_Q), lambda i, j, k: (i, j, k)),
            pl.BlockSpec((1, 1, n_sel, SEL_ROWS, ATTN_Q), lambda i, j, k: (i, j, 0, 0, k)),
        ],
        out_shape=[
            jax.ShapeDtypeStruct((b, D_ATTN, s), F32),
            jax.ShapeDtypeStruct((b, g, n_sel, SEL_ROWS, s), BF16),
        ],
        compiler_params=_params("parallel", "parallel", "parallel"),
        name="cmp_select",
    )(q, kc3, vct, mt)


def _attn_kernel(q_ref, kk_ref, vv_ref, selx_ref, g_ref, oc_ref, o_ref,
                 m_ref, acc_ref, s_a, s_b, smax_ref, pat_ref):
    qi = pl.program_id(2)
    tq = q_ref.shape[1]
    kv = ATTN_KV
    nh = HEADS_PER_GROUP
    n = nh * tq
    zero_tile = selx_ref.shape[2] - 1

    qt4 = q_ref[0].T.astype(BF16)
    qt = jnp.concatenate([qt4[r * HEAD_DIM:(r + 1) * HEAD_DIM] for r in range(nh)], axis=1)
    pad = jnp.zeros((K_AUG - HEAD_DIM - SEL_ROWS, n), BF16)

    m_ref[...] = jnp.full(m_ref.shape, NEG, F32)
    acc_ref[...] = jnp.zeros(acc_ref.shape, F32)

    row = lax.broadcasted_iota(jnp.int32, (kv, tq), 0)
    lane = lax.broadcasted_iota(jnp.int32, (kv, tq), 1)
    pat_ref[PAT_ZERO] = jnp.zeros((kv, tq), F32)
    pat_ref[PAT_CAUSAL] = jnp.where(row <= lane, 0.0, NEG)
    pat_ref[PAT_FAR] = jnp.where(lane < row, 0.0, NEG)
    pat_ref[PAT_NONE] = jnp.full((kv, tq), NEG, F32)

    def params(t):
        w = t - qi
        is_slc = t <= qi
        kind = jnp.where(is_slc, 0, 1)
        back = jnp.clip(w - 1, 0, 2)
        have = (qi - back >= 0) & (w <= 3)
        j = jnp.where(is_slc, t, jnp.maximum(qi - back, 0))
        sel = jnp.where(is_slc, t, zero_tile)
        wpat = jnp.where(back == 0, PAT_CAUSAL, jnp.where(back == 1, PAT_ZERO, PAT_FAR))
        pat = jnp.where(is_slc, jnp.where(t == qi, PAT_CAUSAL, PAT_ZERO),
                        jnp.where(have, wpat, PAT_NONE))
        return kind, j, sel, pat

    def lanes4(x):
        return jnp.concatenate([x] * nh, axis=1)

    def produce(s_ref, slot, t):
        kind, j, sel, pat = params(t)
        k_t = kk_ref[0, 0, kind, pl.ds(pl.multiple_of(j * kv, kv), kv), :]
        rhs = jnp.concatenate([qt, lanes4(selx_ref[0, 0, sel]), pad], axis=0)
        s = _dot(k_t, rhs) + lanes4(pat_ref[pat])
        s_ref[...] = s
        smax_ref[slot] = jnp.max(s, axis=0, keepdims=True)

    def consume(s_ref, slot, t):
        kind, j, _, _ = params(t)
        m_old = m_ref[kind]
        m_new = jnp.maximum(m_old, smax_ref[slot])
        alpha = jnp.exp2(m_old - m_new)
        p = jnp.exp2(s_ref[...] - m_new)
        v_t = vv_ref[0, 0, kind, pl.ds(pl.multiple_of(j * kv, kv), kv), :][:, :V_AUG]
        pv = lax.dot_general(v_t, p.astype(BF16), (((0,), (0,)), ((), ())),
                             preferred_element_type=F32)
        acc_ref[kind] = alpha * acc_ref[kind] + pv
        m_ref[kind] = m_new

    pairs = (qi + 5) // 2
    produce(s_a, 0, 0)

    def body(u, carry):
        produce(s_b, 1, 2 * u + 1)
        consume(s_a, 0, 2 * u)
        produce(s_a, 0, 2 * u + 2)
        consume(s_b, 1, 2 * u + 1)
        return carry

    lax.fori_loop(0, pairs - 1, body, 0)
    last = 2 * pairs - 2
    produce(s_b, 1, last + 1)
    consume(s_a, 0, last)
    consume(s_b, 1, last + 1)

    gt = _sigmoid(g_ref[0]).T
    o_slc = acc_ref[0, :HEAD_DIM] * (1.0 / acc_ref[0, HEAD_DIM:HEAD_DIM + 1])
    o_win = acc_ref[1, :HEAD_DIM] * (1.0 / acc_ref[1, HEAD_DIM:HEAD_DIM + 1])
    outs = []
    for r in range(nh):
        cols = slice(r * tq, (r + 1) * tq)
        o_cmp = oc_ref[0, r * HEAD_DIM:(r + 1) * HEAD_DIM, :]
        outs.append(gt[3 * r:3 * r + 1] * o_cmp + gt[3 * r + 1:3 * r + 2] * o_slc[:, cols]
                    + gt[3 * r + 2:3 * r + 3] * o_win[:, cols])
    o_ref[0] = jnp.concatenate(outs, axis=0).T.astype(o_ref.dtype)


def _attention(q, kk, vv, selx, gates, oc):
    b, s, _ = q.shape
    g = N_GROUPS
    gw = HEADS_PER_GROUP * HEAD_DIM
    tq = ATTN_Q
    n = HEADS_PER_GROUP * tq
    n_sel = selx.shape[2]
    return pl.pallas_call(
        _attn_kernel,
        grid=(b, g, s // tq),
        in_specs=[
            pl.BlockSpec((1, tq, gw), lambda i, j, k: (i, k, j)),
            pl.BlockSpec((1, 1, 2, s, K_AUG), lambda i, j, k: (i, j, 0, 0, 0)),
            pl.BlockSpec((1, 1, 2, s, K_AUG), lambda i, j, k: (i, j, 0, 0, 0)),
            pl.BlockSpec((1, 1, n_sel, SEL_ROWS, tq), lambda i, j, k: (i, j, 0, 0, k)),
            pl.BlockSpec((1, tq, GATE_LANES), lambda i, j, k: (i, k, j)),
            pl.BlockSpec((1, gw, tq), lambda i, j, k: (i, j, k)),
        ],
        out_specs=pl.BlockSpec((1, tq, gw), lambda i, j, k: (i, k, j)),
        out_shape=jax.ShapeDtypeStruct((b, s, D_ATTN), BF16),
        scratch_shapes=[
            pltpu.VMEM((2, 1, n), F32),
            pltpu.VMEM((2, V_AUG, n), F32),
            pltpu.VMEM((ATTN_KV, n), F32),
            pltpu.VMEM((ATTN_KV, n), F32),
            pltpu.VMEM((2, 1, n), F32),
            pltpu.VMEM((4, ATTN_KV, tq), F32),
        ],
        compiler_params=_params("parallel", "parallel", "arbitrary"),
        name="slc_win_attn",
    )(q, kk, vv, selx, gates, oc)


def _rnn_kernel(xr_ref, gr_ref, cw_ref, cb_ref, wa_ref, wi_ref, ba_ref, bi_ref, lam_ref, o_ref,
                xprev, hcarry):
    si = pl.program_id(1)
    ts = xr_ref.shape[1]
    d = xr_ref.shape[2]

    @pl.when(si == 0)
    def _():
        xprev[...] = jnp.zeros(xprev.shape, F32)
        hcarry[...] = jnp.zeros(hcarry.shape, F32)

    ng = ts // SUBLANES
    x3 = xr_ref[0].reshape(ng, SUBLANES, d)
    sub = lax.broadcasted_iota(jnp.int32, (ng, SUBLANES, d), 1)
    xc = cb_ref[...] + x3 * cw_ref[CONV_WIDTH - 1:CONV_WIDTH, :]
    for back in range(1, CONV_WIDTH):
        rolled = pltpu.roll(x3, back, 1)
        prev = jnp.concatenate([pltpu.roll(xprev[...], back, 0)[None], rolled[:-1]], axis=0)
        tap = CONV_WIDTH - 1 - back
        xc = xc + jnp.where(sub >= back, rolled, prev) * cw_ref[tap:tap + 1, :]
    xprev[...] = x3[ng - 1]
    xc = xc.reshape(ts, d)

    xcb = xc.astype(BF16)
    n_pack = d // LRU_PACK
    ra = jnp.concatenate(
        [_dot(xcb[:, c * LRU_PACK:(c + 1) * LRU_PACK], wa_ref[c]) for c in range(n_pack)], axis=1)
    ri = jnp.concatenate(
        [_dot(xcb[:, c * LRU_PACK:(c + 1) * LRU_PACK], wi_ref[c]) for c in range(n_pack)], axis=1)
    r = _sigmoid(ra + ba_ref[...])
    ig = _sigmoid(ri + bi_ref[...])
    log_a = -LRU_C * r * jax.nn.softplus(-lam_ref[...])
    a = jnp.exp(log_a)
    one_m_a2 = -jnp.tanh(log_a) * (a * a + 1.0)
    root = jnp.where(one_m_a2 > 0.0, one_m_a2 * lax.rsqrt(one_m_a2), 0.0)
    bt = root * (ig * xc)

    a3 = a.reshape(ng, SUBLANES, d)
    b3 = bt.reshape(ng, SUBLANES, d)
    step = 1
    while step < SUBLANES:
        keep = sub >= step
        a_prev = jnp.where(keep, pltpu.roll(a3, step, 1), 1.0)
        b_prev = jnp.where(keep, pltpu.roll(b3, step, 1), 0.0)
        b3 = b3 + a3 * b_prev
        a3 = a3 * a_prev
        step *= 2
    h = hcarry[0:1, :]
    hs = []
    for k in range(ng):
        hk = b3[k] + a3[k] * h
        hs.append(hk)
        h = hk[SUBLANES - 1:SUBLANES]
    hcarry[0:1, :] = h
    hfull = jnp.concatenate(hs, axis=0)
    o_ref[0] = (hfull * jax.nn.gelu(gr_ref[0])).astype(o_ref.dtype)


def _rnn(xr, gr, cw, cb, wa, wi, ba, bi, lam):
    b, s, d = xr.shape
    ts = RNN_ROWS
    row = lambda a: _resident(a.shape)
    return pl.pallas_call(
        _rnn_kernel,
        grid=(b, s // ts),
        in_specs=[
            pl.BlockSpec((1, ts, d), lambda i, j: (i, j, 0)),
            pl.BlockSpec((1, ts, d), lambda i, j: (i, j, 0)),
            row(cw), row(cb), row(wa), row(wi), row(ba), row(bi), row(lam),
        ],
        out_specs=pl.BlockSpec((1, ts, d), lambda i, j: (i, j, 0)),
        out_shape=jax.ShapeDtypeStruct((b, s, d), BF16),
        scratch_shapes=[pltpu.VMEM((SUBLANES, d), F32), pltpu.VMEM((SUBLANES, d), F32)],
        compiler_params=_params("parallel", "arbitrary"),
        name="rg_lru",
    )(xr, gr, cw, cb, wa, wi, ba, bi, lam)


def _merge_mlp_kernel(x_ref, attn_ref, rnn_ref, ga_ref, gb_ref, wua_ref, wur_ref, wo_ref,
                      n2_ref, w1_ref, w2_ref, fn_ref, o_ref, *, final):
    up_a = _dot(attn_ref[...], wua_ref[...])
    up_r = _dot(rnn_ref[...], wur_ref[...])
    merged = _sigmoid(ga_ref[...]) * up_a + _sigmoid(gb_ref[...]) * up_r
    x = x_ref[...] + _dot(merged.astype(BF16), wo_ref[...])
    var = jnp.mean(x * x, axis=-1, keepdims=True)
    hn = (x * lax.rsqrt(var + EPS) * n2_ref[...]).astype(BF16)
    d_ff = w1_ref.shape[1]
    acc = jnp.zeros_like(x)
    for c in range(d_ff // MLP_FF_CHUNK):
        lo, hi = c * MLP_FF_CHUNK, (c + 1) * MLP_FF_CHUNK
        h1 = jnp.maximum(_dot(hn, w1_ref[:, lo:hi]), 0.0)
        acc = acc + _dot((h1 * h1).astype(BF16), w2_ref[lo:hi, :])
    x = x + acc
    if final:
        var = jnp.mean(x * x, axis=-1, keepdims=True)
        x = x * lax.rsqrt(var + EPS) * fn_ref[...]
    o_ref[...] = x


def _merge_mlp(x2, attn, rnn, ga, gb, wua, wur, wo, n2, w1, w2, fn, final):
    t, d = x2.shape
    tm = MLP_ROWS
    rows = lambda w: pl.BlockSpec((tm, w), lambda i: (i, 0))
    return pl.pallas_call(
        functools.partial(_merge_mlp_kernel, final=final),
        grid=(t // tm,),
        in_specs=[
            rows(d), rows(attn.shape[1]), rows(rnn.shape[1]), rows(d), rows(d),
            _resident(wua.shape), _resident(wur.shape), _resident(wo.shape),
            _resident(n2.shape), _resident(w1.shape), _resident(w2.shape), _resident(fn.shape),
        ],
        out_specs=rows(d),
        out_shape=jax.ShapeDtypeStruct((t, d), F32),
        compiler_params=_params("parallel"),
        name="merge_mlp",
    )(x2, attn, rnn, ga, gb, wua, wur, wo, n2, w1, w2, fn)


def _block_diag_pack(w, pack):
    nblk, k, _ = w.shape
    per = pack // k
    w4 = w.reshape(nblk // per, per, k, k)
    eye = jnp.eye(per, dtype=w.dtype)
    out = jnp.einsum("cpij,pq->cpiqj", w4, eye)
    return out.reshape(nblk // per, pack, pack)


def _in_proj_weights(w_in):
    d = w_in.shape[0]
    sizes = (D_ATTN, D_KV, D_KV, D_KV, D_KV, D_KV, D_KV, N_HEADS * 3, d, d, d, d)
    offs = np.concatenate([[0], np.cumsum(sizes)])
    col = lambda i: w_in[:, offs[i]:offs[i + 1]]
    q, k_c, v_c, k_s, v_s, k_w, v_w, g_nsa, xr, gr, g_a, g_b = [col(i) for i in range(12)]
    per_group = HEADS_PER_GROUP * 3
    gates = jnp.concatenate(
        [jnp.pad(g_nsa[:, g * per_group:(g + 1) * per_group], ((0, 0), (0, GATE_LANES - per_group)))
         for g in range(N_GROUPS)], axis=1)
    precise = [q * (ATTN_SCALE * LOG2_E), gates, k_c]
    fast = [v_c, k_s, v_s, k_w, v_w, xr, gr, g_a, g_b]
    dtypes = [F32, F32, F32, F32, BF16, BF16, BF16, BF16, F32, F32, F32, F32]
    wcat = jnp.concatenate(precise + fast, axis=1)
    n_precise = sum(p.shape[1] for p in precise)
    wh = wcat.astype(BF16)
    wl = (wcat[:, :n_precise] - wh[:, :n_precise].astype(F32)).astype(BF16)
    layout, start = [], 0
    for p in precise + fast:
        layout.append((start, p.shape[1]))
        start += p.shape[1]
    return wh, wl, layout, dtypes, n_precise


def _aug_keys(k):
    s = k.shape[2]
    blk = (jnp.arange(s) // SLC_BLOCK) % (ATTN_KV // SLC_BLOCK)
    onehot = jax.nn.one_hot(blk, K_AUG - HEAD_DIM, dtype=k.dtype)
    return jnp.concatenate([k, jnp.broadcast_to(onehot, k.shape[:2] + onehot.shape)], axis=-1)


def _aug_values(v):
    ones = jnp.ones(v.shape[:-1] + (1,), v.dtype)
    zeros = jnp.zeros(v.shape[:-1] + (K_AUG - HEAD_DIM - 1,), v.dtype)
    return jnp.concatenate([v, ones, zeros], axis=-1)


def _by_group(a, b, s):
    return a.reshape(b, s, N_GROUPS, HEAD_DIM).transpose(0, 2, 1, 3)


def kernel(x, norm1_w, w_in, cmp_pos_k, cmp_pos_v, cmp_k_w1, cmp_k_w2, cmp_v_w1, cmp_v_w2, conv_w, conv_b, lru_w_a, lru_b_a, lru_w_i, lru_b_i, lru_lambda, w_up_attn, w_up_rnn, w_out, norm2_w, mlp_w1, mlp_w2, final_norm_w):
    b, s, d = x.shape
    depth = w_in.shape[0]
    t = b * s
    ns = s // CMP_STRIDE
    nb = s // SLC_BLOCK
    assert s % ATTN_Q == 0 and s % RNN_ROWS == 0 and t % PROJ_ROWS == 0 and t % MLP_ROWS == 0
    assert ATTN_Q == ATTN_KV and WINDOW == 2 * ATTN_KV and nb >= N_SELECT
    mt = jnp.asarray(_importance_matrix(ns, nb), BF16)
    x2 = x.reshape(t, d)
    fn = final_norm_w.reshape(1, d)
    for l in range(depth):
        wh, wl, layout, dtypes, n_precise = _in_proj_weights(w_in[l])
        (q, gates, k_c, v_c, k_s, v_s, k_w, v_w, xr, gr, g_a, g_b) = _norm_proj(
            x2, norm1_w[l].reshape(1, d), wh, wl, layout, dtypes, n_precise)

        xkv = jnp.stack([_by_group(k_c, b, s), _by_group(v_c, b, s)])
        xkv = xkv.reshape(2, b, N_GROUPS, ns, CMP_STRIDE * HEAD_DIM)
        pos = jnp.stack([cmp_pos_k[l], cmp_pos_v[l]]).reshape(2, 1, CMP_BLOCK * HEAD_DIM)
        pos = jnp.broadcast_to(pos, (2, SUBLANES, CMP_BLOCK * HEAD_DIM))
        w1 = jnp.stack([cmp_k_w1[l], cmp_v_w1[l]])
        w2 = jnp.stack([cmp_k_w2[l], cmp_v_w2[l]])
        w1h, w1l = _split(w1)
        w2h, w2l = _split(w2)
        kvc, kvct = _compress(xkv, pos, w1h, w1l, w2h, w2l)

        q3 = q.reshape(b, s, D_ATTN)
        oc, selx = _select(q3, kvc, kvct, mt)
        kk = jnp.stack([_aug_keys(_by_group(k_s, b, s)), _aug_keys(_by_group(k_w, b, s))], axis=2)
        vv = jnp.stack([_aug_values(_by_group(v_s, b, s)), _aug_values(_by_group(v_w, b, s))], axis=2)
        attn = _attention(q3, kk, vv, selx, gates.reshape(b, s, N_GROUPS * GATE_LANES), oc)

        rnn = _rnn(xr.reshape(b, s, d), gr.reshape(b, s, d),
                   jnp.pad(conv_w[l], ((0, SUBLANES - CONV_WIDTH), (0, 0))), conv_b[l].reshape(1, d),
                   _block_diag_pack(lru_w_a[l], LRU_PACK).astype(BF16),
                   _block_diag_pack(lru_w_i[l], LRU_PACK).astype(BF16),
                   lru_b_a[l].reshape(1, d), lru_b_i[l].reshape(1, d), lru_lambda[l].reshape(1, d))

        x2 = _merge_mlp(x2, attn.reshape(t, D_ATTN), rnn.reshape(t, d), g_a, g_b,
                        w_up_attn[l].astype(BF16), w_up_rnn[l].astype(BF16), w_out[l].astype(BF16),
                        norm2_w[l].reshape(1, d), mlp_w1[l].astype(BF16), mlp_w2[l].astype(BF16),
                        fn, final=(l == depth - 1))
    return x2.reshape(b, s, d)
```

```python
import functools

import numpy as np
import jax
import jax.numpy as jnp
from jax import lax
from jax.experimental import pallas as pl
from jax.experimental.pallas import tpu as pltpu

F32 = jnp.float32
BF16 = jnp.bfloat16

N_HEADS = 8
HEAD_DIM = 64
N_GROUPS = 2
HEADS_PER_GROUP = N_HEADS // N_GROUPS
D_ATTN = N_HEADS * HEAD_DIM
D_KV = N_GROUPS * HEAD_DIM
CMP_BLOCK = 32
CMP_STRIDE = 16
CMP_HIDDEN = 256
SLC_BLOCK = 64
N_SELECT = 16
N_LOCAL = 2
WINDOW = 512
ATTN_SCALE = HEAD_DIM ** -0.5
NEG = -1e30
FORCE_SCORE = 1e4
N_RNN_BLOCKS = 16
CONV_WIDTH = 4
LRU_C = 8.0
EPS = 1e-6
LOG2_E = 1.4426950408889634

VMEM_LIMIT_BYTES = 56 * 1024 * 1024
PROJ_ROWS = 256
MLP_ROWS = 256
MLP_FF_CHUNK = 1024
ATTN_Q = 256
ATTN_KV = 256
RNN_ROWS = 256
LRU_PACK = 256
SUBLANES = 8
GATE_LANES = 128
PAT_ZERO, PAT_CAUSAL, PAT_FAR, PAT_NONE = 0, 1, 2, 3
SEL_ROWS = 16
K_AUG = 4 * HEAD_DIM
V_AUG = HEAD_DIM + 16


def _dot(a, b):
    return jnp.dot(a, b, preferred_element_type=F32)


def _sigmoid(x):
    return 0.5 * jnp.tanh(0.5 * x) + 0.5


def _split(a):
    hi = a.astype(BF16)
    lo = (a - hi.astype(F32)).astype(BF16)
    return hi, lo


def _dot3(ah, al, bh, bl):
    return _dot(ah, bh) + _dot(ah, bl) + _dot(al, bh)


def _params(*sem):
    return pltpu.CompilerParams(dimension_semantics=sem, vmem_limit_bytes=VMEM_LIMIT_BYTES)


def _resident(shape):
    nd = len(shape)
    return pl.BlockSpec(shape, lambda *_: (0,) * nd, pipeline_mode=pl.Buffered(1))


def _norm_proj_kernel(x_ref, nw_ref, wh_ref, wl_ref, *out_refs, chunks, n_precise):
    x = x_ref[...]
    tm = x.shape[0]
    var = jnp.mean(x * x, axis=-1, keepdims=True)
    xn = x * lax.rsqrt(var + EPS) * nw_ref[...]
    xh, xl = _split(xn)
    lane = lax.broadcasted_iota(jnp.int32, (tm, D_KV), 1)
    out = iter(out_refs)
    for c_start, c_width, pieces in chunks:
        wh = wh_ref[:, c_start:c_start + c_width]
        prod = _dot(xh, wh)
        if c_start < n_precise:
            prod = prod + _dot(xl, wh) + _dot(xh, wl_ref[:, c_start:c_start + c_width])
        for kind, off, width in pieces:
            o_ref = next(out)
            acc = prod[:, off:off + width]
            if kind == "plain":
                o_ref[...] = acc.astype(o_ref.dtype)
            elif kind == "keys":
                row = lax.broadcasted_iota(jnp.int32, (tm, D_KV), 0)
                onehot = jnp.where(lane == (row // SLC_BLOCK) % (ATTN_KV // SLC_BLOCK), 1.0, 0.0)
                o_ref[:, :width] = acc.astype(o_ref.dtype)
                o_ref[:, width:] = onehot.astype(o_ref.dtype)
            else:
                ones_col = jnp.where(lane == HEAD_DIM, 1.0, 0.0)
                for i in range(width // D_KV):
                    both = acc[:, i * D_KV:(i + 1) * D_KV]
                    swapped = pltpu.roll(both, HEAD_DIM, 1)
                    lo = 2 * i * D_KV
                    o_ref[:, lo:lo + D_KV] = jnp.where(lane < HEAD_DIM, both, ones_col).astype(o_ref.dtype)
                    o_ref[:, lo + D_KV:lo + 2 * D_KV] = jnp.where(
                        lane < HEAD_DIM, swapped, ones_col).astype(o_ref.dtype)


def _norm_proj(x2, norm_w, wh, wl, chunks, dtypes, n_precise):
    t, d = x2.shape
    assert PROJ_ROWS % ATTN_KV == 0
    grid = (t // PROJ_ROWS,)
    widths = [w if kind == "plain" else 2 * w for (_, _, pieces) in chunks for (kind, _, w) in pieces]
    out_shape = [jax.ShapeDtypeStruct((t, w), dt) for w, dt in zip(widths, dtypes)]
    out_specs = [pl.BlockSpec((PROJ_ROWS, w), lambda i: (i, 0)) for w in widths]
    return pl.pallas_call(
        functools.partial(_norm_proj_kernel, chunks=chunks, n_precise=n_precise),
        grid=grid,
        in_specs=[
            pl.BlockSpec((PROJ_ROWS, d), lambda i: (i, 0)),
            _resident(norm_w.shape),
            _resident(wh.shape),
            _resident(wl.shape),
        ],
        out_specs=out_specs,
        out_shape=out_shape,
        compiler_params=_params("parallel"),
        name="norm_proj",
    )(x2, norm_w, wh, wl)


def _compress_kernel(x_ref, pos_ref, w1h_ref, w1l_ref, w2h_ref, w2l_ref, c3_ref, ct_ref):
    ns = x_ref.shape[1] // CMP_STRIDE
    width = N_GROUPS * CMP_HIDDEN

    def times_w1(rows, q):
        rh, rl = _split(rows)
        wh = w1h_ref[0, q]
        return (_dot(jnp.concatenate([rh, rl], axis=1), jnp.concatenate([wh, wh], axis=0))
                + _dot(rh, w1l_ref[0, q]))

    first = jnp.zeros((ns, width), F32)
    second = jnp.zeros((ns, width), F32)
    posb = jnp.zeros((SUBLANES, width), F32)
    for p in range(CMP_STRIDE):
        xp = x_ref[0, pl.ds(p, ns, stride=CMP_STRIDE), :]
        first = first + times_w1(xp, p)
        second = second + times_w1(xp, p + CMP_STRIDE)
        posb = posb + times_w1(pos_ref[0, p], p) + times_w1(pos_ref[0, p + CMP_STRIDE], p + CMP_STRIDE)
    pre = first + pltpu.roll(second, ns - 1, 0) + posb[0:1]
    hid = jax.nn.gelu(pre)
    row = lax.broadcasted_iota(jnp.int32, (ns, HEAD_DIM), 0)
    for g in range(N_GROUPS):
        hh, hl = _split(hid[:, g * CMP_HIDDEN:(g + 1) * CMP_HIDDEN])
        c = _dot3(hh, hl, w2h_ref[0], w2l_ref[0])
        c = jnp.where(row < ns - 1, c, 0.0)
        ch, cl = _split(c)
        c3_ref[0, 0, g] = jnp.concatenate([ch, ch, cl, jnp.zeros_like(ch)], axis=1)
        wide = jnp.concatenate([c, jnp.zeros_like(c)], axis=1)
        ct_ref[0, 0, g] = wide.T[:HEAD_DIM].astype(ct_ref.dtype)


def _compress(kv_c, pos, w1h, w1l, w2h, w2l):
    b, s, _ = kv_c.shape
    ns = s // CMP_STRIDE
    g = N_GROUPS
    return pl.pallas_call(
        _compress_kernel,
        grid=(2, b),
        in_specs=[
            pl.BlockSpec((1, s, D_KV), lambda k, i: (i, 0, k)),
            pl.BlockSpec((1, CMP_BLOCK, SUBLANES, D_KV), lambda k, i: (k, 0, 0, 0)),
            pl.BlockSpec((1, CMP_BLOCK, D_KV, g * CMP_HIDDEN), lambda k, i: (k, 0, 0, 0)),
            pl.BlockSpec((1, CMP_BLOCK, D_KV, g * CMP_HIDDEN), lambda k, i: (k, 0, 0, 0)),
            pl.BlockSpec((1, CMP_HIDDEN, HEAD_DIM), lambda k, i: (k, 0, 0)),
            pl.BlockSpec((1, CMP_HIDDEN, HEAD_DIM), lambda k, i: (k, 0, 0)),
        ],
        out_specs=[
            pl.BlockSpec((1, 1, g, ns, 4 * HEAD_DIM), lambda k, i: (k, i, 0, 0, 0)),
            pl.BlockSpec((1, 1, g, HEAD_DIM, ns), lambda k, i: (k, i, 0, 0, 0)),
        ],
        out_shape=[
            jax.ShapeDtypeStruct((2, b, g, ns, 4 * HEAD_DIM), BF16),
            jax.ShapeDtypeStruct((2, b, g, HEAD_DIM, ns), BF16),
        ],
        compiler_params=_params("parallel", "parallel"),
        name="compress",
    )(kv_c, pos, w1h, w1l, w2h, w2l)


def _importance_matrix(nc_pad, nb):
    r_s = SLC_BLOCK // CMP_STRIDE
    r_c = CMP_BLOCK // CMP_STRIDE
    mat = np.zeros((nb, nc_pad), np.float32)
    for j in range(nb):
        for m in range(r_s):
            for n in range(r_c):
                i = r_s * j - m - n
                if i >= 0:
                    mat[j, i] += 1.0
    return mat


def _select_kernel(q_ref, kc3_ref, vct_ref, mt_ref, oc_ref, selx_ref):
    qi = pl.program_id(2)
    tq = q_ref.shape[1]
    ncp = kc3_ref.shape[3]
    nb = mt_ref.shape[0]
    nh = HEADS_PER_GROUP
    bpt = ATTN_KV // SLC_BLOCK

    def lanes4(x):
        return jnp.concatenate([x] * nh, axis=1)

    qt4 = q_ref[0].T
    qt = jnp.concatenate([qt4[r * HEAD_DIM:(r + 1) * HEAD_DIM] for r in range(nh)], axis=1)
    qh, ql = _split(qt)
    rhs = jnp.concatenate([qh, ql, qh, jnp.zeros_like(qh)], axis=0)
    s = _dot(kc3_ref[0, 0, 0], rhs)
    t1 = qi * tq + lax.broadcasted_iota(jnp.int32, (1, tq), 1)
    ci = lax.broadcasted_iota(jnp.int32, (ncp, 1), 0)
    mask = lanes4((ci * CMP_STRIDE + (CMP_BLOCK - 1)) <= t1)
    s = jnp.where(mask, s, NEG)
    m = jnp.max(s, axis=0, keepdims=True)
    e = jnp.exp2(s - m)
    den = jnp.sum(e, axis=0, keepdims=True)
    inv = jnp.where(m > 0.5 * NEG, 1.0 / den, 0.0)
    p = e * inv
    oc = _dot(vct_ref[0, 0, 0], p.astype(BF16))
    ps = p[:, 0:tq]
    for r in range(1, nh):
        ps = ps + p[:, r * tq:(r + 1) * tq]
        oc_ref[0, r * HEAD_DIM:(r + 1) * HEAD_DIM, :] = oc[:, r * tq:(r + 1) * tq]
    oc_ref[0, 0:HEAD_DIM, :] = oc[:, 0:tq]

    blk = lax.broadcasted_iota(jnp.int32, (nb, 1), 0)
    cur = t1 // SLC_BLOCK
    valid = blk <= cur

    def store_rows(selb):
        for jt in range(nb // bpt):
            rows = jnp.concatenate([selb[jt * bpt:(jt + 1) * bpt], jnp.zeros((SEL_ROWS - bpt, tq), F32)], axis=0)
            selx_ref[0, 0, jt] = rows.astype(selx_ref.dtype)

    need_rank = (qi + 1) * tq > N_SELECT * SLC_BLOCK

    @pl.when(jnp.logical_not(need_rank))
    def _():
        store_rows(jnp.where(valid, 0.0, NEG))

    @pl.when(need_rank)
    def _():
        p1 = ps.astype(BF16)
        r1 = ps - p1.astype(F32)
        p2 = r1.astype(BF16)
        p3 = (r1 - p2.astype(F32)).astype(BF16)
        mt = mt_ref[...]
        imp = _dot(mt, p1) + _dot(mt, p2) + _dot(mt, p3)
        forced = (blk == 0) | (blk > cur - N_LOCAL)
        score = jnp.where(valid, jnp.where(forced, FORCE_SCORE, imp), NEG)
        nchunk = nb // SUBLANES
        chunks = [score[SUBLANES * v:SUBLANES * (v + 1)] for v in range(nchunk)]
        ranks = [jnp.zeros((SUBLANES, tq), jnp.int32) for _ in range(nchunk)]
        sub = lax.broadcasted_iota(jnp.int32, (SUBLANES, 1), 0)
        for i in range(nb):
            si = score[i:i + 1, :]
            vi = i // SUBLANES
            for v in range(nchunk):
                if v > vi:
                    beats = jnp.where(si >= chunks[v], 1, 0)
                elif v < vi:
                    beats = jnp.where(si > chunks[v], 1, 0)
                else:
                    beats = jnp.where(sub > i % SUBLANES, jnp.where(si >= chunks[v], 1, 0),
                                      jnp.where(si > chunks[v], 1, 0))
                ranks[v] = ranks[v] + beats
        rank = jnp.concatenate(ranks, axis=0)
        sel = valid & (rank < N_SELECT)
        store_rows(jnp.where(sel, 0.0, NEG))


def _select(q, kc3, vct, mt):
    b, s, _ = q.shape
    g = N_GROUPS
    ncp = kc3.shape[3]
    nb = mt.shape[0]
    gw = HEADS_PER_GROUP * HEAD_DIM
    n_sel = nb // (ATTN_KV // SLC_BLOCK)
    grid = (b, g, s // ATTN_Q)
    return pl.pallas_call(
        _select_kernel,
        grid=grid,
        in_specs=[
            pl.BlockSpec((1, ATTN_Q, gw), lambda i, j, k: (i, k, j)),
            pl.BlockSpec((1, 1, 1, ncp, 4 * HEAD_DIM), lambda i, j, k: (0, i, j, 0, 0)),
            pl.BlockSpec((1, 1, 1, HEAD_DIM, ncp), lambda i, j, k: (1, i, j, 0, 0)),
            pl.BlockSpec((nb, ncp), lambda i, j, k: (0, 0)),
        ],
        out_specs=[
            pl.BlockSpec((1, gw, ATTN_Q), lambda i, j, k: (i, j, k)),
            pl.BlockSpec((1, 1, n_sel, SEL_ROWS, ATTN_Q), lambda i, j, k: (i, j, 0, 0, k)),
        ],
        out_shape=[
            jax.ShapeDtypeStruct((b, D_ATTN, s), F32),
            jax.ShapeDtypeStruct((b, g, n_sel, SEL_ROWS, s), BF16),
        ],
        compiler_params=_params("parallel", "parallel", "parallel"),
        name="cmp_select",
    )(q, kc3, vct, mt)


def _attn_kernel(q_ref, ks_ref, kw_ref, vs_ref, vw_ref, selx_ref, g_ref, oc_ref, o_ref,
                 m_ref, acc_ref, s_a, s_b, smax_ref, pat_ref):
    grp = pl.program_id(1)
    qi = pl.program_id(2)
    tq = q_ref.shape[1]
    kv = ATTN_KV
    nh = HEADS_PER_GROUP
    n = nh * tq
    slc, win = 0, 1

    qt4 = q_ref[0].T.astype(BF16)
    qt = jnp.concatenate([qt4[r * HEAD_DIM:(r + 1) * HEAD_DIM] for r in range(nh)], axis=1)
    zq = jnp.zeros_like(qt)
    q2 = jnp.concatenate([jnp.where(grp == 0, qt, zq), jnp.where(grp == 1, qt, zq)], axis=0)
    pad = jnp.zeros((K_AUG - D_KV - SEL_ROWS, n), BF16)

    m_ref[...] = jnp.full(m_ref.shape, NEG, F32)
    acc_ref[...] = jnp.zeros(acc_ref.shape, F32)

    row = lax.broadcasted_iota(jnp.int32, (kv, tq), 0)
    lane = lax.broadcasted_iota(jnp.int32, (kv, tq), 1)
    pat_ref[PAT_ZERO] = jnp.zeros((kv, tq), F32)
    pat_ref[PAT_CAUSAL] = jnp.where(row <= lane, 0.0, NEG)
    pat_ref[PAT_FAR] = jnp.where(lane < row, 0.0, NEG)
    pat_ref[PAT_NONE] = jnp.full((kv, tq), NEG, F32)

    def lanes4(x):
        return jnp.concatenate([x] * nh, axis=1)

    def rows(j):
        return pl.ds(pl.multiple_of(j * kv, kv), kv)

    def produce_slc(s_ref, slot, j):
        rhs = jnp.concatenate([q2, lanes4(selx_ref[0, 0, j]), pad], axis=0)
        s = _dot(ks_ref[0, rows(j), :], rhs)
        s_ref[...] = s
        smax_ref[slot] = jnp.max(s, axis=0, keepdims=True)

    def produce_win(s_ref, slot, j, pat):
        s = _dot(kw_ref[0, rows(j), :], q2) + lanes4(pat_ref[pat])
        s_ref[...] = s
        smax_ref[slot] = jnp.max(s, axis=0, keepdims=True)

    def softmax_update(s, s_max, v_ref, j, kind):
        m_old = m_ref[kind]
        m_new = jnp.maximum(m_old, s_max)
        alpha = jnp.exp2(m_old - m_new)
        p = jnp.exp2(s - m_new)
        v_t = v_ref[0, rows(j), :][:, :V_AUG]
        pv = lax.dot_general(v_t, p.astype(BF16), (((0,), (0,)), ((), ())),
                             preferred_element_type=F32)
        acc_ref[kind] = alpha * acc_ref[kind] + pv
        m_ref[kind] = m_new

    def consume(s_ref, slot, v_ref, j, kind):
        softmax_update(s_ref[...], smax_ref[slot], v_ref, j, kind)

    odd = qi % 2

    @pl.when(odd == 1)
    def _():
        produce_slc(s_b, 1, 0)
        consume(s_b, 1, vs_ref, 0, slc)

    produce_slc(s_a, 0, odd)

    def body(u, carry):
        j = odd + 2 * u
        produce_slc(s_b, 1, j + 1)
        consume(s_a, 0, vs_ref, j, slc)
        produce_slc(s_a, 0, j + 2)
        consume(s_b, 1, vs_ref, j + 1, slc)
        return carry

    lax.fori_loop(0, (qi - odd) // 2, body, 0)

    j1 = jnp.maximum(qi - 1, 0)
    j2 = jnp.maximum(qi - 2, 0)
    produce_win(s_b, 1, qi, PAT_CAUSAL)
    s_diag = s_a[...] + lanes4(pat_ref[PAT_CAUSAL])
    softmax_update(s_diag, jnp.max(s_diag, axis=0, keepdims=True), vs_ref, qi, slc)
    produce_win(s_a, 0, j1, jnp.where(qi >= 1, PAT_ZERO, PAT_NONE))
    consume(s_b, 1, vw_ref, qi, win)
    produce_win(s_b, 1, j2, jnp.where(qi >= 2, PAT_FAR, PAT_NONE))
    consume(s_a, 0, vw_ref, j1, win)
    consume(s_b, 1, vw_ref, j2, win)

    gt = _sigmoid(g_ref[0]).T
    o_slc = acc_ref[slc, :HEAD_DIM] * (1.0 / acc_ref[slc, HEAD_DIM:HEAD_DIM + 1])
    o_win = acc_ref[win, :HEAD_DIM] * (1.0 / acc_ref[win, HEAD_DIM:HEAD_DIM + 1])
    outs = []
    for r in range(nh):
        cols = slice(r * tq, (r + 1) * tq)
        o_cmp = oc_ref[0, r * HEAD_DIM:(r + 1) * HEAD_DIM, :]
        outs.append(gt[3 * r:3 * r + 1] * o_cmp + gt[3 * r + 1:3 * r + 2] * o_slc[:, cols]
                    + gt[3 * r + 2:3 * r + 3] * o_win[:, cols])
    o_ref[0] = jnp.concatenate(outs, axis=0).T.astype(o_ref.dtype)


def _attention(q, ks, kw, vall, selx, gates, oc):
    b, s, _ = q.shape
    g = N_GROUPS
    gw = HEADS_PER_GROUP * HEAD_DIM
    tq = ATTN_Q
    n = HEADS_PER_GROUP * tq
    n_sel = selx.shape[2]
    return pl.pallas_call(
        _attn_kernel,
        grid=(b, g, s // tq),
        in_specs=[
            pl.BlockSpec((1, tq, gw), lambda i, j, k: (i, k, j)),
            pl.BlockSpec((1, s, K_AUG), lambda i, j, k: (i, 0, 0)),
            pl.BlockSpec((1, s, D_KV), lambda i, j, k: (i, 0, 0)),
            pl.BlockSpec((1, s, D_KV), lambda i, j, k: (i, 0, j)),
            pl.BlockSpec((1, s, D_KV), lambda i, j, k: (i, 0, N_GROUPS + j)),
            pl.BlockSpec((1, 1, n_sel, SEL_ROWS, tq), lambda i, j, k: (i, j, 0, 0, k)),
            pl.BlockSpec((1, tq, GATE_LANES), lambda i, j, k: (i, k, j)),
            pl.BlockSpec((1, gw, tq), lambda i, j, k: (i, j, k)),
        ],
        out_specs=pl.BlockSpec((1, tq, gw), lambda i, j, k: (i, k, j)),
        out_shape=jax.ShapeDtypeStruct((b, s, D_ATTN), BF16),
        scratch_shapes=[
            pltpu.VMEM((2, 1, n), F32),
            pltpu.VMEM((2, V_AUG, n), F32),
            pltpu.VMEM((ATTN_KV, n), F32),
            pltpu.VMEM((ATTN_KV, n), F32),
            pltpu.VMEM((2, 1, n), F32),
            pltpu.VMEM((4, ATTN_KV, tq), F32),
        ],
        compiler_params=_params("parallel", "parallel", "arbitrary"),
        name="slc_win_attn",
    )(q, ks, kw, vall, vall, selx, gates, oc)


def _rnn_kernel(xr_ref, gr_ref, cw_ref, cb_ref, wa_ref, wi_ref, ba_ref, bi_ref, lam_ref, o_ref,
                xprev, hcarry):
    si = pl.program_id(1)
    ts = xr_ref.shape[1]
    d = xr_ref.shape[2]

    @pl.when(si == 0)
    def _():
        xprev[...] = jnp.zeros(xprev.shape, F32)
        hcarry[...] = jnp.zeros(hcarry.shape, F32)

    ng = ts // SUBLANES
    x3 = xr_ref[0].reshape(ng, SUBLANES, d)
    sub = lax.broadcasted_iota(jnp.int32, (ng, SUBLANES, d), 1)
    xc = cb_ref[...] + x3 * cw_ref[CONV_WIDTH - 1:CONV_WIDTH, :]
    for back in range(1, CONV_WIDTH):
        rolled = pltpu.roll(x3, back, 1)
        prev = jnp.concatenate([pltpu.roll(xprev[...], back, 0)[None], rolled[:-1]], axis=0)
        tap = CONV_WIDTH - 1 - back
        xc = xc + jnp.where(sub >= back, rolled, prev) * cw_ref[tap:tap + 1, :]
    xprev[...] = x3[ng - 1]
    xc = xc.reshape(ts, d)

    xcb = xc.astype(BF16)
    n_pack = d // LRU_PACK
    ra = jnp.concatenate(
        [_dot(xcb[:, c * LRU_PACK:(c + 1) * LRU_PACK], wa_ref[c]) for c in range(n_pack)], axis=1)
    ri = jnp.concatenate(
        [_dot(xcb[:, c * LRU_PACK:(c + 1) * LRU_PACK], wi_ref[c]) for c in range(n_pack)], axis=1)
    r = _sigmoid(ra + ba_ref[...])
    ig = _sigmoid(ri + bi_ref[...])
    log_a = -LRU_C * r * jax.nn.softplus(-lam_ref[...])
    a = jnp.exp(log_a)
    one_m_a2 = -jnp.tanh(log_a) * (a * a + 1.0)
    root = jnp.where(one_m_a2 > 0.0, one_m_a2 * lax.rsqrt(one_m_a2), 0.0)
    bt = root * (ig * xc)

    a3 = a.reshape(ng, SUBLANES, d)
    b3 = bt.reshape(ng, SUBLANES, d)
    step = 1
    while step < SUBLANES:
        keep = sub >= step
        a_prev = jnp.where(keep, pltpu.roll(a3, step, 1), 1.0)
        b_prev = jnp.where(keep, pltpu.roll(b3, step, 1), 0.0)
        b3 = b3 + a3 * b_prev
        a3 = a3 * a_prev
        step *= 2
    h = hcarry[0:1, :]
    hs = []
    for k in range(ng):
        hk = b3[k] + a3[k] * h
        hs.append(hk)
        h = hk[SUBLANES - 1:SUBLANES]
    hcarry[0:1, :] = h
    hfull = jnp.concatenate(hs, axis=0)
    o_ref[0] = (hfull * jax.nn.gelu(gr_ref[0])).astype(o_ref.dtype)


def _rnn(xr, gr, cw, cb, wa, wi, ba, bi, lam):
    b, s, d = xr.shape
    ts = RNN_ROWS
    row = lambda a: _resident(a.shape)
    return pl.pallas_call(
        _rnn_kernel,
        grid=(b, s // ts),
        in_specs=[
            pl.BlockSpec((1, ts, d), lambda i, j: (i, j, 0)),
            pl.BlockSpec((1, ts, d), lambda i, j: (i, j, 0)),
            row(cw), row(cb), row(wa), row(wi), row(ba), row(bi), row(lam),
        ],
        out_specs=pl.BlockSpec((1, ts, d), lambda i, j: (i, j, 0)),
        out_shape=jax.ShapeDtypeStruct((b, s, d), BF16),
        scratch_shapes=[pltpu.VMEM((SUBLANES, d), F32), pltpu.VMEM((SUBLANES, d), F32)],
        compiler_params=_params("parallel", "arbitrary"),
        name="rg_lru",
    )(xr, gr, cw, cb, wa, wi, ba, bi, lam)


def _merge_mlp_kernel(x_ref, attn_ref, rnn_ref, ga_ref, gb_ref, wua_ref, wur_ref, wo_ref,
                      n2_ref, w1_ref, w2_ref, fn_ref, o_ref, *, final):
    up_a = _dot(attn_ref[...], wua_ref[...])
    up_r = _dot(rnn_ref[...], wur_ref[...])
    merged = _sigmoid(ga_ref[...]) * up_a + _sigmoid(gb_ref[...]) * up_r
    x = x_ref[...] + _dot(merged.astype(BF16), wo_ref[...])
    var = jnp.mean(x * x, axis=-1, keepdims=True)
    hn = (x * lax.rsqrt(var + EPS) * n2_ref[...]).astype(BF16)
    d_ff = w1_ref.shape[1]
    acc = jnp.zeros_like(x)
    for c in range(d_ff // MLP_FF_CHUNK):
        lo, hi = c * MLP_FF_CHUNK, (c + 1) * MLP_FF_CHUNK
        h1 = jnp.maximum(_dot(hn, w1_ref[:, lo:hi]), 0.0)
        acc = acc + _dot((h1 * h1).astype(BF16), w2_ref[lo:hi, :])
    x = x + acc
    if final:
        var = jnp.mean(x * x, axis=-1, keepdims=True)
        x = x * lax.rsqrt(var + EPS) * fn_ref[...]
    o_ref[...] = x


def _merge_mlp(x2, attn, rnn, ga, gb, wua, wur, wo, n2, w1, w2, fn, final):
    t, d = x2.shape
    tm = MLP_ROWS
    rows = lambda w: pl.BlockSpec((tm, w), lambda i: (i, 0))
    return pl.pallas_call(
        functools.partial(_merge_mlp_kernel, final=final),
        grid=(t // tm,),
        in_specs=[
            rows(d), rows(attn.shape[1]), rows(rnn.shape[1]), rows(d), rows(d),
            _resident(wua.shape), _resident(wur.shape), _resident(wo.shape),
            _resident(n2.shape), _resident(w1.shape), _resident(w2.shape), _resident(fn.shape),
        ],
        out_specs=rows(d),
        out_shape=jax.ShapeDtypeStruct((t, d), F32),
        compiler_params=_params("parallel"),
        name="merge_mlp",
    )(x2, attn, rnn, ga, gb, wua, wur, wo, n2, w1, w2, fn)


def _block_diag_pack(w, pack):
    nblk, k, _ = w.shape
    per = pack // k
    w4 = w.reshape(nblk // per, per, k, k)
    eye = jnp.eye(per, dtype=w.dtype)
    out = jnp.einsum("cpij,pq->cpiqj", w4, eye)
    return out.reshape(nblk // per, pack, pack)


def _in_proj_weights(w_in):
    d = w_in.shape[0]
    sizes = (D_ATTN, D_KV, D_KV, D_KV, D_KV, D_KV, D_KV, N_HEADS * 3, d, d, d, d)
    offs = np.concatenate([[0], np.cumsum(sizes)])
    col = lambda i: w_in[:, offs[i]:offs[i + 1]]
    q, k_c, v_c, k_s, v_s, k_w, v_w, g_nsa, xr, gr, g_a, g_b = [col(i) for i in range(12)]
    per_group = HEADS_PER_GROUP * 3
    gates = jnp.concatenate(
        [jnp.pad(g_nsa[:, g * per_group:(g + 1) * per_group], ((0, 0), (0, GATE_LANES - per_group)))
         for g in range(N_GROUPS)], axis=1)
    groups = [[("plain", q * (ATTN_SCALE * LOG2_E), F32)],
              [("plain", jnp.concatenate([k_c, v_c], axis=1), F32)],
              [("plain", gates, F32)],
              [("keys", k_s, BF16), ("plain", k_w, BF16)],
              [("values", jnp.concatenate([v_s, v_w], axis=1), BF16)],
              [("plain", xr, F32)], [("plain", gr, F32)], [("plain", g_a, F32)], [("plain", g_b, F32)]]
    n_precise = sum(p[1].shape[1] for grp in groups[:2] for p in grp)
    wcat = jnp.concatenate([p[1] for grp in groups for p in grp], axis=1)
    wh = wcat.astype(BF16)
    wl = (wcat[:, :n_precise] - wh[:, :n_precise].astype(F32)).astype(BF16)
    chunks, start = [], 0
    for grp in groups:
        pieces, off = [], 0
        for kind, cols, _ in grp:
            pieces.append((kind, off, cols.shape[1]))
            off += cols.shape[1]
        chunks.append((start, off, tuple(pieces)))
        start += off
    return wh, wl, tuple(chunks), [p[2] for grp in groups for p in grp], n_precise


def _compress_weights(w1_k, w1_v, w2_k, w2_v, pos_k, pos_v):
    w1 = jnp.stack([w1_k, w1_v]).reshape(2, CMP_BLOCK, HEAD_DIM, CMP_HIDDEN)
    eye = jnp.eye(N_GROUPS, dtype=w1.dtype)
    w1g = jnp.einsum("kpdh,gf->kpgdfh", w1, eye).reshape(
        2, CMP_BLOCK, N_GROUPS * HEAD_DIM, N_GROUPS * CMP_HIDDEN)
    pos = jnp.stack([pos_k, pos_v])
    pos = jnp.concatenate([pos] * N_GROUPS, axis=-1)[:, :, None, :]
    pos = jnp.broadcast_to(pos, (2, CMP_BLOCK, SUBLANES, N_GROUPS * HEAD_DIM))
    w1h, w1l = _split(w1g)
    w2h, w2l = _split(jnp.stack([w2_k, w2_v]))
    return pos, w1h, w1l, w2h, w2l


def kernel(x, norm1_w, w_in, cmp_pos_k, cmp_pos_v, cmp_k_w1, cmp_k_w2, cmp_v_w1, cmp_v_w2, conv_w, conv_b, lru_w_a, lru_b_a, lru_w_i, lru_b_i, lru_lambda, w_up_attn, w_up_rnn, w_out, norm2_w, mlp_w1, mlp_w2, final_norm_w):
    b, s, d = x.shape
    depth = w_in.shape[0]
    t = b * s
    ns = s // CMP_STRIDE
    nb = s // SLC_BLOCK
    assert s % ATTN_Q == 0 and s % RNN_ROWS == 0 and t % PROJ_ROWS == 0 and t % MLP_ROWS == 0
    assert ATTN_Q == ATTN_KV and WINDOW == 2 * ATTN_KV and nb >= N_SELECT
    mt = jnp.asarray(_importance_matrix(ns, nb), BF16)
    x2 = x.reshape(t, d)
    fn = final_norm_w.reshape(1, d)
    for l in range(depth):
        wh, wl, chunks, dtypes, n_precise = _in_proj_weights(w_in[l])
        (q, kv_c, gates, k_s, k_w, vall, xr, gr, g_a, g_b) = _norm_proj(
            x2, norm1_w[l].reshape(1, d), wh, wl, chunks, dtypes, n_precise)

        kvc, kvct = _compress(
            kv_c.reshape(b, s, 2 * D_KV),
            *_compress_weights(cmp_k_w1[l], cmp_v_w1[l], cmp_k_w2[l], cmp_v_w2[l], cmp_pos_k[l], cmp_pos_v[l]))

        q3 = q.reshape(b, s, D_ATTN)
        oc, selx = _select(q3, kvc, kvct, mt)
        attn = _attention(q3, k_s.reshape(b, s, K_AUG), k_w.reshape(b, s, D_KV),
                          vall.reshape(b, s, 4 * D_KV), selx,
                          gates.reshape(b, s, N_GROUPS * GATE_LANES), oc)

        rnn = _rnn(xr.reshape(b, s, d), gr.reshape(b, s, d),
                   jnp.pad(conv_w[l], ((0, SUBLANES - CONV_WIDTH), (0, 0))), conv_b[l].reshape(1, d),
                   _block_diag_pack(lru_w_a[l], LRU_PACK).astype(BF16),
                   _block_diag_pack(lru_w_i[l], LRU_PACK).astype(BF16),
                   lru_b_a[l].reshape(1, d), lru_b_i[l].reshape(1, d), lru_lambda[l].reshape(1, d))

        x2 = _merge_mlp(x2, attn.reshape(t, D_ATTN), rnn.reshape(t, d), g_a, g_b,
                        w_up_attn[l].astype(BF16), w_up_rnn[l].astype(BF16), w_out[l].astype(BF16),
                        norm2_w[l].reshape(1, d), mlp_w1[l].astype(BF16), mlp_w2[l].astype(BF16),
                        fn, final=(l == depth - 1))
    return x2.reshape(b, s, d)
```

```python
import functools

import numpy as np
import jax
import jax.numpy as jnp
from jax import lax
from jax.experimental import pallas as pl
from jax.experimental.pallas import tpu as pltpu

F32 = jnp.float32
BF16 = jnp.bfloat16

N_HEADS = 8
HEAD_DIM = 64
N_GROUPS = 2
HEADS_PER_GROUP = N_HEADS // N_GROUPS
D_ATTN = N_HEADS * HEAD_DIM
D_KV = N_GROUPS * HEAD_DIM
CMP_BLOCK = 32
CMP_STRIDE = 16
CMP_HIDDEN = 256
SLC_BLOCK = 64
N_SELECT = 16
N_LOCAL = 2
WINDOW = 512
ATTN_SCALE = HEAD_DIM ** -0.5
NEG = -1e30
FORCE_SCORE = 1e4
N_RNN_BLOCKS = 16
CONV_WIDTH = 4
LRU_C = 8.0
EPS = 1e-6
LOG2_E = 1.4426950408889634

VMEM_LIMIT_BYTES = 56 * 1024 * 1024
PROJ_ROWS = 512
MLP_ROWS = 512
MLP_FF_CHUNK = 1024
ATTN_Q = 256
ATTN_KV = 256
RNN_ROWS = 256
LRU_PACK = 256
SUBLANES = 8
GATE_LANES = 128
PAT_ZERO, PAT_CAUSAL, PAT_FAR, PAT_NONE = 0, 1, 2, 3
SEL_ROWS = 16
K_AUG = 4 * HEAD_DIM
V_AUG = HEAD_DIM + 16


def _dot(a, b):
    return jnp.dot(a, b, preferred_element_type=F32)


def _sigmoid(x):
    return 0.5 * jnp.tanh(0.5 * x) + 0.5


def _split(a):
    hi = a.astype(BF16)
    lo = (a - hi.astype(F32)).astype(BF16)
    return hi, lo


def _dot3(ah, al, bh, bl):
    return _dot(ah, bh) + _dot(ah, bl) + _dot(al, bh)


def _params(*sem):
    return pltpu.CompilerParams(dimension_semantics=sem, vmem_limit_bytes=VMEM_LIMIT_BYTES)


def _resident(shape):
    nd = len(shape)
    return pl.BlockSpec(shape, lambda *_: (0,) * nd, pipeline_mode=pl.Buffered(1))


def _norm_proj_kernel(x_ref, nw_ref, wh_ref, wl_ref, *out_refs, chunks, n_precise):
    x = x_ref[...]
    tm = x.shape[0]
    var = jnp.mean(x * x, axis=-1, keepdims=True)
    xn = x * lax.rsqrt(var + EPS) * nw_ref[...]
    xh, xl = _split(xn)
    lane = lax.broadcasted_iota(jnp.int32, (tm, D_KV), 1)
    out = iter(out_refs)
    for c_start, c_width, pieces in chunks:
        wh = wh_ref[:, c_start:c_start + c_width]
        prod = _dot(xh, wh)
        if c_start < n_precise:
            prod = prod + _dot(xl, wh) + _dot(xh, wl_ref[:, c_start:c_start + c_width])
        for kind, off, width in pieces:
            o_ref = next(out)
            acc = prod[:, off:off + width]
            if kind == "plain":
                o_ref[...] = acc.astype(o_ref.dtype)
            elif kind == "transposed":
                o_ref[...] = acc.T.astype(o_ref.dtype)
            elif kind == "keys":
                row = lax.broadcasted_iota(jnp.int32, (tm, D_KV), 0)
                onehot = jnp.where(lane == (row // SLC_BLOCK) % (ATTN_KV // SLC_BLOCK), 1.0, 0.0)
                o_ref[:, :width] = acc.astype(o_ref.dtype)
                o_ref[:, width:] = onehot.astype(o_ref.dtype)
            else:
                ones_col = jnp.where(lane == HEAD_DIM, 1.0, 0.0)
                for i in range(width // D_KV):
                    both = acc[:, i * D_KV:(i + 1) * D_KV]
                    swapped = pltpu.roll(both, HEAD_DIM, 1)
                    lo = 2 * i * D_KV
                    o_ref[:, lo:lo + D_KV] = jnp.where(lane < HEAD_DIM, both, ones_col).astype(o_ref.dtype)
                    o_ref[:, lo + D_KV:lo + 2 * D_KV] = jnp.where(
                        lane < HEAD_DIM, swapped, ones_col).astype(o_ref.dtype)


def _norm_proj(x2, norm_w, wh, wl, chunks, dtypes, n_precise):
    t, d = x2.shape
    assert PROJ_ROWS % ATTN_KV == 0
    grid = (t // PROJ_ROWS,)
    out_shape, out_specs = [], []
    for (kind, _, w), dt in zip([pc for (_, _, pieces) in chunks for pc in pieces], dtypes):
        if kind == "transposed":
            out_shape.append(jax.ShapeDtypeStruct((w, t), dt))
            out_specs.append(pl.BlockSpec((w, PROJ_ROWS), lambda i: (0, i)))
        else:
            w = w if kind == "plain" else 2 * w
            out_shape.append(jax.ShapeDtypeStruct((t, w), dt))
            out_specs.append(pl.BlockSpec((PROJ_ROWS, w), lambda i: (i, 0)))
    return pl.pallas_call(
        functools.partial(_norm_proj_kernel, chunks=chunks, n_precise=n_precise),
        grid=grid,
        in_specs=[
            pl.BlockSpec((PROJ_ROWS, d), lambda i: (i, 0)),
            _resident(norm_w.shape),
            _resident(wh.shape),
            _resident(wl.shape),
        ],
        out_specs=out_specs,
        out_shape=out_shape,
        compiler_params=_params("parallel"),
        name="norm_proj",
    )(x2, norm_w, wh, wl)


def _compress_kernel(x_ref, pos_ref, w1h_ref, w1l_ref, w2h_ref, w2l_ref, c3_ref, ct_ref):
    ns = x_ref.shape[1] // CMP_STRIDE
    width = N_GROUPS * CMP_HIDDEN

    def times_w1(rows, q):
        rh, rl = _split(rows)
        wh = w1h_ref[0, q]
        return (_dot(jnp.concatenate([rh, rl], axis=1), jnp.concatenate([wh, wh], axis=0))
                + _dot(rh, w1l_ref[0, q]))

    first = jnp.zeros((ns, width), F32)
    second = jnp.zeros((ns, width), F32)
    posb = jnp.zeros((SUBLANES, width), F32)
    for p in range(CMP_STRIDE):
        xp = x_ref[0, pl.ds(p, ns, stride=CMP_STRIDE), :]
        first = first + times_w1(xp, p)
        second = second + times_w1(xp, p + CMP_STRIDE)
        posb = posb + times_w1(pos_ref[0, p], p) + times_w1(pos_ref[0, p + CMP_STRIDE], p + CMP_STRIDE)
    pre = first + pltpu.roll(second, ns - 1, 0) + posb[0:1]
    hid = jax.nn.gelu(pre)
    row = lax.broadcasted_iota(jnp.int32, (ns, HEAD_DIM), 0)
    for g in range(N_GROUPS):
        hh, hl = _split(hid[:, g * CMP_HIDDEN:(g + 1) * CMP_HIDDEN])
        c = _dot3(hh, hl, w2h_ref[0], w2l_ref[0])
        c = jnp.where(row < ns - 1, c, 0.0)
        ch, cl = _split(c)
        c3_ref[0, 0, g] = jnp.concatenate([ch, ch, cl, jnp.zeros_like(ch)], axis=1)
        wide = jnp.concatenate([c, jnp.zeros_like(c)], axis=1)
        ct_ref[0, 0, g] = wide.T[:HEAD_DIM].astype(ct_ref.dtype)


def _compress(kv_c, pos, w1h, w1l, w2h, w2l):
    b, s, _ = kv_c.shape
    ns = s // CMP_STRIDE
    g = N_GROUPS
    return pl.pallas_call(
        _compress_kernel,
        grid=(2, b),
        in_specs=[
            pl.BlockSpec((1, s, D_KV), lambda k, i: (i, 0, k)),
            pl.BlockSpec((1, CMP_BLOCK, SUBLANES, D_KV), lambda k, i: (k, 0, 0, 0)),
            pl.BlockSpec((1, CMP_BLOCK, D_KV, g * CMP_HIDDEN), lambda k, i: (k, 0, 0, 0)),
            pl.BlockSpec((1, CMP_BLOCK, D_KV, g * CMP_HIDDEN), lambda k, i: (k, 0, 0, 0)),
            pl.BlockSpec((1, CMP_HIDDEN, HEAD_DIM), lambda k, i: (k, 0, 0)),
            pl.BlockSpec((1, CMP_HIDDEN, HEAD_DIM), lambda k, i: (k, 0, 0)),
        ],
        out_specs=[
            pl.BlockSpec((1, 1, g, ns, 4 * HEAD_DIM), lambda k, i: (k, i, 0, 0, 0)),
            pl.BlockSpec((1, 1, g, HEAD_DIM, ns), lambda k, i: (k, i, 0, 0, 0)),
        ],
        out_shape=[
            jax.ShapeDtypeStruct((2, b, g, ns, 4 * HEAD_DIM), BF16),
            jax.ShapeDtypeStruct((2, b, g, HEAD_DIM, ns), BF16),
        ],
        compiler_params=_params("parallel", "parallel"),
        name="compress",
    )(kv_c, pos, w1h, w1l, w2h, w2l)


def _importance_matrix(nc_pad, nb):
    r_s = SLC_BLOCK // CMP_STRIDE
    r_c = CMP_BLOCK // CMP_STRIDE
    mat = np.zeros((nb, nc_pad), np.float32)
    for j in range(nb):
        for m in range(r_s):
            for n in range(r_c):
                i = r_s * j - m - n
                if i >= 0:
                    mat[j, i] += 1.0
    return mat


def _select_kernel(q_ref, kc3_ref, vct_ref, mt_ref, oc_ref, selx_ref, score_ref, rank_ref):
    qi = pl.program_id(2)
    tq = q_ref.shape[1]
    ncp = kc3_ref.shape[3]
    nb = mt_ref.shape[0]
    nh = HEADS_PER_GROUP
    bpt = ATTN_KV // SLC_BLOCK
    half_heads = nh // 2

    t1 = qi * tq + lax.broadcasted_iota(jnp.int32, (1, tq), 1)
    ci = lax.broadcasted_iota(jnp.int32, (ncp, 1), 0)
    visible = (ci * CMP_STRIDE + (CMP_BLOCK - 1)) <= t1
    mask = jnp.concatenate([visible] * half_heads, axis=1)

    def scores(h):
        qt = jnp.concatenate([q_ref[r * HEAD_DIM:(r + 1) * HEAD_DIM, :]
                              for r in range(h * half_heads, (h + 1) * half_heads)], axis=1)
        qh, ql = _split(qt)
        rhs = jnp.concatenate([qh, ql, qh, jnp.zeros_like(qh)], axis=0)
        return _dot(kc3_ref[0, 0, 0], rhs)

    def softmax(s):
        s = jnp.where(mask, s, NEG)
        m = jnp.max(s, axis=0, keepdims=True)
        e = jnp.exp2(s - m)
        den = jnp.sum(e, axis=0, keepdims=True)
        inv = jnp.where(m > 0.5 * NEG, 1.0 / den, 0.0)
        return e * inv

    s_halves = [scores(h) for h in range(2)]
    ps = None
    for h in range(2):
        p = softmax(s_halves[h])
        oc = _dot(vct_ref[0, 0, 0], p.astype(BF16))
        for k in range(half_heads):
            r = h * half_heads + k
            pk = p[:, k * tq:(k + 1) * tq]
            ps = pk if ps is None else ps + pk
            oc_ref[0, 0, 0, r * HEAD_DIM:(r + 1) * HEAD_DIM, :] = oc[:, k * tq:(k + 1) * tq]

    blk = lax.broadcasted_iota(jnp.int32, (nb, 1), 0)
    cur = t1 // SLC_BLOCK
    valid = blk <= cur

    def store_rows(selb):
        for jt in range(nb // bpt):
            rows = jnp.concatenate([selb[jt * bpt:(jt + 1) * bpt], jnp.zeros((SEL_ROWS - bpt, tq), F32)], axis=0)
            selx_ref[0, 0, 0, jt] = rows.astype(selx_ref.dtype)

    need_rank = (qi + 1) * tq > N_SELECT * SLC_BLOCK

    @pl.when(jnp.logical_not(need_rank))
    def _():
        store_rows(jnp.where(valid, 0.0, NEG))

    @pl.when(need_rank)
    def _():
        p1 = ps.astype(BF16)
        r1 = ps - p1.astype(F32)
        p2 = r1.astype(BF16)
        p3 = (r1 - p2.astype(F32)).astype(BF16)
        mt = mt_ref[...]
        imp = _dot(mt, p1) + _dot(mt, p2) + _dot(mt, p3)
        forced = (blk == 0) | (blk > cur - N_LOCAL)
        score_ref[...] = jnp.where(valid, jnp.where(forced, FORCE_SCORE, imp), NEG)
        rank_ref[...] = jnp.zeros(rank_ref.shape, jnp.int32)

    nchunk = nb // SUBLANES
    last_visible = ((qi + 1) * tq - 1) // SLC_BLOCK
    sub = lax.broadcasted_iota(jnp.int32, (SUBLANES, 1), 0)
    for vi in range(nchunk):
        @pl.when(need_rank & (vi * SUBLANES <= last_visible))
        def _():
            for i in range(vi * SUBLANES, (vi + 1) * SUBLANES):
                si = score_ref[i:i + 1, :]
                for v in range(nchunk):
                    rows = slice(SUBLANES * v, SUBLANES * (v + 1))
                    chunk = score_ref[rows, :]
                    if v > vi:
                        beats = jnp.where(si >= chunk, 1, 0)
                    elif v < vi:
                        beats = jnp.where(si > chunk, 1, 0)
                    else:
                        beats = jnp.where(sub > i % SUBLANES, jnp.where(si >= chunk, 1, 0),
                                          jnp.where(si > chunk, 1, 0))
                    rank_ref[rows, :] = rank_ref[rows, :] + beats

    @pl.when(need_rank)
    def _():
        sel = valid & (rank_ref[...] < N_SELECT)
        store_rows(jnp.where(sel, 0.0, NEG))


def _select(qt, kc3, vct, mt, b):
    s = qt.shape[1] // b
    g = N_GROUPS
    ncp = kc3.shape[3]
    nb = mt.shape[0]
    gw = HEADS_PER_GROUP * HEAD_DIM
    n_sel = nb // (ATTN_KV // SLC_BLOCK)
    grid = (b, g, s // ATTN_Q)
    return pl.pallas_call(
        _select_kernel,
        grid=grid,
        in_specs=[
            pl.BlockSpec((gw, ATTN_Q), lambda i, j, k: (j, i * (s // ATTN_Q) + k)),
            pl.BlockSpec((1, 1, 1, ncp, 4 * HEAD_DIM), lambda i, j, k: (0, i, j, 0, 0)),
            pl.BlockSpec((1, 1, 1, HEAD_DIM, ncp), lambda i, j, k: (1, i, j, 0, 0)),
            pl.BlockSpec((nb, ncp), lambda i, j, k: (0, 0)),
        ],
        out_specs=[
            pl.BlockSpec((1, 1, 1, gw, ATTN_Q), lambda i, j, k: (i, j, k, 0, 0)),
            pl.BlockSpec((1, 1, 1, n_sel, SEL_ROWS, ATTN_Q), lambda i, j, k: (i, j, k, 0, 0, 0)),
        ],
        out_shape=[
            jax.ShapeDtypeStruct((b, g, s // ATTN_Q, gw, ATTN_Q), F32),
            jax.ShapeDtypeStruct((b, g, s // ATTN_Q, n_sel, SEL_ROWS, ATTN_Q), BF16),
        ],
        scratch_shapes=[pltpu.VMEM((nb, ATTN_Q), F32), pltpu.VMEM((nb, ATTN_Q), jnp.int32)],
        compiler_params=_params("parallel", "parallel", "parallel"),
        name="cmp_select",
    )(qt, kc3, vct, mt)


def _attn_kernel(q_ref, ks_ref, kw_ref, vs_ref, vw_ref, selx_ref, g_ref, oc_ref, o_ref,
                 m_ref, acc_ref, s_a, s_b, smax_ref, pat_ref):
    grp = pl.program_id(1)
    qi = pl.program_id(2)
    tq = q_ref.shape[1]
    kv = ATTN_KV
    nh = HEADS_PER_GROUP
    n = nh * tq
    slc, win = 0, 1

    qt4 = q_ref[...].astype(BF16)
    qt = jnp.concatenate([qt4[r * HEAD_DIM:(r + 1) * HEAD_DIM] for r in range(nh)], axis=1)
    zq = jnp.zeros_like(qt)
    q2 = jnp.concatenate([jnp.where(grp == 0, qt, zq), jnp.where(grp == 1, qt, zq)], axis=0)
    pad = jnp.zeros((K_AUG - D_KV - SEL_ROWS, n), BF16)

    m_ref[...] = jnp.full(m_ref.shape, NEG, F32)
    acc_ref[...] = jnp.zeros(acc_ref.shape, F32)

    @pl.when(qi == 0)
    def _():
        row = lax.broadcasted_iota(jnp.int32, (kv, tq), 0)
        lane = lax.broadcasted_iota(jnp.int32, (kv, tq), 1)
        pat_ref[PAT_ZERO] = jnp.zeros((kv, tq), F32)
        pat_ref[PAT_CAUSAL] = jnp.where(row <= lane, 0.0, NEG)
        pat_ref[PAT_FAR] = jnp.where(lane < row, 0.0, NEG)
        pat_ref[PAT_NONE] = jnp.full((kv, tq), NEG, F32)

    def lanes4(x):
        return jnp.concatenate([x] * nh, axis=1)

    def rows(j):
        return pl.ds(pl.multiple_of(j * kv, kv), kv)

    def store_scores(s, s_ref, slot):
        s_ref[...] = s
        smax_ref[slot] = jnp.max(s, axis=0, keepdims=True)

    def slc_scores(j):
        rhs = jnp.concatenate([q2, lanes4(selx_ref[0, 0, 0, j]), pad], axis=0)
        return _dot(ks_ref[0, rows(j), :], rhs)

    def produce_slc(s_ref, slot, j):
        store_scores(slc_scores(j), s_ref, slot)

    def produce_diag(s_ref, slot):
        store_scores(slc_scores(qi) + lanes4(pat_ref[PAT_CAUSAL]), s_ref, slot)

    def produce_win(s_ref, slot, j, pat):
        s = _dot(kw_ref[0, rows(j), :], q2)
        store_scores(s if pat is None else s + lanes4(pat_ref[pat]), s_ref, slot)

    def consume(s_ref, slot, v_ref, j, kind):
        m_old = m_ref[kind]
        m_new = jnp.maximum(m_old, smax_ref[slot])
        alpha = jnp.exp2(m_old - m_new)
        p = jnp.exp2(s_ref[...] - m_new)
        v_t = v_ref[0, rows(j), :][:, :V_AUG]
        pv = lax.dot_general(v_t, p.astype(BF16), (((0,), (0,)), ((), ())),
                             preferred_element_type=F32)
        acc_ref[kind] = alpha * acc_ref[kind] + pv
        m_ref[kind] = m_new

    def plain(j):
        return (lambda r, sl: produce_slc(r, sl, j)), (lambda r, sl: consume(r, sl, vs_ref, j, slc))

    def window(j, pat):
        return (lambda r, sl: produce_win(r, sl, j, pat)), (lambda r, sl: consume(r, sl, vw_ref, j, win))

    diag = (produce_diag, lambda r, sl: consume(r, sl, vs_ref, qi, slc))

    def run(stages):
        bufs = ((s_a, 0), (s_b, 1))
        for i, (_, consume_i) in enumerate(stages):
            if i + 1 < len(stages):
                stages[i + 1][0](*bufs[(i + 1) % 2])
            consume_i(*bufs[i % 2])

    @pl.when(qi == 0)
    def _():
        produce_diag(s_a, 0)
        run([diag, window(0, PAT_CAUSAL)])

    @pl.when(qi > 0)
    def _():
        produce_slc(s_a, 0, 0)

        def body(u, carry):
            j = 2 * u
            produce_slc(s_b, 1, j + 1)
            consume(s_a, 0, vs_ref, j, slc)
            produce_slc(s_a, 0, j + 2)
            consume(s_b, 1, vs_ref, j + 1, slc)
            return carry

        lax.fori_loop(0, (qi - 1) // 2, body, 0)

        @pl.when(qi % 2 == 1)
        def _():
            far = jnp.where(qi >= 3, PAT_FAR, PAT_NONE)
            run([plain(qi - 1), diag, window(qi, PAT_CAUSAL), window(qi - 1, None),
                 window(jnp.maximum(qi - 2, 0), far)])

        @pl.when(qi % 2 == 0)
        def _():
            run([plain(qi - 2), plain(qi - 1), diag, window(qi, PAT_CAUSAL), window(qi - 1, None),
                 window(qi - 2, PAT_FAR)])

    gt = _sigmoid(g_ref[0]).T
    o_slc = acc_ref[slc, :HEAD_DIM] * (1.0 / acc_ref[slc, HEAD_DIM:HEAD_DIM + 1])
    o_win = acc_ref[win, :HEAD_DIM] * (1.0 / acc_ref[win, HEAD_DIM:HEAD_DIM + 1])
    outs = []
    for r in range(nh):
        cols = slice(r * tq, (r + 1) * tq)
        o_cmp = oc_ref[0, 0, 0, r * HEAD_DIM:(r + 1) * HEAD_DIM, :]
        outs.append(gt[3 * r:3 * r + 1] * o_cmp + gt[3 * r + 1:3 * r + 2] * o_slc[:, cols]
                    + gt[3 * r + 2:3 * r + 3] * o_win[:, cols])
    o_ref[0] = jnp.concatenate(outs, axis=0).T.astype(o_ref.dtype)


def _attention(qt, ks, kw, vall, selx, gates, oc):
    b, s, _ = ks.shape
    g = N_GROUPS
    gw = HEADS_PER_GROUP * HEAD_DIM
    tq = ATTN_Q
    n = HEADS_PER_GROUP * tq
    n_sel = selx.shape[3]
    return pl.pallas_call(
        _attn_kernel,
        grid=(b, g, s // tq),
        in_specs=[
            pl.BlockSpec((gw, tq), lambda i, j, k: (j, i * (s // tq) + k)),
            pl.BlockSpec((1, s, K_AUG), lambda i, j, k: (i, 0, 0)),
            pl.BlockSpec((1, s, D_KV), lambda i, j, k: (i, 0, 0)),
            pl.BlockSpec((1, s, D_KV), lambda i, j, k: (i, 0, j)),
            pl.BlockSpec((1, s, D_KV), lambda i, j, k: (i, 0, N_GROUPS + j)),
            pl.BlockSpec((1, 1, 1, n_sel, SEL_ROWS, tq), lambda i, j, k: (i, j, k, 0, 0, 0)),
            pl.BlockSpec((1, tq, GATE_LANES), lambda i, j, k: (i, k, j)),
            pl.BlockSpec((1, 1, 1, gw, tq), lambda i, j, k: (i, j, k, 0, 0)),
        ],
        out_specs=pl.BlockSpec((1, tq, gw), lambda i, j, k: (i, k, j)),
        out_shape=jax.ShapeDtypeStruct((b, s, D_ATTN), BF16),
        scratch_shapes=[
            pltpu.VMEM((2, 1, n), F32),
            pltpu.VMEM((2, V_AUG, n), F32),
            pltpu.VMEM((ATTN_KV, n), F32),
            pltpu.VMEM((ATTN_KV, n), F32),
            pltpu.VMEM((2, 1, n), F32),
            pltpu.VMEM((4, ATTN_KV, tq), F32),
        ],
        compiler_params=_params("parallel", "parallel", "arbitrary"),
        name="slc_win_attn",
    )(qt, ks, kw, vall, vall, selx, gates, oc)


def _rnn_kernel(xr_ref, gr_ref, cw_ref, cb_ref, wa_ref, wi_ref, ba_ref, bi_ref, lam_ref, o_ref,
                xprev, hcarry):
    si = pl.program_id(1)
    ts = xr_ref.shape[1]
    d = xr_ref.shape[2]

    @pl.when(si == 0)
    def _():
        xprev[...] = jnp.zeros(xprev.shape, F32)
        hcarry[...] = jnp.zeros(hcarry.shape, F32)

    ng = ts // SUBLANES
    x3 = xr_ref[0].reshape(ng, SUBLANES, d)
    sub = lax.broadcasted_iota(jnp.int32, (ng, SUBLANES, d), 1)
    xc = cb_ref[...] + x3 * cw_ref[CONV_WIDTH - 1:CONV_WIDTH, :]
    for back in range(1, CONV_WIDTH):
        rolled = pltpu.roll(x3, back, 1)
        prev = jnp.concatenate([pltpu.roll(xprev[...], back, 0)[None], rolled[:-1]], axis=0)
        tap = CONV_WIDTH - 1 - back
        xc = xc + jnp.where(sub >= back, rolled, prev) * cw_ref[tap:tap + 1, :]
    xprev[...] = x3[ng - 1]
    xc = xc.reshape(ts, d)

    xcb = xc.astype(BF16)
    n_pack = d // LRU_PACK
    ra = jnp.concatenate(
        [_dot(xcb[:, c * LRU_PACK:(c + 1) * LRU_PACK], wa_ref[c]) for c in range(n_pack)], axis=1)
    ri = jnp.concatenate(
        [_dot(xcb[:, c * LRU_PACK:(c + 1) * LRU_PACK], wi_ref[c]) for c in range(n_pack)], axis=1)
    r = _sigmoid(ra + ba_ref[...])
    ig = _sigmoid(ri + bi_ref[...])
    log_a = -LRU_C * r * jax.nn.softplus(-lam_ref[...])
    a = jnp.exp(log_a)
    one_m_a2 = -jnp.tanh(log_a) * (a * a + 1.0)
    root = jnp.where(one_m_a2 > 0.0, one_m_a2 * lax.rsqrt(one_m_a2), 0.0)
    bt = root * (ig * xc)

    a3 = a.reshape(ng, SUBLANES, d)
    b3 = bt.reshape(ng, SUBLANES, d)
    step = 1
    while step < SUBLANES:
        keep = sub >= step
        a_prev = jnp.where(keep, pltpu.roll(a3, step, 1), 1.0)
        b_prev = jnp.where(keep, pltpu.roll(b3, step, 1), 0.0)
        b3 = b3 + a3 * b_prev
        a3 = a3 * a_prev
        step *= 2
    h = hcarry[0:1, :]
    hs = []
    for k in range(ng):
        hk = b3[k] + a3[k] * h
        hs.append(hk)
        h = hk[SUBLANES - 1:SUBLANES]
    hcarry[0:1, :] = h
    hfull = jnp.concatenate(hs, axis=0)
    o_ref[0] = (hfull * jax.nn.gelu(gr_ref[0])).astype(o_ref.dtype)


def _rnn(xr, gr, cw, cb, wa, wi, ba, bi, lam):
    b, s, d = xr.shape
    ts = RNN_ROWS
    row = lambda a: _resident(a.shape)
    return pl.pallas_call(
        _rnn_kernel,
        grid=(b, s // ts),
        in_specs=[
            pl.BlockSpec((1, ts, d), lambda i, j: (i, j, 0)),
            pl.BlockSpec((1, ts, d), lambda i, j: (i, j, 0)),
            row(cw), row(cb), row(wa), row(wi), row(ba), row(bi), row(lam),
        ],
        out_specs=pl.BlockSpec((1, ts, d), lambda i, j: (i, j, 0)),
        out_shape=jax.ShapeDtypeStruct((b, s, d), BF16),
        scratch_shapes=[pltpu.VMEM((SUBLANES, d), F32), pltpu.VMEM((SUBLANES, d), F32)],
        compiler_params=_params("parallel", "arbitrary"),
        name="rg_lru",
    )(xr, gr, cw, cb, wa, wi, ba, bi, lam)


def _merge_mlp_kernel(x_ref, attn_ref, rnn_ref, ga_ref, gb_ref, wua_ref, wur_ref, wo_ref,
                      n2_ref, w1_ref, w2_ref, fn_ref, o_ref, *, final):
    up_a = _dot(attn_ref[...], wua_ref[...])
    up_r = _dot(rnn_ref[...], wur_ref[...])
    merged = _sigmoid(ga_ref[...]) * up_a + _sigmoid(gb_ref[...]) * up_r
    x = x_ref[...] + _dot(merged.astype(BF16), wo_ref[...])
    var = jnp.mean(x * x, axis=-1, keepdims=True)
    hn = (x * lax.rsqrt(var + EPS) * n2_ref[...]).astype(BF16)
    d_ff = w1_ref.shape[1]
    acc = jnp.zeros_like(x)
    for c in range(d_ff // MLP_FF_CHUNK):
        lo, hi = c * MLP_FF_CHUNK, (c + 1) * MLP_FF_CHUNK
        h1 = jnp.maximum(_dot(hn, w1_ref[:, lo:hi]), 0.0)
        acc = acc + _dot((h1 * h1).astype(BF16), w2_ref[lo:hi, :])
    x = x + acc
    if final:
        var = jnp.mean(x * x, axis=-1, keepdims=True)
        x = x * lax.rsqrt(var + EPS) * fn_ref[...]
    o_ref[...] = x


def _merge_mlp(x2, attn, rnn, ga, gb, wua, wur, wo, n2, w1, w2, fn, final):
    t, d = x2.shape
    tm = MLP_ROWS
    rows = lambda w: pl.BlockSpec((tm, w), lambda i: (i, 0))
    return pl.pallas_call(
        functools.partial(_merge_mlp_kernel, final=final),
        grid=(t // tm,),
        in_specs=[
            rows(d), rows(attn.shape[1]), rows(rnn.shape[1]), rows(d), rows(d),
            _resident(wua.shape), _resident(wur.shape), _resident(wo.shape),
            _resident(n2.shape), _resident(w1.shape), _resident(w2.shape), _resident(fn.shape),
        ],
        out_specs=rows(d),
        out_shape=jax.ShapeDtypeStruct((t, d), F32),
        compiler_params=_params("parallel"),
        name="merge_mlp",
    )(x2, attn, rnn, ga, gb, wua, wur, wo, n2, w1, w2, fn)


def _block_diag_pack(w, pack):
    nblk, k, _ = w.shape
    per = pack // k
    w4 = w.reshape(nblk // per, per, k, k)
    eye = jnp.eye(per, dtype=w.dtype)
    out = jnp.einsum("cpij,pq->cpiqj", w4, eye)
    return out.reshape(nblk // per, pack, pack)


def _in_proj_weights(w_in):
    d = w_in.shape[0]
    sizes = (D_ATTN, D_KV, D_KV, D_KV, D_KV, D_KV, D_KV, N_HEADS * 3, d, d, d, d)
    offs = np.concatenate([[0], np.cumsum(sizes)])
    col = lambda i: w_in[:, offs[i]:offs[i + 1]]
    q, k_c, v_c, k_s, v_s, k_w, v_w, g_nsa, xr, gr, g_a, g_b = [col(i) for i in range(12)]
    per_group = HEADS_PER_GROUP * 3
    gates = jnp.concatenate(
        [jnp.pad(g_nsa[:, g * per_group:(g + 1) * per_group], ((0, 0), (0, GATE_LANES - per_group)))
         for g in range(N_GROUPS)], axis=1)
    groups = [[("transposed", q * (ATTN_SCALE * LOG2_E), F32)],
              [("plain", jnp.concatenate([k_c, v_c], axis=1), F32)],
              [("plain", gates, F32)],
              [("keys", k_s, BF16), ("plain", k_w, BF16)],
              [("values", jnp.concatenate([v_s, v_w], axis=1), BF16)],
              [("plain", xr, F32)], [("plain", gr, F32)], [("plain", g_a, F32)], [("plain", g_b, F32)]]
    n_precise = sum(p[1].shape[1] for grp in groups[:2] for p in grp)
    wcat = jnp.concatenate([p[1] for grp in groups for p in grp], axis=1)
    wh = wcat.astype(BF16)
    wl = (wcat[:, :n_precise] - wh[:, :n_precise].astype(F32)).astype(BF16)
    chunks, start = [], 0
    for grp in groups:
        pieces, off = [], 0
        for kind, cols, _ in grp:
            pieces.append((kind, off, cols.shape[1]))
            off += cols.shape[1]
        chunks.append((start, off, tuple(pieces)))
        start += off
    return wh, wl, tuple(chunks), [p[2] for grp in groups for p in grp], n_precise


def _compress_weights(w1_k, w1_v, w2_k, w2_v, pos_k, pos_v):
    w1 = jnp.stack([w1_k, w1_v]).reshape(2, CMP_BLOCK, HEAD_DIM, CMP_HIDDEN)
    eye = jnp.eye(N_GROUPS, dtype=w1.dtype)
    w1g = jnp.einsum("kpdh,gf->kpgdfh", w1, eye).reshape(
        2, CMP_BLOCK, N_GROUPS * HEAD_DIM, N_GROUPS * CMP_HIDDEN)
    pos = jnp.stack([pos_k, pos_v])
    pos = jnp.concatenate([pos] * N_GROUPS, axis=-1)[:, :, None, :]
    pos = jnp.broadcast_to(pos, (2, CMP_BLOCK, SUBLANES, N_GROUPS * HEAD_DIM))
    w1h, w1l = _split(w1g)
    w2h, w2l = _split(jnp.stack([w2_k, w2_v]))
    return pos, w1h, w1l, w2h, w2l


def kernel(x, norm1_w, w_in, cmp_pos_k, cmp_pos_v, cmp_k_w1, cmp_k_w2, cmp_v_w1, cmp_v_w2, conv_w, conv_b, lru_w_a, lru_b_a, lru_w_i, lru_b_i, lru_lambda, w_up_attn, w_up_rnn, w_out, norm2_w, mlp_w1, mlp_w2, final_norm_w):
    b, s, d = x.shape
    depth = w_in.shape[0]
    t = b * s
    ns = s // CMP_STRIDE
    nb = s // SLC_BLOCK
    assert s % ATTN_Q == 0 and s % RNN_ROWS == 0 and t % PROJ_ROWS == 0 and t % MLP_ROWS == 0
    assert ATTN_Q == ATTN_KV and WINDOW == 2 * ATTN_KV and nb >= N_SELECT
    mt = jnp.asarray(_importance_matrix(ns, nb), BF16)
    x2 = x.reshape(t, d)
    fn = final_norm_w.reshape(1, d)
    for l in range(depth):
        wh, wl, chunks, dtypes, n_precise = _in_proj_weights(w_in[l])
        (qt, kv_c, gates, k_s, k_w, vall, xr, gr, g_a, g_b) = _norm_proj(
            x2, norm1_w[l].reshape(1, d), wh, wl, chunks, dtypes, n_precise)

        kvc, kvct = _compress(
            kv_c.reshape(b, s, 2 * D_KV),
            *_compress_weights(cmp_k_w1[l], cmp_v_w1[l], cmp_k_w2[l], cmp_v_w2[l], cmp_pos_k[l], cmp_pos_v[l]))

        oc, selx = _select(qt, kvc, kvct, mt, b)
        attn = _attention(qt, k_s.reshape(b, s, K_AUG), k_w.reshape(b, s, D_KV),
                          vall.reshape(b, s, 4 * D_KV), selx,
                          gates.reshape(b, s, N_GROUPS * GATE_LANES), oc)

        rnn = _rnn(xr.reshape(b, s, d), gr.reshape(b, s, d),
                   jnp.pad(conv_w[l], ((0, SUBLANES - CONV_WIDTH), (0, 0))), conv_b[l].reshape(1, d),
                   _block_diag_pack(lru_w_a[l], LRU_PACK).astype(BF16),
                   _block_diag_pack(lru_w_i[l], LRU_PACK).astype(BF16),
                   lru_b_a[l].reshape(1, d), lru_b_i[l].reshape(1, d), lru_lambda[l].reshape(1, d))

        x2 = _merge_mlp(x2, attn.reshape(t, D_ATTN), rnn.reshape(t, d), g_a, g_b,
                        w_up_attn[l].astype(BF16), w_up_rnn[l].astype(BF16), w_out[l].astype(BF16),
                        norm2_w[l].reshape(1, d), mlp_w1[l].astype(BF16), mlp_w2[l].astype(BF16),
                        fn, final=(l == depth - 1))
    return x2.reshape(b, s, d)
```

```python
import functools

import numpy as np
import jax
import jax.numpy as jnp
from jax import lax
from jax.experimental import pallas as pl
from jax.experimental.pallas import tpu as pltpu

F32 = jnp.float32
BF16 = jnp.bfloat16

N_HEADS = 8
HEAD_DIM = 64
N_GROUPS = 2
HEADS_PER_GROUP = N_HEADS // N_GROUPS
D_ATTN = N_HEADS * HEAD_DIM
D_KV = N_GROUPS * HEAD_DIM
CMP_BLOCK = 32
CMP_STRIDE = 16
CMP_HIDDEN = 256
SLC_BLOCK = 64
N_SELECT = 16
N_LOCAL = 2
WINDOW = 512
ATTN_SCALE = HEAD_DIM ** -0.5
NEG = -1e30
FORCE_SCORE = 1e4
N_RNN_BLOCKS = 16
CONV_WIDTH = 4
LRU_C = 8.0
EPS = 1e-6
LOG2_E = 1.4426950408889634

VMEM_LIMIT_BYTES = 56 * 1024 * 1024
PROJ_ROWS = 512
MLP_ROWS = 512
MLP_FF_CHUNK = 1024
ATTN_Q = 256
ATTN_KV = 256
ATTN_UNROLL = 4
RNN_ROWS = 256
LRU_PACK = 256
SUBLANES = 8
GATE_LANES = 128
PAT_ZERO, PAT_CAUSAL, PAT_FAR, PAT_NONE = 0, 1, 2, 3
SEL_ROWS = 16
K_AUG = 4 * HEAD_DIM
V_AUG = HEAD_DIM + 16


def _dot(a, b):
    return jnp.dot(a, b, preferred_element_type=F32)


def _sigmoid(x):
    return 0.5 * jnp.tanh(0.5 * x) + 0.5


def _split(a):
    hi = a.astype(BF16)
    lo = (a - hi.astype(F32)).astype(BF16)
    return hi, lo


def _dot3(ah, al, bh, bl):
    return _dot(ah, bh) + _dot(ah, bl) + _dot(al, bh)


def _params(*sem):
    return pltpu.CompilerParams(dimension_semantics=sem, vmem_limit_bytes=VMEM_LIMIT_BYTES)


def _resident(shape):
    nd = len(shape)
    return pl.BlockSpec(shape, lambda *_: (0,) * nd, pipeline_mode=pl.Buffered(1))


def _norm_proj_kernel(x_ref, nw_ref, wh_ref, wl_ref, *out_refs, chunks, n_precise):
    x = x_ref[...]
    tm = x.shape[0]
    var = jnp.mean(x * x, axis=-1, keepdims=True)
    xn = x * lax.rsqrt(var + EPS) * nw_ref[...]
    xh, xl = _split(xn)
    lane = lax.broadcasted_iota(jnp.int32, (tm, D_KV), 1)
    out = iter(out_refs)
    for c_start, c_width, pieces in chunks:
        wh = wh_ref[:, c_start:c_start + c_width]
        prod = _dot(xh, wh)
        if c_start < n_precise:
            prod = prod + _dot(xl, wh) + _dot(xh, wl_ref[:, c_start:c_start + c_width])
        for kind, off, width in pieces:
            o_ref = next(out)
            acc = prod[:, off:off + width]
            if kind == "plain":
                o_ref[...] = acc.astype(o_ref.dtype)
            elif kind == "transposed":
                o_ref[...] = acc.T.astype(o_ref.dtype)
            elif kind == "gelu":
                o_ref[...] = jax.nn.gelu(acc).astype(o_ref.dtype)
            elif kind == "keys":
                row = lax.broadcasted_iota(jnp.int32, (tm, D_KV), 0)
                onehot = jnp.where(lane == (row // SLC_BLOCK) % (ATTN_KV // SLC_BLOCK), 1.0, 0.0)
                o_ref[:, :width] = acc.astype(o_ref.dtype)
                o_ref[:, width:] = onehot.astype(o_ref.dtype)
            else:
                ones_col = jnp.where(lane == HEAD_DIM, 1.0, 0.0)
                for i in range(width // D_KV):
                    both = acc[:, i * D_KV:(i + 1) * D_KV]
                    swapped = pltpu.roll(both, HEAD_DIM, 1)
                    lo = 2 * i * D_KV
                    o_ref[:, lo:lo + D_KV] = jnp.where(lane < HEAD_DIM, both, ones_col).astype(o_ref.dtype)
                    o_ref[:, lo + D_KV:lo + 2 * D_KV] = jnp.where(
                        lane < HEAD_DIM, swapped, ones_col).astype(o_ref.dtype)


def _norm_proj(x2, norm_w, wh, wl, chunks, dtypes, n_precise):
    t, d = x2.shape
    assert PROJ_ROWS % ATTN_KV == 0
    grid = (t // PROJ_ROWS,)
    out_shape, out_specs = [], []
    for (kind, _, w), dt in zip([pc for (_, _, pieces) in chunks for pc in pieces], dtypes):
        if kind == "transposed":
            out_shape.append(jax.ShapeDtypeStruct((w, t), dt))
            out_specs.append(pl.BlockSpec((w, PROJ_ROWS), lambda i: (0, i)))
        else:
            w = 2 * w if kind in ("keys", "values") else w
            out_shape.append(jax.ShapeDtypeStruct((t, w), dt))
            out_specs.append(pl.BlockSpec((PROJ_ROWS, w), lambda i: (i, 0)))
    return pl.pallas_call(
        functools.partial(_norm_proj_kernel, chunks=chunks, n_precise=n_precise),
        grid=grid,
        in_specs=[
            pl.BlockSpec((PROJ_ROWS, d), lambda i: (i, 0)),
            _resident(norm_w.shape),
            _resident(wh.shape),
            _resident(wl.shape),
        ],
        out_specs=out_specs,
        out_shape=out_shape,
        compiler_params=_params("parallel"),
        name="norm_proj",
    )(x2, norm_w, wh, wl)


def _compress_kernel(x_ref, pos_ref, w1h_ref, w1l_ref, w2h_ref, w2l_ref, c3_ref, ct_ref):
    ns = x_ref.shape[1] // CMP_STRIDE

    def times_w1(rows, q):
        rh, rl = _split(rows)
        wh = w1h_ref[0, q]
        return (_dot(jnp.concatenate([rh, rl], axis=1), jnp.concatenate([wh, wh], axis=0))
                + _dot(rh, w1l_ref[0, q]))

    first = [jnp.zeros((ns, CMP_HIDDEN), F32) for _ in range(N_GROUPS)]
    second = [jnp.zeros((ns, CMP_HIDDEN), F32) for _ in range(N_GROUPS)]
    posb = jnp.zeros((SUBLANES, CMP_HIDDEN), F32)
    for p in range(CMP_STRIDE):
        xp = x_ref[0, pl.ds(p, ns, stride=CMP_STRIDE), :]
        for g in range(N_GROUPS):
            xg = xp[:, g * HEAD_DIM:(g + 1) * HEAD_DIM]
            first[g] = first[g] + times_w1(xg, p)
            second[g] = second[g] + times_w1(xg, p + CMP_STRIDE)
        posb = posb + times_w1(pos_ref[0, p], p) + times_w1(pos_ref[0, p + CMP_STRIDE], p + CMP_STRIDE)
    row = lax.broadcasted_iota(jnp.int32, (ns, HEAD_DIM), 0)
    for g in range(N_GROUPS):
        pre = first[g] + pltpu.roll(second[g], ns - 1, 0) + posb[0:1]
        hh, hl = _split(jax.nn.gelu(pre))
        c = _dot3(hh, hl, w2h_ref[0], w2l_ref[0])
        c = jnp.where(row < ns - 1, c, 0.0)
        ch, cl = _split(c)
        c3_ref[0, 0, g] = jnp.concatenate([ch, ch, cl, jnp.zeros_like(ch)], axis=1)
        wide = jnp.concatenate([c, jnp.zeros_like(c)], axis=1)
        ct_ref[0, 0, g] = wide.T[:HEAD_DIM].astype(ct_ref.dtype)


def _compress(kv_c, pos, w1h, w1l, w2h, w2l):
    b, s, _ = kv_c.shape
    ns = s // CMP_STRIDE
    g = N_GROUPS
    return pl.pallas_call(
        _compress_kernel,
        grid=(2, b),
        in_specs=[
            pl.BlockSpec((1, s, D_KV), lambda k, i: (i, 0, k)),
            pl.BlockSpec((1, CMP_BLOCK, SUBLANES, HEAD_DIM), lambda k, i: (k, 0, 0, 0)),
            pl.BlockSpec((1, CMP_BLOCK, HEAD_DIM, CMP_HIDDEN), lambda k, i: (k, 0, 0, 0)),
            pl.BlockSpec((1, CMP_BLOCK, HEAD_DIM, CMP_HIDDEN), lambda k, i: (k, 0, 0, 0)),
            pl.BlockSpec((1, CMP_HIDDEN, HEAD_DIM), lambda k, i: (k, 0, 0)),
            pl.BlockSpec((1, CMP_HIDDEN, HEAD_DIM), lambda k, i: (k, 0, 0)),
        ],
        out_specs=[
            pl.BlockSpec((1, 1, g, ns, 4 * HEAD_DIM), lambda k, i: (k, i, 0, 0, 0)),
            pl.BlockSpec((1, 1, g, HEAD_DIM, ns), lambda k, i: (k, i, 0, 0, 0)),
        ],
        out_shape=[
            jax.ShapeDtypeStruct((2, b, g, ns, 4 * HEAD_DIM), BF16),
            jax.ShapeDtypeStruct((2, b, g, HEAD_DIM, ns), BF16),
        ],
        compiler_params=_params("parallel", "parallel"),
        name="compress",
    )(kv_c, pos, w1h, w1l, w2h, w2l)


def _importance_matrix(nc_pad, nb):
    r_s = SLC_BLOCK // CMP_STRIDE
    r_c = CMP_BLOCK // CMP_STRIDE
    mat = np.zeros((nb, nc_pad), np.float32)
    for j in range(nb):
        for m in range(r_s):
            for n in range(r_c):
                i = r_s * j - m - n
                if i >= 0:
                    mat[j, i] += 1.0
    return mat


def _select_kernel(q_ref, kc3_ref, vct_ref, mt_ref, oc_ref, selx_ref, score_ref, rank_ref):
    qi = pl.program_id(2)
    tq = q_ref.shape[1]
    ncp = kc3_ref.shape[3]
    nb = mt_ref.shape[0]
    nh = HEADS_PER_GROUP
    bpt = ATTN_KV // SLC_BLOCK
    half_heads = nh // 2

    t1 = qi * tq + lax.broadcasted_iota(jnp.int32, (1, tq), 1)
    ci = lax.broadcasted_iota(jnp.int32, (ncp, 1), 0)
    visible = (ci * CMP_STRIDE + (CMP_BLOCK - 1)) <= t1
    mask = jnp.concatenate([visible] * half_heads, axis=1)

    def scores(h):
        qt = jnp.concatenate([q_ref[r * HEAD_DIM:(r + 1) * HEAD_DIM, :]
                              for r in range(h * half_heads, (h + 1) * half_heads)], axis=1)
        qh, ql = _split(qt)
        rhs = jnp.concatenate([qh, ql, qh, jnp.zeros_like(qh)], axis=0)
        return _dot(kc3_ref[0, 0, 0], rhs)

    def softmax(s):
        s = jnp.where(mask, s, NEG)
        m = jnp.max(s, axis=0, keepdims=True)
        e = jnp.exp2(s - m)
        den = jnp.sum(e, axis=0, keepdims=True)
        inv = jnp.where(m > 0.5 * NEG, 1.0 / den, 0.0)
        return e * inv

    s_halves = [scores(h) for h in range(2)]
    ps = None
    for h in range(2):
        p = softmax(s_halves[h])
        oc = _dot(vct_ref[0, 0, 0], p.astype(BF16))
        for k in range(half_heads):
            r = h * half_heads + k
            pk = p[:, k * tq:(k + 1) * tq]
            ps = pk if ps is None else ps + pk
            oc_ref[0, 0, 0, r * HEAD_DIM:(r + 1) * HEAD_DIM, :] = oc[:, k * tq:(k + 1) * tq]

    blk = lax.broadcasted_iota(jnp.int32, (nb, 1), 0)
    cur = t1 // SLC_BLOCK
    valid = blk <= cur

    def store_rows(selb):
        for jt in range(nb // bpt):
            rows = jnp.concatenate([selb[jt * bpt:(jt + 1) * bpt], jnp.zeros((SEL_ROWS - bpt, tq), F32)], axis=0)
            selx_ref[0, 0, 0, jt] = rows.astype(selx_ref.dtype)

    need_rank = (qi + 1) * tq > N_SELECT * SLC_BLOCK

    @pl.when(jnp.logical_not(need_rank))
    def _():
        store_rows(jnp.where(valid, 0.0, NEG))

    @pl.when(need_rank)
    def _():
        p1 = ps.astype(BF16)
        r1 = ps - p1.astype(F32)
        p2 = r1.astype(BF16)
        p3 = (r1 - p2.astype(F32)).astype(BF16)
        mt = mt_ref[...]
        imp = _dot(mt, p1) + _dot(mt, p2) + _dot(mt, p3)
        forced = (blk == 0) | (blk > cur - N_LOCAL)
        score_ref[...] = jnp.where(valid, jnp.where(forced, FORCE_SCORE, imp), NEG)
        rank_ref[...] = jnp.zeros(rank_ref.shape, jnp.int32)

    nchunk = nb // SUBLANES
    last_visible = ((qi + 1) * tq - 1) // SLC_BLOCK
    sub = lax.broadcasted_iota(jnp.int32, (SUBLANES, 1), 0)
    for vi in range(nchunk):
        @pl.when(need_rank & (vi * SUBLANES <= last_visible))
        def _():
            for i in range(vi * SUBLANES, (vi + 1) * SUBLANES):
                si = score_ref[i:i + 1, :]
                for v in range(nchunk):
                    rows = slice(SUBLANES * v, SUBLANES * (v + 1))
                    chunk = score_ref[rows, :]
                    if v > vi:
                        beats = jnp.where(si >= chunk, 1, 0)
                    elif v < vi:
                        beats = jnp.where(si > chunk, 1, 0)
                    else:
                        beats = jnp.where(sub > i % SUBLANES, jnp.where(si >= chunk, 1, 0),
                                          jnp.where(si > chunk, 1, 0))
                    rank_ref[rows, :] = rank_ref[rows, :] + beats

    @pl.when(need_rank)
    def _():
        sel = valid & (rank_ref[...] < N_SELECT)
        store_rows(jnp.where(sel, 0.0, NEG))


def _select(qt, kc3, vct, mt, b):
    s = qt.shape[1] // b
    g = N_GROUPS
    ncp = kc3.shape[3]
    nb = mt.shape[0]
    gw = HEADS_PER_GROUP * HEAD_DIM
    n_sel = nb // (ATTN_KV // SLC_BLOCK)
    grid = (b, g, s // ATTN_Q)
    return pl.pallas_call(
        _select_kernel,
        grid=grid,
        in_specs=[
            pl.BlockSpec((gw, ATTN_Q), lambda i, j, k: (j, i * (s // ATTN_Q) + k)),
            pl.BlockSpec((1, 1, 1, ncp, 4 * HEAD_DIM), lambda i, j, k: (0, i, j, 0, 0)),
            pl.BlockSpec((1, 1, 1, HEAD_DIM, ncp), lambda i, j, k: (1, i, j, 0, 0)),
            pl.BlockSpec((nb, ncp), lambda i, j, k: (0, 0)),
        ],
        out_specs=[
            pl.BlockSpec((1, 1, 1, gw, ATTN_Q), lambda i, j, k: (i, j, k, 0, 0)),
            pl.BlockSpec((1, 1, 1, n_sel, SEL_ROWS, ATTN_Q), lambda i, j, k: (i, j, k, 0, 0, 0)),
        ],
        out_shape=[
            jax.ShapeDtypeStruct((b, g, s // ATTN_Q, gw, ATTN_Q), F32),
            jax.ShapeDtypeStruct((b, g, s // ATTN_Q, n_sel, SEL_ROWS, ATTN_Q), BF16),
        ],
        scratch_shapes=[pltpu.VMEM((nb, ATTN_Q), F32), pltpu.VMEM((nb, ATTN_Q), jnp.int32)],
        compiler_params=_params("parallel", "parallel", "parallel"),
        name="cmp_select",
    )(qt, kc3, vct, mt)


def _attn_kernel(q_ref, ks_ref, kw_ref, vs_ref, vw_ref, selx_ref, g_ref, oc_ref, o_ref,
                 m_ref, acc_ref, s_a, s_b, smax_ref, pat_ref):
    grp = pl.program_id(1)
    qi = pl.program_id(2)
    tq = q_ref.shape[1]
    kv = ATTN_KV
    nh = HEADS_PER_GROUP
    n = nh * tq
    slc, win = 0, 1

    qt4 = q_ref[...].astype(BF16)
    qt = jnp.concatenate([qt4[r * HEAD_DIM:(r + 1) * HEAD_DIM] for r in range(nh)], axis=1)
    zq = jnp.zeros_like(qt)
    q2 = jnp.concatenate([jnp.where(grp == 0, qt, zq), jnp.where(grp == 1, qt, zq)], axis=0)
    pad = jnp.zeros((K_AUG - D_KV - SEL_ROWS, n), BF16)

    m_ref[...] = jnp.full(m_ref.shape, NEG, F32)
    acc_ref[...] = jnp.zeros(acc_ref.shape, F32)

    @pl.when(qi == 0)
    def _():
        row = lax.broadcasted_iota(jnp.int32, (kv, tq), 0)
        lane = lax.broadcasted_iota(jnp.int32, (kv, tq), 1)
        pat_ref[PAT_ZERO] = jnp.zeros((kv, tq), F32)
        pat_ref[PAT_CAUSAL] = jnp.where(row <= lane, 0.0, NEG)
        pat_ref[PAT_FAR] = jnp.where(lane < row, 0.0, NEG)
        pat_ref[PAT_NONE] = jnp.full((kv, tq), NEG, F32)

    def lanes4(x):
        return jnp.concatenate([x] * nh, axis=1)

    def rows(j):
        return pl.ds(pl.multiple_of(j * kv, kv), kv)

    def store_scores(s, s_ref, slot):
        s_ref[...] = s
        smax_ref[slot] = jnp.max(s, axis=0, keepdims=True)

    def slc_scores(j):
        rhs = jnp.concatenate([q2, lanes4(selx_ref[0, 0, 0, j]), pad], axis=0)
        return _dot(ks_ref[0, rows(j), :], rhs)

    def produce_slc(s_ref, slot, j):
        store_scores(slc_scores(j), s_ref, slot)

    def produce_diag(s_ref, slot):
        store_scores(slc_scores(qi) + lanes4(pat_ref[PAT_CAUSAL]), s_ref, slot)

    def produce_win(s_ref, slot, j, pat):
        s = _dot(kw_ref[0, rows(j), :], q2)
        store_scores(s if pat is None else s + lanes4(pat_ref[pat]), s_ref, slot)

    def consume(s_ref, slot, v_ref, j, kind):
        m_old = m_ref[kind]
        m_new = jnp.maximum(m_old, smax_ref[slot])
        alpha = jnp.exp2(m_old - m_new)
        p = jnp.exp2(s_ref[...] - m_new)
        v_t = v_ref[0, rows(j), :][:, :V_AUG]
        pv = lax.dot_general(v_t, p.astype(BF16), (((0,), (0,)), ((), ())),
                             preferred_element_type=F32)
        acc_ref[kind] = alpha * acc_ref[kind] + pv
        m_ref[kind] = m_new

    def plain(j):
        return (lambda r, sl: produce_slc(r, sl, j)), (lambda r, sl: consume(r, sl, vs_ref, j, slc))

    def window(j, pat):
        return (lambda r, sl: produce_win(r, sl, j, pat)), (lambda r, sl: consume(r, sl, vw_ref, j, win))

    diag = (produce_diag, lambda r, sl: consume(r, sl, vs_ref, qi, slc))

    def run(stages):
        bufs = ((s_a, 0), (s_b, 1))
        for i, (_, consume_i) in enumerate(stages):
            if i + 1 < len(stages):
                stages[i + 1][0](*bufs[(i + 1) % 2])
            consume_i(*bufs[i % 2])

    @pl.when(qi == 0)
    def _():
        produce_diag(s_a, 0)
        run([diag, window(0, PAT_CAUSAL)])

    @pl.when(qi > 0)
    def _():
        produce_slc(s_a, 0, 0)

        def body(u, carry):
            j = ATTN_UNROLL * u
            bufs = ((s_a, 0), (s_b, 1))
            for k in range(ATTN_UNROLL):
                produce_slc(*bufs[(k + 1) % 2], j + k + 1)
                consume(*bufs[k % 2], vs_ref, j + k, slc)
            return carry

        n_loop = (qi - 1) // ATTN_UNROLL
        lax.fori_loop(0, n_loop, body, 0)
        left = qi - ATTN_UNROLL * n_loop

        for k in range(1, ATTN_UNROLL + 1):
            @pl.when(left == k)
            def _():
                far = jnp.where(qi >= 2, PAT_FAR, PAT_NONE) if k == 1 else PAT_FAR
                run([plain(qi - k + i) for i in range(k)]
                    + [diag, window(qi, PAT_CAUSAL), window(qi - 1, None), window(jnp.maximum(qi - 2, 0), far)])

    gt = _sigmoid(g_ref[0]).T
    o_slc = acc_ref[slc, :HEAD_DIM] * (1.0 / acc_ref[slc, HEAD_DIM:HEAD_DIM + 1])
    o_win = acc_ref[win, :HEAD_DIM] * (1.0 / acc_ref[win, HEAD_DIM:HEAD_DIM + 1])
    outs = []
    for r in range(nh):
        cols = slice(r * tq, (r + 1) * tq)
        o_cmp = oc_ref[0, 0, 0, r * HEAD_DIM:(r + 1) * HEAD_DIM, :]
        outs.append(gt[3 * r:3 * r + 1] * o_cmp + gt[3 * r + 1:3 * r + 2] * o_slc[:, cols]
                    + gt[3 * r + 2:3 * r + 3] * o_win[:, cols])
    o_ref[0] = jnp.concatenate(outs, axis=0).T.astype(o_ref.dtype)


def _attention(qt, ks, kw, vall, selx, gates, oc):
    b, s, _ = ks.shape
    g = N_GROUPS
    gw = HEADS_PER_GROUP * HEAD_DIM
    tq = ATTN_Q
    n = HEADS_PER_GROUP * tq
    n_sel = selx.shape[3]
    return pl.pallas_call(
        _attn_kernel,
        grid=(b, g, s // tq),
        in_specs=[
            pl.BlockSpec((gw, tq), lambda i, j, k: (j, i * (s // tq) + k)),
            pl.BlockSpec((1, s, K_AUG), lambda i, j, k: (i, 0, 0)),
            pl.BlockSpec((1, s, D_KV), lambda i, j, k: (i, 0, 0)),
            pl.BlockSpec((1, s, D_KV), lambda i, j, k: (i, 0, j)),
            pl.BlockSpec((1, s, D_KV), lambda i, j, k: (i, 0, N_GROUPS + j)),
            pl.BlockSpec((1, 1, 1, n_sel, SEL_ROWS, tq), lambda i, j, k: (i, j, k, 0, 0, 0)),
            pl.BlockSpec((1, tq, GATE_LANES), lambda i, j, k: (i, k, j)),
            pl.BlockSpec((1, 1, 1, gw, tq), lambda i, j, k: (i, j, k, 0, 0)),
        ],
        out_specs=pl.BlockSpec((1, tq, gw), lambda i, j, k: (i, k, j)),
        out_shape=jax.ShapeDtypeStruct((b, s, D_ATTN), BF16),
        scratch_shapes=[
            pltpu.VMEM((2, 1, n), F32),
            pltpu.VMEM((2, V_AUG, n), F32),
            pltpu.VMEM((ATTN_KV, n), F32),
            pltpu.VMEM((ATTN_KV, n), F32),
            pltpu.VMEM((2, 1, n), F32),
            pltpu.VMEM((4, ATTN_KV, tq), F32),
        ],
        compiler_params=_params("parallel", "parallel", "arbitrary"),
        name="slc_win_attn",
    )(qt, ks, kw, vall, vall, selx, gates, oc)


def _rnn_kernel(xr_ref, gr_ref, cw_ref, cb_ref, wa_ref, wi_ref, ba_ref, bi_ref, lam_ref, o_ref,
                xprev, hcarry):
    si = pl.program_id(1)
    ts = xr_ref.shape[1]
    d = xr_ref.shape[2]

    @pl.when(si == 0)
    def _():
        xprev[...] = jnp.zeros(xprev.shape, F32)
        hcarry[...] = jnp.zeros(hcarry.shape, F32)

    ng = ts // SUBLANES
    x3 = xr_ref[0].reshape(ng, SUBLANES, d)
    sub = lax.broadcasted_iota(jnp.int32, (ng, SUBLANES, d), 1)
    xc = cb_ref[...] + x3 * cw_ref[CONV_WIDTH - 1:CONV_WIDTH, :]
    for back in range(1, CONV_WIDTH):
        rolled = pltpu.roll(x3, back, 1)
        prev = jnp.concatenate([pltpu.roll(xprev[...], back, 0)[None], rolled[:-1]], axis=0)
        tap = CONV_WIDTH - 1 - back
        xc = xc + jnp.where(sub >= back, rolled, prev) * cw_ref[tap:tap + 1, :]
    xprev[...] = x3[ng - 1]
    xc = xc.reshape(ts, d)

    xcb = xc.astype(BF16)
    n_pack = d // LRU_PACK
    ra = jnp.concatenate(
        [_dot(xcb[:, c * LRU_PACK:(c + 1) * LRU_PACK], wa_ref[c]) for c in range(n_pack)], axis=1)
    ri = jnp.concatenate(
        [_dot(xcb[:, c * LRU_PACK:(c + 1) * LRU_PACK], wi_ref[c]) for c in range(n_pack)], axis=1)
    r = _sigmoid(ra + ba_ref[...])
    ig = _sigmoid(ri + bi_ref[...])
    log_a = -LRU_C * r * jax.nn.softplus(-lam_ref[...])
    a = jnp.exp(log_a)
    one_m_a2 = -jnp.tanh(log_a) * (a * a + 1.0)
    root = jnp.where(one_m_a2 > 0.0, one_m_a2 * lax.rsqrt(one_m_a2), 0.0)
    bt = root * (ig * xc)

    a3 = a.reshape(ng, SUBLANES, d)
    b3 = bt.reshape(ng, SUBLANES, d)
    step = 1
    while step < SUBLANES:
        keep = sub >= step
        a_prev = jnp.where(keep, pltpu.roll(a3, step, 1), 1.0)
        b_prev = jnp.where(keep, pltpu.roll(b3, step, 1), 0.0)
        b3 = b3 + a3 * b_prev
        a3 = a3 * a_prev
        step *= 2
    h = hcarry[0:1, :]
    hs = []
    for k in range(ng):
        hk = b3[k] + a3[k] * h
        hs.append(hk)
        h = hk[SUBLANES - 1:SUBLANES]
    hcarry[0:1, :] = h
    hfull = jnp.concatenate(hs, axis=0)
    o_ref[0] = (hfull * gr_ref[0]).astype(o_ref.dtype)


def _rnn(xr, gr, cw, cb, wa, wi, ba, bi, lam):
    b, s, d = xr.shape
    ts = RNN_ROWS
    row = lambda a: _resident(a.shape)
    return pl.pallas_call(
        _rnn_kernel,
        grid=(b, s // ts),
        in_specs=[
            pl.BlockSpec((1, ts, d), lambda i, j: (i, j, 0)),
            pl.BlockSpec((1, ts, d), lambda i, j: (i, j, 0)),
            row(cw), row(cb), row(wa), row(wi), row(ba), row(bi), row(lam),
        ],
        out_specs=pl.BlockSpec((1, ts, d), lambda i, j: (i, j, 0)),
        out_shape=jax.ShapeDtypeStruct((b, s, d), BF16),
        scratch_shapes=[pltpu.VMEM((SUBLANES, d), F32), pltpu.VMEM((SUBLANES, d), F32)],
        compiler_params=_params("parallel", "arbitrary"),
        name="rg_lru",
    )(xr, gr, cw, cb, wa, wi, ba, bi, lam)


def _merge_mlp_kernel(x_ref, attn_ref, rnn_ref, ga_ref, gb_ref, wua_ref, wur_ref, wo_ref,
                      n2_ref, w1_ref, w2_ref, fn_ref, o_ref, *, final):
    up_a = _dot(attn_ref[...], wua_ref[...])
    up_r = _dot(rnn_ref[...], wur_ref[...])
    merged = _sigmoid(ga_ref[...]) * up_a + _sigmoid(gb_ref[...]) * up_r
    x = x_ref[...] + _dot(merged.astype(BF16), wo_ref[...])
    var = jnp.mean(x * x, axis=-1, keepdims=True)
    hn = (x * lax.rsqrt(var + EPS) * n2_ref[...]).astype(BF16)
    d_ff = w1_ref.shape[1]
    acc = jnp.zeros_like(x)
    for c in range(d_ff // MLP_FF_CHUNK):
        lo, hi = c * MLP_FF_CHUNK, (c + 1) * MLP_FF_CHUNK
        h1 = jnp.maximum(_dot(hn, w1_ref[:, lo:hi]), 0.0)
        acc = acc + _dot((h1 * h1).astype(BF16), w2_ref[lo:hi, :])
    x = x + acc
    if final:
        var = jnp.mean(x * x, axis=-1, keepdims=True)
        x = x * lax.rsqrt(var + EPS) * fn_ref[...]
    o_ref[...] = x


def _merge_mlp(x2, attn, rnn, ga, gb, wua, wur, wo, n2, w1, w2, fn, final):
    t, d = x2.shape
    tm = MLP_ROWS
    rows = lambda w: pl.BlockSpec((tm, w), lambda i: (i, 0))
    return pl.pallas_call(
        functools.partial(_merge_mlp_kernel, final=final),
        grid=(t // tm,),
        in_specs=[
            rows(d), rows(attn.shape[1]), rows(rnn.shape[1]), rows(d), rows(d),
            _resident(wua.shape), _resident(wur.shape), _resident(wo.shape),
            _resident(n2.shape), _resident(w1.shape), _resident(w2.shape), _resident(fn.shape),
        ],
        out_specs=rows(d),
        out_shape=jax.ShapeDtypeStruct((t, d), F32),
        compiler_params=_params("parallel"),
        name="merge_mlp",
    )(x2, attn, rnn, ga, gb, wua, wur, wo, n2, w1, w2, fn)


def _block_diag_pack(w, pack):
    nblk, k, _ = w.shape
    per = pack // k
    w4 = w.reshape(nblk // per, per, k, k)
    eye = jnp.eye(per, dtype=w.dtype)
    out = jnp.einsum("cpij,pq->cpiqj", w4, eye)
    return out.reshape(nblk // per, pack, pack)


def _in_proj_weights(w_in):
    d = w_in.shape[0]
    sizes = (D_ATTN, D_KV, D_KV, D_KV, D_KV, D_KV, D_KV, N_HEADS * 3, d, d, d, d)
    offs = np.concatenate([[0], np.cumsum(sizes)])
    col = lambda i: w_in[:, offs[i]:offs[i + 1]]
    q, k_c, v_c, k_s, v_s, k_w, v_w, g_nsa, xr, gr, g_a, g_b = [col(i) for i in range(12)]
    per_group = HEADS_PER_GROUP * 3
    gates = jnp.concatenate(
        [jnp.pad(g_nsa[:, g * per_group:(g + 1) * per_group], ((0, 0), (0, GATE_LANES - per_group)))
         for g in range(N_GROUPS)], axis=1)
    groups = [[("transposed", q * (ATTN_SCALE * LOG2_E), F32)],
              [("plain", jnp.concatenate([k_c, v_c], axis=1), F32)],
              [("plain", gates, F32)],
              [("keys", k_s, BF16), ("plain", k_w, BF16)],
              [("values", jnp.concatenate([v_s, v_w], axis=1), BF16)],
              [("plain", xr, F32)], [("gelu", gr, F32)], [("plain", g_a, F32)], [("plain", g_b, F32)]]
    n_precise = sum(p[1].shape[1] for grp in groups[:2] for p in grp)
    wcat = jnp.concatenate([p[1] for grp in groups for p in grp], axis=1)
    wh = wcat.astype(BF16)
    wl = (wcat[:, :n_precise] - wh[:, :n_precise].astype(F32)).astype(BF16)
    chunks, start = [], 0
    for grp in groups:
        pieces, off = [], 0
        for kind, cols, _ in grp:
            pieces.append((kind, off, cols.shape[1]))
            off += cols.shape[1]
        chunks.append((start, off, tuple(pieces)))
        start += off
    return wh, wl, tuple(chunks), [p[2] for grp in groups for p in grp], n_precise


def _compress_weights(w1_k, w1_v, w2_k, w2_v, pos_k, pos_v):
    w1 = jnp.stack([w1_k, w1_v]).reshape(2, CMP_BLOCK, HEAD_DIM, CMP_HIDDEN)
    pos = jnp.stack([pos_k, pos_v])[:, :, None, :]
    pos = jnp.broadcast_to(pos, (2, CMP_BLOCK, SUBLANES, HEAD_DIM))
    w1h, w1l = _split(w1)
    w2h, w2l = _split(jnp.stack([w2_k, w2_v]))
    return pos, w1h, w1l, w2h, w2l


def kernel(x, norm1_w, w_in, cmp_pos_k, cmp_pos_v, cmp_k_w1, cmp_k_w2, cmp_v_w1, cmp_v_w2, conv_w, conv_b, lru_w_a, lru_b_a, lru_w_i, lru_b_i, lru_lambda, w_up_attn, w_up_rnn, w_out, norm2_w, mlp_w1, mlp_w2, final_norm_w):
    b, s, d = x.shape
    depth = w_in.shape[0]
    t = b * s
    ns = s // CMP_STRIDE
    nb = s // SLC_BLOCK
    assert s % ATTN_Q == 0 and s % RNN_ROWS == 0 and t % PROJ_ROWS == 0 and t % MLP_ROWS == 0
    assert ATTN_Q == ATTN_KV and WINDOW == 2 * ATTN_KV and nb >= N_SELECT
    mt = jnp.asarray(_importance_matrix(ns, nb), BF16)
    x2 = x.reshape(t, d)
    fn = final_norm_w.reshape(1, d)
    for l in range(depth):
        wh, wl, chunks, dtypes, n_precise = _in_proj_weights(w_in[l])
        (qt, kv_c, gates, k_s, k_w, vall, xr, gr, g_a, g_b) = _norm_proj(
            x2, norm1_w[l].reshape(1, d), wh, wl, chunks, dtypes, n_precise)

        kvc, kvct = _compress(
            kv_c.reshape(b, s, 2 * D_KV),
            *_compress_weights(cmp_k_w1[l], cmp_v_w1[l], cmp_k_w2[l], cmp_v_w2[l], cmp_pos_k[l], cmp_pos_v[l]))

        oc, selx = _select(qt, kvc, kvct, mt, b)
        attn = _attention(qt, k_s.reshape(b, s, K_AUG), k_w.reshape(b, s, D_KV),
                          vall.reshape(b, s, 4 * D_KV), selx,
                          gates.reshape(b, s, N_GROUPS * GATE_LANES), oc)

        rnn = _rnn(xr.reshape(b, s, d), gr.reshape(b, s, d),
                   jnp.pad(conv_w[l], ((0, SUBLANES - CONV_WIDTH), (0, 0))), conv_b[l].reshape(1, d),
                   _block_diag_pack(lru_w_a[l], LRU_PACK).astype(BF16),
                   _block_diag_pack(lru_w_i[l], LRU_PACK).astype(BF16),
                   lru_b_a[l].reshape(1, d), lru_b_i[l].reshape(1, d), lru_lambda[l].reshape(1, d))

        x2 = _merge_mlp(x2, attn.reshape(t, D_ATTN), rnn.reshape(t, d), g_a, g_b,
                        w_up_attn[l].astype(BF16), w_up_rnn[l].astype(BF16), w_out[l].astype(BF16),
                        norm2_w[l].reshape(1, d), mlp_w1[l].astype(BF16), mlp_w2[l].astype(BF16),
                        fn, final=(l == depth - 1))
    return x2.reshape(b, s, d)
```

```python
import functools

import numpy as np
import jax
import jax.numpy as jnp
from jax import lax
from jax.experimental import pallas as pl
from jax.experimental.pallas import tpu as pltpu

F32 = jnp.float32
BF16 = jnp.bfloat16

N_HEADS = 8
HEAD_DIM = 64
N_GROUPS = 2
HEADS_PER_GROUP = N_HEADS // N_GROUPS
D_ATTN = N_HEADS * HEAD_DIM
D_KV = N_GROUPS * HEAD_DIM
CMP_BLOCK = 32
CMP_STRIDE = 16
CMP_HIDDEN = 256
SLC_BLOCK = 64
N_SELECT = 16
N_LOCAL = 2
WINDOW = 512
ATTN_SCALE = HEAD_DIM ** -0.5
NEG = -1e30
FORCE_SCORE = 1e4
N_RNN_BLOCKS = 16
CONV_WIDTH = 4
LRU_C = 8.0
EPS = 1e-6
LOG2_E = 1.4426950408889634

VMEM_LIMIT_BYTES = 56 * 1024 * 1024
PROJ_ROWS = 512
MLP_ROWS = 512
MLP_FF_CHUNK = 1024
ATTN_Q = 256
ATTN_KV = 256
ATTN_UNROLL = 4
RNN_ROWS = 1024
LRU_PACK = 256
SUBLANES = 8
GATE_LANES = 128
PAT_ZERO, PAT_CAUSAL, PAT_FAR, PAT_NONE = 0, 1, 2, 3
SEL_ROWS = 16
K_AUG = 4 * HEAD_DIM
V_AUG = HEAD_DIM + 16


def _dot(a, b):
    return jnp.dot(a, b, preferred_element_type=F32)


def _sigmoid(x):
    return 0.5 * jnp.tanh(0.5 * x) + 0.5


def _split(a):
    hi = a.astype(BF16)
    lo = (a - hi.astype(F32)).astype(BF16)
    return hi, lo


def _dot3(ah, al, bh, bl):
    return _dot(ah, bh) + _dot(ah, bl) + _dot(al, bh)


def _params(*sem):
    return pltpu.CompilerParams(dimension_semantics=sem, vmem_limit_bytes=VMEM_LIMIT_BYTES)


def _resident(shape):
    nd = len(shape)
    return pl.BlockSpec(shape, lambda *_: (0,) * nd, pipeline_mode=pl.Buffered(1))


def _norm_proj_kernel(x_ref, nw_ref, wh_ref, wl_ref, *out_refs, chunks, n_precise):
    x = x_ref[...]
    tm = x.shape[0]
    var = jnp.mean(x * x, axis=-1, keepdims=True)
    xn = x * lax.rsqrt(var + EPS) * nw_ref[...]
    xh, xl = _split(xn)
    lane = lax.broadcasted_iota(jnp.int32, (tm, D_KV), 1)
    out = iter(out_refs)
    for c_start, c_width, pieces in chunks:
        wh = wh_ref[:, c_start:c_start + c_width]
        prod = _dot(xh, wh)
        if c_start < n_precise:
            prod = prod + _dot(xl, wh) + _dot(xh, wl_ref[:, c_start:c_start + c_width])
        for kind, off, width in pieces:
            o_ref = next(out)
            acc = prod[:, off:off + width]
            if kind == "plain":
                o_ref[...] = acc.astype(o_ref.dtype)
            elif kind == "transposed":
                o_ref[...] = acc.T.astype(o_ref.dtype)
            elif kind == "gelu":
                o_ref[...] = jax.nn.gelu(acc).astype(o_ref.dtype)
            elif kind == "keys":
                row = lax.broadcasted_iota(jnp.int32, (tm, D_KV), 0)
                onehot = jnp.where(lane == (row // SLC_BLOCK) % (ATTN_KV // SLC_BLOCK), 1.0, 0.0)
                o_ref[:, :width] = acc.astype(o_ref.dtype)
                o_ref[:, width:] = onehot.astype(o_ref.dtype)
            else:
                ones_col = jnp.where(lane == HEAD_DIM, 1.0, 0.0)
                for i in range(width // D_KV):
                    both = acc[:, i * D_KV:(i + 1) * D_KV]
                    swapped = pltpu.roll(both, HEAD_DIM, 1)
                    lo = 2 * i * D_KV
                    o_ref[:, lo:lo + D_KV] = jnp.where(lane < HEAD_DIM, both, ones_col).astype(o_ref.dtype)
                    o_ref[:, lo + D_KV:lo + 2 * D_KV] = jnp.where(
                        lane < HEAD_DIM, swapped, ones_col).astype(o_ref.dtype)


def _norm_proj(x2, norm_w, wh, wl, chunks, dtypes, n_precise):
    t, d = x2.shape
    assert PROJ_ROWS % ATTN_KV == 0
    grid = (t // PROJ_ROWS,)
    out_shape, out_specs = [], []
    for (kind, _, w), dt in zip([pc for (_, _, pieces) in chunks for pc in pieces], dtypes):
        if kind == "transposed":
            out_shape.append(jax.ShapeDtypeStruct((w, t), dt))
            out_specs.append(pl.BlockSpec((w, PROJ_ROWS), lambda i: (0, i)))
        else:
            w = 2 * w if kind in ("keys", "values") else w
            out_shape.append(jax.ShapeDtypeStruct((t, w), dt))
            out_specs.append(pl.BlockSpec((PROJ_ROWS, w), lambda i: (i, 0)))
    return pl.pallas_call(
        functools.partial(_norm_proj_kernel, chunks=chunks, n_precise=n_precise),
        grid=grid,
        in_specs=[
            pl.BlockSpec((PROJ_ROWS, d), lambda i: (i, 0)),
            _resident(norm_w.shape),
            _resident(wh.shape),
            _resident(wl.shape),
        ],
        out_specs=out_specs,
        out_shape=out_shape,
        compiler_params=_params("parallel"),
        name="norm_proj",
    )(x2, norm_w, wh, wl)


def _compress_kernel(x_ref, pos_ref, w1h_ref, w1l_ref, w2h_ref, w2l_ref, c3_ref, ct_ref):
    ns = x_ref.shape[1] // CMP_STRIDE

    def times_w1(rows, q):
        rh, rl = _split(rows)
        wh = w1h_ref[0, q]
        return (_dot(jnp.concatenate([rh, rl], axis=1), jnp.concatenate([wh, wh], axis=0))
                + _dot(rh, w1l_ref[0, q]))

    first = [jnp.zeros((ns, CMP_HIDDEN), F32) for _ in range(N_GROUPS)]
    second = [jnp.zeros((ns, CMP_HIDDEN), F32) for _ in range(N_GROUPS)]
    posb = jnp.zeros((SUBLANES, CMP_HIDDEN), F32)
    for p in range(CMP_STRIDE):
        xp = x_ref[0, pl.ds(p, ns, stride=CMP_STRIDE), :]
        for g in range(N_GROUPS):
            xg = xp[:, g * HEAD_DIM:(g + 1) * HEAD_DIM]
            first[g] = first[g] + times_w1(xg, p)
            second[g] = second[g] + times_w1(xg, p + CMP_STRIDE)
        posb = posb + times_w1(pos_ref[0, p], p) + times_w1(pos_ref[0, p + CMP_STRIDE], p + CMP_STRIDE)
    row = lax.broadcasted_iota(jnp.int32, (ns, HEAD_DIM), 0)
    for g in range(N_GROUPS):
        pre = first[g] + pltpu.roll(second[g], ns - 1, 0) + posb[0:1]
        hh, hl = _split(jax.nn.gelu(pre))
        c = _dot3(hh, hl, w2h_ref[0], w2l_ref[0])
        c = jnp.where(row < ns - 1, c, 0.0)
        ch, cl = _split(c)
        c3_ref[0, 0, g] = jnp.concatenate([ch, ch, cl, jnp.zeros_like(ch)], axis=1)
        wide = jnp.concatenate([c, jnp.zeros_like(c)], axis=1)
        ct_ref[0, 0, g] = wide.T[:HEAD_DIM].astype(ct_ref.dtype)


def _compress(kv_c, pos, w1h, w1l, w2h, w2l):
    b, s, _ = kv_c.shape
    ns = s // CMP_STRIDE
    g = N_GROUPS
    return pl.pallas_call(
        _compress_kernel,
        grid=(2, b),
        in_specs=[
            pl.BlockSpec((1, s, D_KV), lambda k, i: (i, 0, k)),
            pl.BlockSpec((1, CMP_BLOCK, SUBLANES, HEAD_DIM), lambda k, i: (k, 0, 0, 0)),
            pl.BlockSpec((1, CMP_BLOCK, HEAD_DIM, CMP_HIDDEN), lambda k, i: (k, 0, 0, 0)),
            pl.BlockSpec((1, CMP_BLOCK, HEAD_DIM, CMP_HIDDEN), lambda k, i: (k, 0, 0, 0)),
            pl.BlockSpec((1, CMP_HIDDEN, HEAD_DIM), lambda k, i: (k, 0, 0)),
            pl.BlockSpec((1, CMP_HIDDEN, HEAD_DIM), lambda k, i: (k, 0, 0)),
        ],
        out_specs=[
            pl.BlockSpec((1, 1, g, ns, 4 * HEAD_DIM), lambda k, i: (k, i, 0, 0, 0)),
            pl.BlockSpec((1, 1, g, HEAD_DIM, ns), lambda k, i: (k, i, 0, 0, 0)),
        ],
        out_shape=[
            jax.ShapeDtypeStruct((2, b, g, ns, 4 * HEAD_DIM), BF16),
            jax.ShapeDtypeStruct((2, b, g, HEAD_DIM, ns), BF16),
        ],
        compiler_params=_params("parallel", "parallel"),
        name="compress",
    )(kv_c, pos, w1h, w1l, w2h, w2l)


def _importance_matrix(nc_pad, nb):
    r_s = SLC_BLOCK // CMP_STRIDE
    r_c = CMP_BLOCK // CMP_STRIDE
    mat = np.zeros((nb, nc_pad), np.float32)
    for j in range(nb):
        for m in range(r_s):
            for n in range(r_c):
                i = r_s * j - m - n
                if i >= 0:
                    mat[j, i] += 1.0
    return mat


def _select_kernel(q_ref, kc3_ref, vct_ref, mt_ref, oc_ref, selx_ref, score_ref, rank_ref):
    qi = pl.program_id(2)
    tq = q_ref.shape[1]
    ncp = kc3_ref.shape[3]
    nb = mt_ref.shape[0]
    nh = HEADS_PER_GROUP
    bpt = ATTN_KV // SLC_BLOCK
    half_heads = nh // 2

    t1 = qi * tq + lax.broadcasted_iota(jnp.int32, (1, tq), 1)
    ci = lax.broadcasted_iota(jnp.int32, (ncp, 1), 0)
    visible = (ci * CMP_STRIDE + (CMP_BLOCK - 1)) <= t1
    mask = jnp.concatenate([visible] * half_heads, axis=1)

    def scores(h):
        qt = jnp.concatenate([q_ref[r * HEAD_DIM:(r + 1) * HEAD_DIM, :]
                              for r in range(h * half_heads, (h + 1) * half_heads)], axis=1)
        qh, ql = _split(qt)
        rhs = jnp.concatenate([qh, ql, qh, jnp.zeros_like(qh)], axis=0)
        return _dot(kc3_ref[0, 0, 0], rhs)

    def softmax(s):
        s = jnp.where(mask, s, NEG)
        m = jnp.max(s, axis=0, keepdims=True)
        e = jnp.exp2(s - m)
        den = jnp.sum(e, axis=0, keepdims=True)
        inv = jnp.where(m > 0.5 * NEG, 1.0 / den, 0.0)
        return e * inv

    s_halves = [scores(h) for h in range(2)]
    ps = None
    for h in range(2):
        p = softmax(s_halves[h])
        oc = _dot(vct_ref[0, 0, 0], p.astype(BF16))
        for k in range(half_heads):
            r = h * half_heads + k
            pk = p[:, k * tq:(k + 1) * tq]
            ps = pk if ps is None else ps + pk
            oc_ref[0, 0, 0, r * HEAD_DIM:(r + 1) * HEAD_DIM, :] = oc[:, k * tq:(k + 1) * tq]

    blk = lax.broadcasted_iota(jnp.int32, (nb, 1), 0)
    cur = t1 // SLC_BLOCK
    valid = blk <= cur

    def store_rows(selb):
        for jt in range(nb // bpt):
            rows = jnp.concatenate([selb[jt * bpt:(jt + 1) * bpt], jnp.zeros((SEL_ROWS - bpt, tq), F32)], axis=0)
            selx_ref[0, 0, 0, jt] = rows.astype(selx_ref.dtype)

    need_rank = (qi + 1) * tq > N_SELECT * SLC_BLOCK

    @pl.when(jnp.logical_not(need_rank))
    def _():
        store_rows(jnp.where(valid, 0.0, NEG))

    @pl.when(need_rank)
    def _():
        p1 = ps.astype(BF16)
        r1 = ps - p1.astype(F32)
        p2 = r1.astype(BF16)
        p3 = (r1 - p2.astype(F32)).astype(BF16)
        mt = mt_ref[...]
        imp = _dot(mt, p1) + _dot(mt, p2) + _dot(mt, p3)
        forced = (blk == 0) | (blk > cur - N_LOCAL)
        score_ref[...] = jnp.where(valid, jnp.where(forced, FORCE_SCORE, imp), NEG)
        rank_ref[...] = jnp.zeros(rank_ref.shape, jnp.int32)

    nchunk = nb // SUBLANES
    last_visible = ((qi + 1) * tq - 1) // SLC_BLOCK
    sub = lax.broadcasted_iota(jnp.int32, (SUBLANES, 1), 0)
    for vi in range(nchunk):
        @pl.when(need_rank & (vi * SUBLANES <= last_visible))
        def _():
            for i in range(vi * SUBLANES, (vi + 1) * SUBLANES):
                si = score_ref[i:i + 1, :]
                for v in range(nchunk):
                    rows = slice(SUBLANES * v, SUBLANES * (v + 1))
                    chunk = score_ref[rows, :]
                    if v > vi:
                        beats = jnp.where(si >= chunk, 1, 0)
                    elif v < vi:
                        beats = jnp.where(si > chunk, 1, 0)
                    else:
                        beats = jnp.where(sub > i % SUBLANES, jnp.where(si >= chunk, 1, 0),
                                          jnp.where(si > chunk, 1, 0))
                    rank_ref[rows, :] = rank_ref[rows, :] + beats

    @pl.when(need_rank)
    def _():
        sel = valid & (rank_ref[...] < N_SELECT)
        store_rows(jnp.where(sel, 0.0, NEG))


def _select(qt, kc3, vct, mt, b):
    s = qt.shape[1] // b
    g = N_GROUPS
    ncp = kc3.shape[3]
    nb = mt.shape[0]
    gw = HEADS_PER_GROUP * HEAD_DIM
    n_sel = nb // (ATTN_KV // SLC_BLOCK)
    grid = (b, g, s // ATTN_Q)
    return pl.pallas_call(
        _select_kernel,
        grid=grid,
        in_specs=[
            pl.BlockSpec((gw, ATTN_Q), lambda i, j, k: (j, i * (s // ATTN_Q) + k)),
            pl.BlockSpec((1, 1, 1, ncp, 4 * HEAD_DIM), lambda i, j, k: (0, i, j, 0, 0)),
            pl.BlockSpec((1, 1, 1, HEAD_DIM, ncp), lambda i, j, k: (1, i, j, 0, 0)),
            pl.BlockSpec((nb, ncp), lambda i, j, k: (0, 0)),
        ],
        out_specs=[
            pl.BlockSpec((1, 1, 1, gw, ATTN_Q), lambda i, j, k: (i, j, k, 0, 0)),
            pl.BlockSpec((1, 1, 1, n_sel, SEL_ROWS, ATTN_Q), lambda i, j, k: (i, j, k, 0, 0, 0)),
        ],
        out_shape=[
            jax.ShapeDtypeStruct((b, g, s // ATTN_Q, gw, ATTN_Q), F32),
            jax.ShapeDtypeStruct((b, g, s // ATTN_Q, n_sel, SEL_ROWS, ATTN_Q), BF16),
        ],
        scratch_shapes=[pltpu.VMEM((nb, ATTN_Q), F32), pltpu.VMEM((nb, ATTN_Q), jnp.int32)],
        compiler_params=_params("parallel", "parallel", "parallel"),
        name="cmp_select",
    )(qt, kc3, vct, mt)


def _attn_kernel(q_ref, ks_ref, kw_ref, vs_ref, vw_ref, selx_ref, g_ref, oc_ref, o_ref,
                 m_ref, acc_ref, s_a, s_b, smax_ref, pat_ref):
    grp = pl.program_id(1)
    qi = pl.program_id(2)
    tq = q_ref.shape[1]
    kv = ATTN_KV
    nh = HEADS_PER_GROUP
    n = nh * tq
    slc, win = 0, 1

    qt4 = q_ref[...].astype(BF16)
    qt = jnp.concatenate([qt4[r * HEAD_DIM:(r + 1) * HEAD_DIM] for r in range(nh)], axis=1)
    zq = jnp.zeros_like(qt)
    q2 = jnp.concatenate([jnp.where(grp == 0, qt, zq), jnp.where(grp == 1, qt, zq)], axis=0)
    pad = jnp.zeros((K_AUG - D_KV - SEL_ROWS, n), BF16)

    m_ref[...] = jnp.full(m_ref.shape, NEG, F32)
    acc_ref[...] = jnp.zeros(acc_ref.shape, F32)

    @pl.when(qi == 0)
    def _():
        row = lax.broadcasted_iota(jnp.int32, (kv, tq), 0)
        lane = lax.broadcasted_iota(jnp.int32, (kv, tq), 1)
        pat_ref[PAT_ZERO] = jnp.zeros((kv, tq), F32)
        pat_ref[PAT_CAUSAL] = jnp.where(row <= lane, 0.0, NEG)
        pat_ref[PAT_FAR] = jnp.where(lane < row, 0.0, NEG)
        pat_ref[PAT_NONE] = jnp.full((kv, tq), NEG, F32)

    def lanes4(x):
        return jnp.concatenate([x] * nh, axis=1)

    def rows(j):
        return pl.ds(pl.multiple_of(j * kv, kv), kv)

    def store_scores(s, s_ref, slot):
        s_ref[...] = s
        smax_ref[slot] = jnp.max(s, axis=0, keepdims=True)

    def slc_scores(j):
        rhs = jnp.concatenate([q2, lanes4(selx_ref[0, 0, 0, j]), pad], axis=0)
        return _dot(ks_ref[0, rows(j), :], rhs)

    def produce_slc(s_ref, slot, j):
        store_scores(slc_scores(j), s_ref, slot)

    def produce_diag(s_ref, slot):
        store_scores(slc_scores(qi) + lanes4(pat_ref[PAT_CAUSAL]), s_ref, slot)

    def produce_win(s_ref, slot, j, pat):
        s = _dot(kw_ref[0, rows(j), :], q2)
        store_scores(s if pat is None else s + lanes4(pat_ref[pat]), s_ref, slot)

    def consume(s_ref, slot, v_ref, j, kind):
        m_old = m_ref[kind]
        m_new = jnp.maximum(m_old, smax_ref[slot])
        alpha = jnp.exp2(m_old - m_new)
        p = jnp.exp2(s_ref[...] - m_new)
        v_t = v_ref[0, rows(j), :][:, :V_AUG]
        pv = lax.dot_general(v_t, p.astype(BF16), (((0,), (0,)), ((), ())),
                             preferred_element_type=F32)
        acc_ref[kind] = alpha * acc_ref[kind] + pv
        m_ref[kind] = m_new

    def plain(j):
        return (lambda r, sl: produce_slc(r, sl, j)), (lambda r, sl: consume(r, sl, vs_ref, j, slc))

    def window(j, pat):
        return (lambda r, sl: produce_win(r, sl, j, pat)), (lambda r, sl: consume(r, sl, vw_ref, j, win))

    diag = (produce_diag, lambda r, sl: consume(r, sl, vs_ref, qi, slc))

    def run(stages):
        bufs = ((s_a, 0), (s_b, 1))
        for i, (_, consume_i) in enumerate(stages):
            if i + 1 < len(stages):
                stages[i + 1][0](*bufs[(i + 1) % 2])
            consume_i(*bufs[i % 2])

    @pl.when(qi == 0)
    def _():
        produce_diag(s_a, 0)
        run([diag, window(0, PAT_CAUSAL)])

    @pl.when(qi > 0)
    def _():
        produce_slc(s_a, 0, 0)

        def body(u, carry):
            j = ATTN_UNROLL * u
            bufs = ((s_a, 0), (s_b, 1))
            for k in range(ATTN_UNROLL):
                produce_slc(*bufs[(k + 1) % 2], j + k + 1)
                consume(*bufs[k % 2], vs_ref, j + k, slc)
            return carry

        n_loop = (qi - 1) // ATTN_UNROLL
        lax.fori_loop(0, n_loop, body, 0)
        left = qi - ATTN_UNROLL * n_loop

        for k in range(1, ATTN_UNROLL + 1):
            @pl.when(left == k)
            def _():
                far = jnp.where(qi >= 2, PAT_FAR, PAT_NONE) if k == 1 else PAT_FAR
                run([plain(qi - k + i) for i in range(k)]
                    + [diag, window(qi, PAT_CAUSAL), window(qi - 1, None), window(jnp.maximum(qi - 2, 0), far)])

    gt = _sigmoid(g_ref[0]).T
    o_slc = acc_ref[slc, :HEAD_DIM] * (1.0 / acc_ref[slc, HEAD_DIM:HEAD_DIM + 1])
    o_win = acc_ref[win, :HEAD_DIM] * (1.0 / acc_ref[win, HEAD_DIM:HEAD_DIM + 1])
    outs = []
    for r in range(nh):
        cols = slice(r * tq, (r + 1) * tq)
        o_cmp = oc_ref[0, 0, 0, r * HEAD_DIM:(r + 1) * HEAD_DIM, :]
        outs.append(gt[3 * r:3 * r + 1] * o_cmp + gt[3 * r + 1:3 * r + 2] * o_slc[:, cols]
                    + gt[3 * r + 2:3 * r + 3] * o_win[:, cols])
    o_ref[0] = jnp.concatenate(outs, axis=0).T.astype(o_ref.dtype)


def _attention(qt, ks, kw, vall, selx, gates, oc):
    b, s, _ = ks.shape
    g = N_GROUPS
    gw = HEADS_PER_GROUP * HEAD_DIM
    tq = ATTN_Q
    n = HEADS_PER_GROUP * tq
    n_sel = selx.shape[3]
    return pl.pallas_call(
        _attn_kernel,
        grid=(b, g, s // tq),
        in_specs=[
            pl.BlockSpec((gw, tq), lambda i, j, k: (j, i * (s // tq) + k)),
            pl.BlockSpec((1, s, K_AUG), lambda i, j, k: (i, 0, 0)),
            pl.BlockSpec((1, s, D_KV), lambda i, j, k: (i, 0, 0)),
            pl.BlockSpec((1, s, D_KV), lambda i, j, k: (i, 0, j)),
            pl.BlockSpec((1, s, D_KV), lambda i, j, k: (i, 0, N_GROUPS + j)),
            pl.BlockSpec((1, 1, 1, n_sel, SEL_ROWS, tq), lambda i, j, k: (i, j, k, 0, 0, 0)),
            pl.BlockSpec((1, tq, GATE_LANES), lambda i, j, k: (i, k, j)),
            pl.BlockSpec((1, 1, 1, gw, tq), lambda i, j, k: (i, j, k, 0, 0)),
        ],
        out_specs=pl.BlockSpec((1, tq, gw), lambda i, j, k: (i, k, j)),
        out_shape=jax.ShapeDtypeStruct((b, s, D_ATTN), BF16),
        scratch_shapes=[
            pltpu.VMEM((2, 1, n), F32),
            pltpu.VMEM((2, V_AUG, n), F32),
            pltpu.VMEM((ATTN_KV, n), F32),
            pltpu.VMEM((ATTN_KV, n), F32),
            pltpu.VMEM((2, 1, n), F32),
            pltpu.VMEM((4, ATTN_KV, tq), F32),
        ],
        compiler_params=_params("parallel", "parallel", "arbitrary"),
        name="slc_win_attn",
    )(qt, ks, kw, vall, vall, selx, gates, oc)


def _rnn_kernel(xr_ref, gr_ref, cw_ref, cb_ref, wa_ref, wi_ref, ba_ref, bi_ref, lam_ref, o_ref,
                xprev, hcarry):
    si = pl.program_id(1)
    ts = xr_ref.shape[1]
    d = xr_ref.shape[2]

    @pl.when(si == 0)
    def _():
        xprev[...] = jnp.zeros(xprev.shape, F32)
        hcarry[...] = jnp.zeros(hcarry.shape, F32)

    ng = ts // SUBLANES
    x3 = xr_ref[0].reshape(ng, SUBLANES, d)
    sub = lax.broadcasted_iota(jnp.int32, (ng, SUBLANES, d), 1)
    xc = cb_ref[...] + x3 * cw_ref[CONV_WIDTH - 1:CONV_WIDTH, :]
    for back in range(1, CONV_WIDTH):
        rolled = pltpu.roll(x3, back, 1)
        prev = jnp.concatenate([pltpu.roll(xprev[...], back, 0)[None], rolled[:-1]], axis=0)
        tap = CONV_WIDTH - 1 - back
        xc = xc + jnp.where(sub >= back, rolled, prev) * cw_ref[tap:tap + 1, :]
    xprev[...] = x3[ng - 1]
    xc = xc.reshape(ts, d)

    xcb = xc.astype(BF16)
    n_pack = d // LRU_PACK
    ra = jnp.concatenate(
        [_dot(xcb[:, c * LRU_PACK:(c + 1) * LRU_PACK], wa_ref[c]) for c in range(n_pack)], axis=1)
    ri = jnp.concatenate(
        [_dot(xcb[:, c * LRU_PACK:(c + 1) * LRU_PACK], wi_ref[c]) for c in range(n_pack)], axis=1)
    tr1 = jnp.tanh(ra + ba_ref[...]) + 1.0
    ig = 0.5 * jnp.tanh(ri + bi_ref[...]) + 0.5
    neg_log_a = tr1 * (0.5 * LRU_C * jax.nn.softplus(-lam_ref[...]))
    a = jnp.exp2(neg_log_a * (-LOG2_E))
    one_m_a2 = jnp.tanh(neg_log_a) * (a * a + 1.0)
    root = jnp.where(one_m_a2 > 0.0, one_m_a2 * lax.rsqrt(one_m_a2), 0.0)
    bt = root * (ig * xc)

    a3 = a.reshape(ng, SUBLANES, d)
    b3 = bt.reshape(ng, SUBLANES, d)
    step = 1
    while step < SUBLANES:
        keep = sub >= step
        a_prev = jnp.where(keep, pltpu.roll(a3, step, 1), 1.0)
        b_prev = jnp.where(keep, pltpu.roll(b3, step, 1), 0.0)
        b3 = b3 + a3 * b_prev
        a3 = a3 * a_prev
        step *= 2
    h = hcarry[0:1, :]
    hs = []
    for k in range(ng):
        hk = b3[k] + a3[k] * h
        hs.append(hk)
        h = hk[SUBLANES - 1:SUBLANES]
    hcarry[0:1, :] = h
    hfull = jnp.concatenate(hs, axis=0)
    o_ref[0] = (hfull * gr_ref[0]).astype(o_ref.dtype)


def _rnn(xr, gr, cw, cb, wa, wi, ba, bi, lam):
    b, s, d = xr.shape
    ts = RNN_ROWS
    row = lambda a: _resident(a.shape)
    return pl.pallas_call(
        _rnn_kernel,
        grid=(b, s // ts),
        in_specs=[
            pl.BlockSpec((1, ts, d), lambda i, j: (i, j, 0)),
            pl.BlockSpec((1, ts, d), lambda i, j: (i, j, 0)),
            row(cw), row(cb), row(wa), row(wi), row(ba), row(bi), row(lam),
        ],
        out_specs=pl.BlockSpec((1, ts, d), lambda i, j: (i, j, 0)),
        out_shape=jax.ShapeDtypeStruct((b, s, d), BF16),
        scratch_shapes=[pltpu.VMEM((SUBLANES, d), F32), pltpu.VMEM((SUBLANES, d), F32)],
        compiler_params=_params("parallel", "arbitrary"),
        name="rg_lru",
    )(xr, gr, cw, cb, wa, wi, ba, bi, lam)


def _merge_mlp_kernel(x_ref, attn_ref, rnn_ref, ga_ref, gb_ref, wua_ref, wur_ref, wo_ref,
                      n2_ref, w1_ref, w2_ref, fn_ref, o_ref, *, final):
    up_a = _dot(attn_ref[...], wua_ref[...])
    up_r = _dot(rnn_ref[...], wur_ref[...])
    merged = _sigmoid(ga_ref[...]) * up_a + _sigmoid(gb_ref[...]) * up_r
    x = x_ref[...] + _dot(merged.astype(BF16), wo_ref[...])
    var = jnp.mean(x * x, axis=-1, keepdims=True)
    hn = (x * lax.rsqrt(var + EPS) * n2_ref[...]).astype(BF16)
    d_ff = w1_ref.shape[1]
    acc = jnp.zeros_like(x)
    for c in range(d_ff // MLP_FF_CHUNK):
        lo, hi = c * MLP_FF_CHUNK, (c + 1) * MLP_FF_CHUNK
        h1 = jnp.maximum(_dot(hn, w1_ref[:, lo:hi]), 0.0)
        acc = acc + _dot((h1 * h1).astype(BF16), w2_ref[lo:hi, :])
    x = x + acc
    if final:
        var = jnp.mean(x * x, axis=-1, keepdims=True)
        x = x * lax.rsqrt(var + EPS) * fn_ref[...]
    o_ref[...] = x


def _merge_mlp(x2, attn, rnn, ga, gb, wua, wur, wo, n2, w1, w2, fn, final):
    t, d = x2.shape
    tm = MLP_ROWS
    rows = lambda w: pl.BlockSpec((tm, w), lambda i: (i, 0))
    return pl.pallas_call(
        functools.partial(_merge_mlp_kernel, final=final),
        grid=(t // tm,),
        in_specs=[
            rows(d), rows(attn.shape[1]), rows(rnn.shape[1]), rows(d), rows(d),
            _resident(wua.shape), _resident(wur.shape), _resident(wo.shape),
            _resident(n2.shape), _resident(w1.shape), _resident(w2.shape), _resident(fn.shape),
        ],
        out_specs=rows(d),
        out_shape=jax.ShapeDtypeStruct((t, d), F32),
        compiler_params=_params("parallel"),
        name="merge_mlp",
    )(x2, attn, rnn, ga, gb, wua, wur, wo, n2, w1, w2, fn)


def _block_diag_pack(w, pack):
    nblk, k, _ = w.shape
    per = pack // k
    w4 = w.reshape(nblk // per, per, k, k)
    eye = jnp.eye(per, dtype=w.dtype)
    out = jnp.einsum("cpij,pq->cpiqj", w4, eye)
    return out.reshape(nblk // per, pack, pack)


def _in_proj_weights(w_in):
    d = w_in.shape[0]
    sizes = (D_ATTN, D_KV, D_KV, D_KV, D_KV, D_KV, D_KV, N_HEADS * 3, d, d, d, d)
    offs = np.concatenate([[0], np.cumsum(sizes)])
    col = lambda i: w_in[:, offs[i]:offs[i + 1]]
    q, k_c, v_c, k_s, v_s, k_w, v_w, g_nsa, xr, gr, g_a, g_b = [col(i) for i in range(12)]
    per_group = HEADS_PER_GROUP * 3
    gates = jnp.concatenate(
        [jnp.pad(g_nsa[:, g * per_group:(g + 1) * per_group], ((0, 0), (0, GATE_LANES - per_group)))
         for g in range(N_GROUPS)], axis=1)
    groups = [[("transposed", q * (ATTN_SCALE * LOG2_E), F32)],
              [("plain", jnp.concatenate([k_c, v_c], axis=1), F32)],
              [("plain", gates, F32)],
              [("keys", k_s, BF16), ("plain", k_w, BF16)],
              [("values", jnp.concatenate([v_s, v_w], axis=1), BF16)],
              [("plain", xr, F32)], [("gelu", gr, F32)], [("plain", g_a, F32)], [("plain", g_b, F32)]]
    n_precise = sum(p[1].shape[1] for grp in groups[:2] for p in grp)
    wcat = jnp.concatenate([p[1] for grp in groups for p in grp], axis=1)
    wh = wcat.astype(BF16)
    wl = (wcat[:, :n_precise] - wh[:, :n_precise].astype(F32)).astype(BF16)
    chunks, start = [], 0
    for grp in groups:
        pieces, off = [], 0
        for kind, cols, _ in grp:
            pieces.append((kind, off, cols.shape[1]))
            off += cols.shape[1]
        chunks.append((start, off, tuple(pieces)))
        start += off
    return wh, wl, tuple(chunks), [p[2] for grp in groups for p in grp], n_precise


def _compress_weights(w1_k, w1_v, w2_k, w2_v, pos_k, pos_v):
    w1 = jnp.stack([w1_k, w1_v]).reshape(2, CMP_BLOCK, HEAD_DIM, CMP_HIDDEN)
    pos = jnp.stack([pos_k, pos_v])[:, :, None, :]
    pos = jnp.broadcast_to(pos, (2, CMP_BLOCK, SUBLANES, HEAD_DIM))
    w1h, w1l = _split(w1)
    w2h, w2l = _split(jnp.stack([w2_k, w2_v]))
    return pos, w1h, w1l, w2h, w2l


def kernel(x, norm1_w, w_in, cmp_pos_k, cmp_pos_v, cmp_k_w1, cmp_k_w2, cmp_v_w1, cmp_v_w2, conv_w, conv_b, lru_w_a, lru_b_a, lru_w_i, lru_b_i, lru_lambda, w_up_attn, w_up_rnn, w_out, norm2_w, mlp_w1, mlp_w2, final_norm_w):
    b, s, d = x.shape
    depth = w_in.shape[0]
    t = b * s
    ns = s // CMP_STRIDE
    nb = s // SLC_BLOCK
    assert s % ATTN_Q == 0 and s % RNN_ROWS == 0 and t % PROJ_ROWS == 0 and t % MLP_ROWS == 0
    assert ATTN_Q == ATTN_KV and WINDOW == 2 * ATTN_KV and nb >= N_SELECT
    mt = jnp.asarray(_importance_matrix(ns, nb), BF16)
    x2 = x.reshape(t, d)
    fn = final_norm_w.reshape(1, d)
    for l in range(depth):
        wh, wl, chunks, dtypes, n_precise = _in_proj_weights(w_in[l])
        (qt, kv_c, gates, k_s, k_w, vall, xr, gr, g_a, g_b) = _norm_proj(
            x2, norm1_w[l].reshape(1, d), wh, wl, chunks, dtypes, n_precise)

        kvc, kvct = _compress(
            kv_c.reshape(b, s, 2 * D_KV),
            *_compress_weights(cmp_k_w1[l], cmp_v_w1[l], cmp_k_w2[l], cmp_v_w2[l], cmp_pos_k[l], cmp_pos_v[l]))

        oc, selx = _select(qt, kvc, kvct, mt, b)
        attn = _attention(qt, k_s.reshape(b, s, K_AUG), k_w.reshape(b, s, D_KV),
                          vall.reshape(b, s, 4 * D_KV), selx,
                          gates.reshape(b, s, N_GROUPS * GATE_LANES), oc)

        rnn = _rnn(xr.reshape(b, s, d), gr.reshape(b, s, d),
                   jnp.pad(conv_w[l], ((0, SUBLANES - CONV_WIDTH), (0, 0))), conv_b[l].reshape(1, d),
                   _block_diag_pack(0.5 * lru_w_a[l], LRU_PACK).astype(BF16),
                   _block_diag_pack(0.5 * lru_w_i[l], LRU_PACK).astype(BF16),
                   0.5 * lru_b_a[l].reshape(1, d), 0.5 * lru_b_i[l].reshape(1, d), lru_lambda[l].reshape(1, d))

        x2 = _merge_mlp(x2, attn.reshape(t, D_ATTN), rnn.reshape(t, d), g_a, g_b,
                        w_up_attn[l].astype(BF16), w_up_rnn[l].astype(BF16), w_out[l].astype(BF16),
                        norm2_w[l].reshape(1, d), mlp_w1[l].astype(BF16), mlp_w2[l].astype(BF16),
                        fn, final=(l == depth - 1))
    return x2.reshape(b, s, d)
```

```python
import functools

import numpy as np
import jax
import jax.numpy as jnp
from jax import lax
from jax.experimental import pallas as pl
from jax.experimental.pallas import tpu as pltpu

F32 = jnp.float32
BF16 = jnp.bfloat16

N_HEADS = 8
HEAD_DIM = 64
N_GROUPS = 2
HEADS_PER_GROUP = N_HEADS // N_GROUPS
D_ATTN = N_HEADS * HEAD_DIM
D_KV = N_GROUPS * HEAD_DIM
CMP_BLOCK = 32
CMP_STRIDE = 16
CMP_HIDDEN = 256
SLC_BLOCK = 64
N_SELECT = 16
N_LOCAL = 2
WINDOW = 512
ATTN_SCALE = HEAD_DIM ** -0.5
NEG = -1e30
FORCE_SCORE = 1e4
N_RNN_BLOCKS = 16
CONV_WIDTH = 4
LRU_C = 8.0
EPS = 1e-6
LOG2_E = 1.4426950408889634

VMEM_LIMIT_BYTES = 56 * 1024 * 1024
PROJ_ROWS = 512
MLP_ROWS = 512
MLP_FF_CHUNK = 1024
ATTN_Q = 256
ATTN_KV = 256
ATTN_UNROLL = 6
RNN_ROWS = 1024
LRU_PACK = 256
SUBLANES = 8
GATE_LANES = 128
PAT_ZERO, PAT_CAUSAL, PAT_FAR, PAT_NONE = 0, 1, 2, 3
SEL_ROWS = 16
K_AUG = 4 * HEAD_DIM
V_AUG = HEAD_DIM + 16


def _dot(a, b):
    return jnp.dot(a, b, preferred_element_type=F32)


def _sigmoid(x):
    return 0.5 * jnp.tanh(0.5 * x) + 0.5


def _split(a):
    hi = a.astype(BF16)
    lo = (a - hi.astype(F32)).astype(BF16)
    return hi, lo


def _dot3(ah, al, bh, bl):
    return _dot(ah, bh) + _dot(ah, bl) + _dot(al, bh)


def _params(*sem):
    return pltpu.CompilerParams(dimension_semantics=sem, vmem_limit_bytes=VMEM_LIMIT_BYTES)


def _resident(shape):
    nd = len(shape)
    return pl.BlockSpec(shape, lambda *_: (0,) * nd, pipeline_mode=pl.Buffered(1))


def _resident_layer(stacked_shape, layer):
    nd = len(stacked_shape)
    return pl.BlockSpec((1,) + tuple(stacked_shape[1:]), lambda *_: (layer,) + (0,) * (nd - 1),
                        pipeline_mode=pl.Buffered(1))


def _norm_proj_kernel(x_ref, nw_ref, wh_ref, wl_ref, *out_refs, chunks, n_precise):
    x = x_ref[...]
    tm = x.shape[0]
    var = jnp.mean(x * x, axis=-1, keepdims=True)
    xn = x * lax.rsqrt(var + EPS) * nw_ref[...]
    xh, xl = _split(xn)
    lane = lax.broadcasted_iota(jnp.int32, (tm, D_KV), 1)
    out = iter(out_refs)
    for c_start, c_width, pieces in chunks:
        wh = wh_ref[:, c_start:c_start + c_width]
        prod = _dot(xh, wh)
        if c_start < n_precise:
            prod = prod + _dot(xl, wh) + _dot(xh, wl_ref[:, c_start:c_start + c_width])
        for kind, off, width in pieces:
            o_ref = next(out)
            acc = prod[:, off:off + width]
            if kind == "plain":
                o_ref[...] = acc.astype(o_ref.dtype)
            elif kind == "transposed":
                o_ref[...] = acc.T.astype(o_ref.dtype)
            elif kind == "gelu":
                o_ref[...] = jax.nn.gelu(acc).astype(o_ref.dtype)
            elif kind == "keys":
                row = lax.broadcasted_iota(jnp.int32, (tm, D_KV), 0)
                onehot = jnp.where(lane == (row // SLC_BLOCK) % (ATTN_KV // SLC_BLOCK), 1.0, 0.0)
                o_ref[:, :width] = acc.astype(o_ref.dtype)
                o_ref[:, width:] = onehot.astype(o_ref.dtype)
            else:
                ones_col = jnp.where(lane == HEAD_DIM, 1.0, 0.0)
                for i in range(width // D_KV):
                    both = acc[:, i * D_KV:(i + 1) * D_KV]
                    swapped = pltpu.roll(both, HEAD_DIM, 1)
                    lo = 2 * i * D_KV
                    o_ref[:, lo:lo + D_KV] = jnp.where(lane < HEAD_DIM, both, ones_col).astype(o_ref.dtype)
                    o_ref[:, lo + D_KV:lo + 2 * D_KV] = jnp.where(
                        lane < HEAD_DIM, swapped, ones_col).astype(o_ref.dtype)


def _norm_proj(x2, norm_w, wh, wl, chunks, dtypes, n_precise):
    t, d = x2.shape
    assert PROJ_ROWS % ATTN_KV == 0
    grid = (t // PROJ_ROWS,)
    out_shape, out_specs = [], []
    for (kind, _, w), dt in zip([pc for (_, _, pieces) in chunks for pc in pieces], dtypes):
        if kind == "transposed":
            out_shape.append(jax.ShapeDtypeStruct((w, t), dt))
            out_specs.append(pl.BlockSpec((w, PROJ_ROWS), lambda i: (0, i)))
        else:
            w = 2 * w if kind in ("keys", "values") else w
            out_shape.append(jax.ShapeDtypeStruct((t, w), dt))
            out_specs.append(pl.BlockSpec((PROJ_ROWS, w), lambda i: (i, 0)))
    return pl.pallas_call(
        functools.partial(_norm_proj_kernel, chunks=chunks, n_precise=n_precise),
        grid=grid,
        in_specs=[
            pl.BlockSpec((PROJ_ROWS, d), lambda i: (i, 0)),
            _resident(norm_w.shape),
            _resident(wh.shape),
            _resident(wl.shape),
        ],
        out_specs=out_specs,
        out_shape=out_shape,
        compiler_params=_params("parallel"),
        name="norm_proj",
    )(x2, norm_w, wh, wl)


def _compress_kernel(x_ref, pos_ref, w1h_ref, w1l_ref, w2h_ref, w2l_ref, c3_ref, ct_ref):
    ns = x_ref.shape[1] // CMP_STRIDE

    def times_w1(rows, q):
        rh, rl = _split(rows)
        wh = w1h_ref[0, q]
        return (_dot(jnp.concatenate([rh, rl], axis=1), jnp.concatenate([wh, wh], axis=0))
                + _dot(rh, w1l_ref[0, q]))

    first = [jnp.zeros((ns, CMP_HIDDEN), F32) for _ in range(N_GROUPS)]
    second = [jnp.zeros((ns, CMP_HIDDEN), F32) for _ in range(N_GROUPS)]
    posb = jnp.zeros((SUBLANES, CMP_HIDDEN), F32)
    for p in range(CMP_STRIDE):
        xp = x_ref[0, pl.ds(p, ns, stride=CMP_STRIDE), :]
        for g in range(N_GROUPS):
            xg = xp[:, g * HEAD_DIM:(g + 1) * HEAD_DIM]
            first[g] = first[g] + times_w1(xg, p)
            second[g] = second[g] + times_w1(xg, p + CMP_STRIDE)
        posb = posb + times_w1(pos_ref[0, p], p) + times_w1(pos_ref[0, p + CMP_STRIDE], p + CMP_STRIDE)
    row = lax.broadcasted_iota(jnp.int32, (ns, HEAD_DIM), 0)
    for g in range(N_GROUPS):
        pre = first[g] + pltpu.roll(second[g], ns - 1, 0) + posb[0:1]
        hh, hl = _split(jax.nn.gelu(pre))
        c = _dot3(hh, hl, w2h_ref[0], w2l_ref[0])
        c = jnp.where(row < ns - 1, c, 0.0)
        ch, cl = _split(c)
        c3_ref[0, 0, g] = jnp.concatenate([ch, ch, cl, jnp.zeros_like(ch)], axis=1)
        wide = jnp.concatenate([c, jnp.zeros_like(c)], axis=1)
        ct_ref[0, 0, g] = wide.T[:HEAD_DIM].astype(ct_ref.dtype)


def _compress(kv_c, pos, w1h, w1l, w2h, w2l):
    b, s, _ = kv_c.shape
    ns = s // CMP_STRIDE
    g = N_GROUPS
    return pl.pallas_call(
        _compress_kernel,
        grid=(2, b),
        in_specs=[
            pl.BlockSpec((1, s, D_KV), lambda k, i: (i, 0, k)),
            pl.BlockSpec((1, CMP_BLOCK, SUBLANES, HEAD_DIM), lambda k, i: (k, 0, 0, 0)),
            pl.BlockSpec((1, CMP_BLOCK, HEAD_DIM, CMP_HIDDEN), lambda k, i: (k, 0, 0, 0)),
            pl.BlockSpec((1, CMP_BLOCK, HEAD_DIM, CMP_HIDDEN), lambda k, i: (k, 0, 0, 0)),
            pl.BlockSpec((1, CMP_HIDDEN, HEAD_DIM), lambda k, i: (k, 0, 0)),
            pl.BlockSpec((1, CMP_HIDDEN, HEAD_DIM), lambda k, i: (k, 0, 0)),
        ],
        out_specs=[
            pl.BlockSpec((1, 1, g, ns, 4 * HEAD_DIM), lambda k, i: (k, i, 0, 0, 0)),
            pl.BlockSpec((1, 1, g, HEAD_DIM, ns), lambda k, i: (k, i, 0, 0, 0)),
        ],
        out_shape=[
            jax.ShapeDtypeStruct((2, b, g, ns, 4 * HEAD_DIM), BF16),
            jax.ShapeDtypeStruct((2, b, g, HEAD_DIM, ns), BF16),
        ],
        compiler_params=_params("parallel", "parallel"),
        name="compress",
    )(kv_c, pos, w1h, w1l, w2h, w2l)


def _importance_matrix(nc_pad, nb):
    r_s = SLC_BLOCK // CMP_STRIDE
    r_c = CMP_BLOCK // CMP_STRIDE
    mat = np.zeros((nb, nc_pad), np.float32)
    for j in range(nb):
        for m in range(r_s):
            for n in range(r_c):
                i = r_s * j - m - n
                if i >= 0:
                    mat[j, i] += 1.0
    return mat


def _select_kernel(q_ref, kc3_ref, vct_ref, mt_ref, oc_ref, selx_ref, score_ref, rank_ref):
    qi = pl.program_id(2)
    tq = q_ref.shape[1]
    ncp = kc3_ref.shape[3]
    nb = mt_ref.shape[0]
    nh = HEADS_PER_GROUP
    bpt = ATTN_KV // SLC_BLOCK
    half_heads = nh // 2

    t1 = qi * tq + lax.broadcasted_iota(jnp.int32, (1, tq), 1)
    ci = lax.broadcasted_iota(jnp.int32, (ncp, 1), 0)
    visible = (ci * CMP_STRIDE + (CMP_BLOCK - 1)) <= t1
    mask = jnp.concatenate([visible] * half_heads, axis=1)

    def scores(h):
        qt = jnp.concatenate([q_ref[r * HEAD_DIM:(r + 1) * HEAD_DIM, :]
                              for r in range(h * half_heads, (h + 1) * half_heads)], axis=1)
        qh, ql = _split(qt)
        rhs = jnp.concatenate([qh, ql, qh, jnp.zeros_like(qh)], axis=0)
        return _dot(kc3_ref[0, 0, 0], rhs)

    def softmax(s):
        s = jnp.where(mask, s, NEG)
        m = jnp.max(s, axis=0, keepdims=True)
        e = jnp.exp2(s - m)
        den = jnp.sum(e, axis=0, keepdims=True)
        inv = jnp.where(m > 0.5 * NEG, 1.0 / den, 0.0)
        return e * inv

    s_halves = [scores(h) for h in range(2)]
    ps = None
    for h in range(2):
        p = softmax(s_halves[h])
        oc = _dot(vct_ref[0, 0, 0], p.astype(BF16))
        for k in range(half_heads):
            r = h * half_heads + k
            pk = p[:, k * tq:(k + 1) * tq]
            ps = pk if ps is None else ps + pk
            oc_ref[0, 0, 0, r * HEAD_DIM:(r + 1) * HEAD_DIM, :] = oc[:, k * tq:(k + 1) * tq]

    blk = lax.broadcasted_iota(jnp.int32, (nb, 1), 0)
    cur = t1 // SLC_BLOCK
    valid = blk <= cur

    def store_rows(selb):
        for jt in range(nb // bpt):
            rows = jnp.concatenate([selb[jt * bpt:(jt + 1) * bpt], jnp.zeros((SEL_ROWS - bpt, tq), F32)], axis=0)
            selx_ref[0, 0, 0, jt] = rows.astype(selx_ref.dtype)

    need_rank = (qi + 1) * tq > N_SELECT * SLC_BLOCK

    @pl.when(jnp.logical_not(need_rank))
    def _():
        store_rows(jnp.where(valid, 0.0, NEG))

    @pl.when(need_rank)
    def _():
        p1 = ps.astype(BF16)
        r1 = ps - p1.astype(F32)
        p2 = r1.astype(BF16)
        p3 = (r1 - p2.astype(F32)).astype(BF16)
        mt = mt_ref[...]
        imp = _dot(mt, p1) + _dot(mt, p2) + _dot(mt, p3)
        forced = (blk == 0) | (blk > cur - N_LOCAL)
        score_ref[...] = jnp.where(valid, jnp.where(forced, FORCE_SCORE, imp), NEG)
        rank_ref[...] = jnp.zeros(rank_ref.shape, jnp.int32)

    nchunk = nb // SUBLANES
    last_visible = ((qi + 1) * tq - 1) // SLC_BLOCK
    sub = lax.broadcasted_iota(jnp.int32, (SUBLANES, 1), 0)
    for vi in range(nchunk):
        @pl.when(need_rank & (vi * SUBLANES <= last_visible))
        def _():
            for i in range(vi * SUBLANES, (vi + 1) * SUBLANES):
                si = score_ref[i:i + 1, :]
                for v in range(nchunk):
                    rows = slice(SUBLANES * v, SUBLANES * (v + 1))
                    chunk = score_ref[rows, :]
                    if v > vi:
                        beats = jnp.where(si >= chunk, 1, 0)
                    elif v < vi:
                        beats = jnp.where(si > chunk, 1, 0)
                    else:
                        beats = jnp.where(sub > i % SUBLANES, jnp.where(si >= chunk, 1, 0),
                                          jnp.where(si > chunk, 1, 0))
                    rank_ref[rows, :] = rank_ref[rows, :] + beats

    @pl.when(need_rank)
    def _():
        sel = valid & (rank_ref[...] < N_SELECT)
        store_rows(jnp.where(sel, 0.0, NEG))


def _select(qt, kc3, vct, mt, b):
    s = qt.shape[1] // b
    g = N_GROUPS
    ncp = kc3.shape[3]
    nb = mt.shape[0]
    gw = HEADS_PER_GROUP * HEAD_DIM
    n_sel = nb // (ATTN_KV // SLC_BLOCK)
    grid = (b, g, s // ATTN_Q)
    return pl.pallas_call(
        _select_kernel,
        grid=grid,
        in_specs=[
            pl.BlockSpec((gw, ATTN_Q), lambda i, j, k: (j, i * (s // ATTN_Q) + k)),
            pl.BlockSpec((1, 1, 1, ncp, 4 * HEAD_DIM), lambda i, j, k: (0, i, j, 0, 0)),
            pl.BlockSpec((1, 1, 1, HEAD_DIM, ncp), lambda i, j, k: (1, i, j, 0, 0)),
            pl.BlockSpec((nb, ncp), lambda i, j, k: (0, 0)),
        ],
        out_specs=[
            pl.BlockSpec((1, 1, 1, gw, ATTN_Q), lambda i, j, k: (i, j, k, 0, 0)),
            pl.BlockSpec((1, 1, 1, n_sel, SEL_ROWS, ATTN_Q), lambda i, j, k: (i, j, k, 0, 0, 0)),
        ],
        out_shape=[
            jax.ShapeDtypeStruct((b, g, s // ATTN_Q, gw, ATTN_Q), F32),
            jax.ShapeDtypeStruct((b, g, s // ATTN_Q, n_sel, SEL_ROWS, ATTN_Q), BF16),
        ],
        scratch_shapes=[pltpu.VMEM((nb, ATTN_Q), F32), pltpu.VMEM((nb, ATTN_Q), jnp.int32)],
        compiler_params=_params("parallel", "parallel", "parallel"),
        name="cmp_select",
    )(qt, kc3, vct, mt)


def _attn_kernel(q_ref, ks_ref, kw_ref, vs_ref, vw_ref, selx_ref, g_ref, oc_ref, o_ref,
                 m_ref, acc_ref, s_a, s_b, smax_ref, pat_ref):
    grp = pl.program_id(1)
    qi = pl.program_id(2)
    tq = q_ref.shape[1]
    kv = ATTN_KV
    nh = HEADS_PER_GROUP
    n = nh * tq
    slc, win = 0, 1

    qt4 = q_ref[...].astype(BF16)
    qt = jnp.concatenate([qt4[r * HEAD_DIM:(r + 1) * HEAD_DIM] for r in range(nh)], axis=1)
    zq = jnp.zeros_like(qt)
    q2 = jnp.concatenate([jnp.where(grp == 0, qt, zq), jnp.where(grp == 1, qt, zq)], axis=0)
    pad = jnp.zeros((K_AUG - D_KV - SEL_ROWS, n), BF16)

    m_ref[...] = jnp.full(m_ref.shape, NEG, F32)
    acc_ref[...] = jnp.zeros(acc_ref.shape, F32)

    @pl.when(qi == 0)
    def _():
        row = lax.broadcasted_iota(jnp.int32, (kv, tq), 0)
        lane = lax.broadcasted_iota(jnp.int32, (kv, tq), 1)
        pat_ref[PAT_ZERO] = jnp.zeros((kv, tq), F32)
        pat_ref[PAT_CAUSAL] = jnp.where(row <= lane, 0.0, NEG)
        pat_ref[PAT_FAR] = jnp.where(lane < row, 0.0, NEG)
        pat_ref[PAT_NONE] = jnp.full((kv, tq), NEG, F32)

    def lanes4(x):
        return jnp.concatenate([x] * nh, axis=1)

    def rows(j):
        return pl.ds(pl.multiple_of(j * kv, kv), kv)

    def store_scores(s, s_ref, slot):
        s_ref[...] = s
        smax_ref[slot] = jnp.max(s, axis=0, keepdims=True)

    def slc_scores(j):
        rhs = jnp.concatenate([q2, lanes4(selx_ref[0, 0, 0, j]), pad], axis=0)
        return _dot(ks_ref[0, rows(j), :], rhs)

    def produce_slc(s_ref, slot, j):
        store_scores(slc_scores(j), s_ref, slot)

    def produce_diag(s_ref, slot):
        store_scores(slc_scores(qi) + lanes4(pat_ref[PAT_CAUSAL]), s_ref, slot)

    def produce_win(s_ref, slot, j, pat):
        s = _dot(kw_ref[0, rows(j), :], q2)
        store_scores(s if pat is None else s + lanes4(pat_ref[pat]), s_ref, slot)

    def consume(s_ref, slot, v_ref, j, kind):
        m_old = m_ref[kind]
        m_new = jnp.maximum(m_old, smax_ref[slot])
        alpha = jnp.exp2(m_old - m_new)
        p = jnp.exp2(s_ref[...] - m_new)
        v_t = v_ref[0, rows(j), :][:, :V_AUG]
        pv = lax.dot_general(v_t, p.astype(BF16), (((0,), (0,)), ((), ())),
                             preferred_element_type=F32)
        acc_ref[kind] = alpha * acc_ref[kind] + pv
        m_ref[kind] = m_new

    def plain(j):
        return (lambda r, sl: produce_slc(r, sl, j)), (lambda r, sl: consume(r, sl, vs_ref, j, slc))

    def window(j, pat):
        return (lambda r, sl: produce_win(r, sl, j, pat)), (lambda r, sl: consume(r, sl, vw_ref, j, win))

    diag = (produce_diag, lambda r, sl: consume(r, sl, vs_ref, qi, slc))

    def run(stages):
        bufs = ((s_a, 0), (s_b, 1))
        for i, (_, consume_i) in enumerate(stages):
            if i + 1 < len(stages):
                stages[i + 1][0](*bufs[(i + 1) % 2])
            consume_i(*bufs[i % 2])

    @pl.when(qi == 0)
    def _():
        produce_diag(s_a, 0)
        run([diag, window(0, PAT_CAUSAL)])

    @pl.when(qi > 0)
    def _():
        produce_slc(s_a, 0, 0)

        def body(u, carry):
            j = ATTN_UNROLL * u
            bufs = ((s_a, 0), (s_b, 1))
            for k in range(ATTN_UNROLL):
                produce_slc(*bufs[(k + 1) % 2], j + k + 1)
                consume(*bufs[k % 2], vs_ref, j + k, slc)
            return carry

        n_loop = (qi - 1) // ATTN_UNROLL
        lax.fori_loop(0, n_loop, body, 0)
        left = qi - ATTN_UNROLL * n_loop

        for k in range(1, ATTN_UNROLL + 1):
            @pl.when(left == k)
            def _():
                far = jnp.where(qi >= 2, PAT_FAR, PAT_NONE) if k == 1 else PAT_FAR
                run([plain(qi - k + i) for i in range(k)]
                    + [diag, window(qi, PAT_CAUSAL), window(qi - 1, None), window(jnp.maximum(qi - 2, 0), far)])

    gt = _sigmoid(g_ref[0]).T
    o_slc = acc_ref[slc, :HEAD_DIM] * (1.0 / acc_ref[slc, HEAD_DIM:HEAD_DIM + 1])
    o_win = acc_ref[win, :HEAD_DIM] * (1.0 / acc_ref[win, HEAD_DIM:HEAD_DIM + 1])
    outs = []
    for r in range(nh):
        cols = slice(r * tq, (r + 1) * tq)
        o_cmp = oc_ref[0, 0, 0, r * HEAD_DIM:(r + 1) * HEAD_DIM, :]
        outs.append(gt[3 * r:3 * r + 1] * o_cmp + gt[3 * r + 1:3 * r + 2] * o_slc[:, cols]
                    + gt[3 * r + 2:3 * r + 3] * o_win[:, cols])
    o_ref[0] = jnp.concatenate(outs, axis=0).T.astype(o_ref.dtype)


def _attention(qt, ks, kw, vall, selx, gates, oc):
    b, s, _ = ks.shape
    g = N_GROUPS
    gw = HEADS_PER_GROUP * HEAD_DIM
    tq = ATTN_Q
    n = HEADS_PER_GROUP * tq
    n_sel = selx.shape[3]
    return pl.pallas_call(
        _attn_kernel,
        grid=(b, g, s // tq),
        in_specs=[
            pl.BlockSpec((gw, tq), lambda i, j, k: (j, i * (s // tq) + k)),
            pl.BlockSpec((1, s, K_AUG), lambda i, j, k: (i, 0, 0)),
            pl.BlockSpec((1, s, D_KV), lambda i, j, k: (i, 0, 0)),
            pl.BlockSpec((1, s, D_KV), lambda i, j, k: (i, 0, j)),
            pl.BlockSpec((1, s, D_KV), lambda i, j, k: (i, 0, N_GROUPS + j)),
            pl.BlockSpec((1, 1, 1, n_sel, SEL_ROWS, tq), lambda i, j, k: (i, j, k, 0, 0, 0)),
            pl.BlockSpec((1, tq, GATE_LANES), lambda i, j, k: (i, k, j)),
            pl.BlockSpec((1, 1, 1, gw, tq), lambda i, j, k: (i, j, k, 0, 0)),
        ],
        out_specs=pl.BlockSpec((1, tq, gw), lambda i, j, k: (i, k, j)),
        out_shape=jax.ShapeDtypeStruct((b, s, D_ATTN), BF16),
        scratch_shapes=[
            pltpu.VMEM((2, 1, n), F32),
            pltpu.VMEM((2, V_AUG, n), F32),
            pltpu.VMEM((ATTN_KV, n), F32),
            pltpu.VMEM((ATTN_KV, n), F32),
            pltpu.VMEM((2, 1, n), F32),
            pltpu.VMEM((4, ATTN_KV, tq), F32),
        ],
        compiler_params=_params("parallel", "parallel", "arbitrary"),
        name="slc_win_attn",
    )(qt, ks, kw, vall, vall, selx, gates, oc)


def _rnn_kernel(xr_ref, gr_ref, cw_ref, cb_ref, wa_ref, wi_ref, ba_ref, bi_ref, lam_ref, o_ref,
                xprev, hcarry):
    si = pl.program_id(1)
    ts = xr_ref.shape[1]
    d = xr_ref.shape[2]

    @pl.when(si == 0)
    def _():
        xprev[...] = jnp.zeros(xprev.shape, F32)
        hcarry[...] = jnp.zeros(hcarry.shape, F32)

    ng = ts // SUBLANES
    x3 = xr_ref[0].reshape(ng, SUBLANES, d)
    sub = lax.broadcasted_iota(jnp.int32, (ng, SUBLANES, d), 1)
    xc = cb_ref[...] + x3 * cw_ref[CONV_WIDTH - 1:CONV_WIDTH, :]
    for back in range(1, CONV_WIDTH):
        rolled = pltpu.roll(x3, back, 1)
        prev = jnp.concatenate([pltpu.roll(xprev[...], back, 0)[None], rolled[:-1]], axis=0)
        tap = CONV_WIDTH - 1 - back
        xc = xc + jnp.where(sub >= back, rolled, prev) * cw_ref[tap:tap + 1, :]
    xprev[...] = x3[ng - 1]
    xc = xc.reshape(ts, d)

    xcb = xc.astype(BF16)
    n_pack = d // LRU_PACK
    ra = jnp.concatenate(
        [_dot(xcb[:, c * LRU_PACK:(c + 1) * LRU_PACK], wa_ref[c]) for c in range(n_pack)], axis=1)
    ri = jnp.concatenate(
        [_dot(xcb[:, c * LRU_PACK:(c + 1) * LRU_PACK], wi_ref[c]) for c in range(n_pack)], axis=1)
    tr1 = jnp.tanh(ra + ba_ref[...]) + 1.0
    ig = 0.5 * jnp.tanh(ri + bi_ref[...]) + 0.5
    neg_log_a = tr1 * (0.5 * LRU_C * jax.nn.softplus(-lam_ref[...]))
    a = jnp.exp2(neg_log_a * (-LOG2_E))
    one_m_a2 = jnp.tanh(neg_log_a) * (a * a + 1.0)
    root = jnp.where(one_m_a2 > 0.0, one_m_a2 * lax.rsqrt(one_m_a2), 0.0)
    bt = root * (ig * xc)

    a3 = a.reshape(ng, SUBLANES, d)
    b3 = bt.reshape(ng, SUBLANES, d)
    step = 1
    while step < SUBLANES:
        keep = sub >= step
        a_prev = jnp.where(keep, pltpu.roll(a3, step, 1), 1.0)
        b_prev = jnp.where(keep, pltpu.roll(b3, step, 1), 0.0)
        b3 = b3 + a3 * b_prev
        a3 = a3 * a_prev
        step *= 2
    h = hcarry[0:1, :]
    hs = []
    for k in range(ng):
        hk = b3[k] + a3[k] * h
        hs.append(hk)
        h = hk[SUBLANES - 1:SUBLANES]
    hcarry[0:1, :] = h
    hfull = jnp.concatenate(hs, axis=0)
    o_ref[0] = (hfull * gr_ref[0]).astype(o_ref.dtype)


def _rnn(xr, gr, cw, cb, wa, wi, ba, bi, lam):
    b, s, d = xr.shape
    ts = RNN_ROWS
    row = lambda a: _resident(a.shape)
    return pl.pallas_call(
        _rnn_kernel,
        grid=(b, s // ts),
        in_specs=[
            pl.BlockSpec((1, ts, d), lambda i, j: (i, j, 0)),
            pl.BlockSpec((1, ts, d), lambda i, j: (i, j, 0)),
            row(cw), row(cb), row(wa), row(wi), row(ba), row(bi), row(lam),
        ],
        out_specs=pl.BlockSpec((1, ts, d), lambda i, j: (i, j, 0)),
        out_shape=jax.ShapeDtypeStruct((b, s, d), BF16),
        scratch_shapes=[pltpu.VMEM((SUBLANES, d), F32), pltpu.VMEM((SUBLANES, d), F32)],
        compiler_params=_params("parallel", "arbitrary"),
        name="rg_lru",
    )(xr, gr, cw, cb, wa, wi, ba, bi, lam)


def _merge_mlp_kernel(x_ref, attn_ref, rnn_ref, ga_ref, gb_ref, wua_ref, wur_ref, wo_ref,
                      n2_ref, w1_ref, w2_ref, fn_ref, o_ref, *, final):
    up_a = _dot(attn_ref[...], wua_ref[0])
    up_r = _dot(rnn_ref[...], wur_ref[0])
    merged = _sigmoid(ga_ref[...]) * up_a + _sigmoid(gb_ref[...]) * up_r
    x = x_ref[...] + _dot(merged.astype(BF16), wo_ref[0])
    var = jnp.mean(x * x, axis=-1, keepdims=True)
    hn = (x * lax.rsqrt(var + EPS) * n2_ref[...]).astype(BF16)
    d_ff = w1_ref.shape[2]
    acc = jnp.zeros_like(x)
    for c in range(d_ff // MLP_FF_CHUNK):
        lo, hi = c * MLP_FF_CHUNK, (c + 1) * MLP_FF_CHUNK
        h1 = jnp.maximum(_dot(hn, w1_ref[0, :, lo:hi]), 0.0)
        acc = acc + _dot((h1 * h1).astype(BF16), w2_ref[0, lo:hi, :])
    x = x + acc
    if final:
        var = jnp.mean(x * x, axis=-1, keepdims=True)
        x = x * lax.rsqrt(var + EPS) * fn_ref[...]
    o_ref[...] = x


def _merge_mlp(x2, attn, rnn, ga, gb, wua, wur, wo, n2, w1, w2, fn, layer, final):
    t, d = x2.shape
    tm = MLP_ROWS
    rows = lambda w: pl.BlockSpec((tm, w), lambda i: (i, 0))
    return pl.pallas_call(
        functools.partial(_merge_mlp_kernel, final=final),
        grid=(t // tm,),
        in_specs=[
            rows(d), rows(attn.shape[1]), rows(rnn.shape[1]), rows(d), rows(d),
            _resident_layer(wua.shape, layer), _resident_layer(wur.shape, layer), _resident_layer(wo.shape, layer),
            _resident(n2.shape), _resident_layer(w1.shape, layer), _resident_layer(w2.shape, layer),
            _resident(fn.shape),
        ],
        out_specs=rows(d),
        out_shape=jax.ShapeDtypeStruct((t, d), F32),
        compiler_params=_params("parallel"),
        name="merge_mlp",
    )(x2, attn, rnn, ga, gb, wua, wur, wo, n2, w1, w2, fn)


def _block_diag_pack(w, pack):
    nblk, k, _ = w.shape
    per = pack // k
    w4 = w.reshape(nblk // per, per, k, k)
    eye = jnp.eye(per, dtype=w.dtype)
    out = jnp.einsum("cpij,pq->cpiqj", w4, eye)
    return out.reshape(nblk // per, pack, pack)


def _in_proj_weights(w_in):
    d = w_in.shape[0]
    sizes = (D_ATTN, D_KV, D_KV, D_KV, D_KV, D_KV, D_KV, N_HEADS * 3, d, d, d, d)
    offs = np.concatenate([[0], np.cumsum(sizes)])
    col = lambda i: w_in[:, offs[i]:offs[i + 1]]
    q, k_c, v_c, k_s, v_s, k_w, v_w, g_nsa, xr, gr, g_a, g_b = [col(i) for i in range(12)]
    per_group = HEADS_PER_GROUP * 3
    gates = jnp.concatenate(
        [jnp.pad(g_nsa[:, g * per_group:(g + 1) * per_group], ((0, 0), (0, GATE_LANES - per_group)))
         for g in range(N_GROUPS)], axis=1)
    groups = [[("transposed", q * (ATTN_SCALE * LOG2_E), F32)],
              [("plain", jnp.concatenate([k_c, v_c], axis=1), F32)],
              [("plain", gates, F32)],
              [("keys", k_s, BF16), ("plain", k_w, BF16)],
              [("values", jnp.concatenate([v_s, v_w], axis=1), BF16)],
              [("plain", xr, F32)], [("gelu", gr, F32)], [("plain", g_a, F32)], [("plain", g_b, F32)]]
    n_precise = sum(p[1].shape[1] for grp in groups[:2] for p in grp)
    wcat = jnp.concatenate([p[1] for grp in groups for p in grp], axis=1)
    wh = wcat.astype(BF16)
    wl = (wcat[:, :n_precise] - wh[:, :n_precise].astype(F32)).astype(BF16)
    chunks, start = [], 0
    for grp in groups:
        pieces, off = [], 0
        for kind, cols, _ in grp:
            pieces.append((kind, off, cols.shape[1]))
            off += cols.shape[1]
        chunks.append((start, off, tuple(pieces)))
        start += off
    return wh, wl, tuple(chunks), [p[2] for grp in groups for p in grp], n_precise


def _compress_weights(w1_k, w1_v, w2_k, w2_v, pos_k, pos_v):
    w1 = jnp.stack([w1_k, w1_v]).reshape(2, CMP_BLOCK, HEAD_DIM, CMP_HIDDEN)
    pos = jnp.stack([pos_k, pos_v])[:, :, None, :]
    pos = jnp.broadcast_to(pos, (2, CMP_BLOCK, SUBLANES, HEAD_DIM))
    w1h, w1l = _split(w1)
    w2h, w2l = _split(jnp.stack([w2_k, w2_v]))
    return pos, w1h, w1l, w2h, w2l


def kernel(x, norm1_w, w_in, cmp_pos_k, cmp_pos_v, cmp_k_w1, cmp_k_w2, cmp_v_w1, cmp_v_w2, conv_w, conv_b, lru_w_a, lru_b_a, lru_w_i, lru_b_i, lru_lambda, w_up_attn, w_up_rnn, w_out, norm2_w, mlp_w1, mlp_w2, final_norm_w):
    b, s, d = x.shape
    depth = w_in.shape[0]
    t = b * s
    ns = s // CMP_STRIDE
    nb = s // SLC_BLOCK
    assert s % ATTN_Q == 0 and s % RNN_ROWS == 0 and t % PROJ_ROWS == 0 and t % MLP_ROWS == 0
    assert ATTN_Q == ATTN_KV and WINDOW == 2 * ATTN_KV and nb >= N_SELECT
    mt = jnp.asarray(_importance_matrix(ns, nb), BF16)
    x2 = x.reshape(t, d)
    fn = final_norm_w.reshape(1, d)
    wua, wur, wo, w1, w2 = (w.astype(BF16) for w in (w_up_attn, w_up_rnn, w_out, mlp_w1, mlp_w2))
    for l in range(depth):
        wh, wl, chunks, dtypes, n_precise = _in_proj_weights(w_in[l])
        (qt, kv_c, gates, k_s, k_w, vall, xr, gr, g_a, g_b) = _norm_proj(
            x2, norm1_w[l].reshape(1, d), wh, wl, chunks, dtypes, n_precise)

        kvc, kvct = _compress(
            kv_c.reshape(b, s, 2 * D_KV),
            *_compress_weights(cmp_k_w1[l], cmp_v_w1[l], cmp_k_w2[l], cmp_v_w2[l], cmp_pos_k[l], cmp_pos_v[l]))

        oc, selx = _select(qt, kvc, kvct, mt, b)
        attn = _attention(qt, k_s.reshape(b, s, K_AUG), k_w.reshape(b, s, D_KV),
                          vall.reshape(b, s, 4 * D_KV), selx,
                          gates.reshape(b, s, N_GROUPS * GATE_LANES), oc)

        rnn = _rnn(xr.reshape(b, s, d), gr.reshape(b, s, d),
                   jnp.pad(conv_w[l], ((0, SUBLANES - CONV_WIDTH), (0, 0))), conv_b[l].reshape(1, d),
                   _block_diag_pack(0.5 * lru_w_a[l], LRU_PACK).astype(BF16),
                   _block_diag_pack(0.5 * lru_w_i[l], LRU_PACK).astype(BF16),
                   0.5 * lru_b_a[l].reshape(1, d), 0.5 * lru_b_i[l].reshape(1, d), lru_lambda[l].reshape(1, d))

        x2 = _merge_mlp(x2, attn.reshape(t, D_ATTN), rnn.reshape(t, d), g_a, g_b, wua, wur, wo,
                        norm2_w[l].reshape(1, d), w1, w2, fn, layer=l, final=(l == depth - 1))
    return x2.reshape(b, s, d)
```

```python
import functools

import numpy as np
import jax
import jax.numpy as jnp
from jax import lax
from jax.experimental import pallas as pl
from jax.experimental.pallas import tpu as pltpu

F32 = jnp.float32
BF16 = jnp.bfloat16

N_HEADS = 8
HEAD_DIM = 64
N_GROUPS = 2
HEADS_PER_GROUP = N_HEADS // N_GROUPS
D_ATTN = N_HEADS * HEAD_DIM
D_KV = N_GROUPS * HEAD_DIM
CMP_BLOCK = 32
CMP_STRIDE = 16
CMP_HIDDEN = 256
SLC_BLOCK = 64
N_SELECT = 16
N_LOCAL = 2
WINDOW = 512
ATTN_SCALE = HEAD_DIM ** -0.5
NEG = -1e30
FORCE_SCORE = 1e4
N_RNN_BLOCKS = 16
CONV_WIDTH = 4
LRU_C = 8.0
EPS = 1e-6
LOG2_E = 1.4426950408889634

VMEM_LIMIT_BYTES = 56 * 1024 * 1024
PROJ_ROWS = 512
MLP_ROWS = 512
MLP_FF_CHUNK = 1024
ATTN_Q = 256
ATTN_KV = 256
ATTN_UNROLL = 6
RNN_ROWS = 1024
LRU_PACK = 256
SUBLANES = 8
GATE_LANES = 128
PAT_ZERO, PAT_CAUSAL, PAT_FAR, PAT_NONE = 0, 1, 2, 3
SEL_ROWS = 16
K_AUG = 4 * HEAD_DIM
V_AUG = HEAD_DIM + 16


def _dot(a, b):
    return jnp.dot(a, b, preferred_element_type=F32)


def _sigmoid(x):
    return 0.5 * jnp.tanh(0.5 * x) + 0.5


def _split(a):
    hi = a.astype(BF16)
    lo = (a - hi.astype(F32)).astype(BF16)
    return hi, lo


def _dot3(ah, al, bh, bl):
    return _dot(ah, bh) + _dot(ah, bl) + _dot(al, bh)


def _params(*sem):
    return pltpu.CompilerParams(dimension_semantics=sem, vmem_limit_bytes=VMEM_LIMIT_BYTES)


def _resident(shape):
    nd = len(shape)
    return pl.BlockSpec(shape, lambda *_: (0,) * nd, pipeline_mode=pl.Buffered(1))


def _resident_layer(stacked_shape, layer):
    nd = len(stacked_shape)
    return pl.BlockSpec((1,) + tuple(stacked_shape[1:]), lambda *_: (layer,) + (0,) * (nd - 1),
                        pipeline_mode=pl.Buffered(1))


def _norm_proj_kernel(x_ref, nw_ref, *refs, chunks, n_weights):
    w_refs, out_refs = refs[:n_weights], refs[n_weights:]
    x = x_ref[...]
    tm = x.shape[0]
    var = jnp.mean(x * x, axis=-1, keepdims=True)
    xn = x * lax.rsqrt(var + EPS) * nw_ref[...]
    xh, xl = _split(xn)
    lane = lax.broadcasted_iota(jnp.int32, (tm, D_KV), 1)
    for w_idx, lo_idx, c_start, c_width, pieces in chunks:
        wh = w_refs[w_idx][0, :, c_start:c_start + c_width]
        prod = _dot(xh, wh)
        if lo_idx is not None:
            prod = prod + _dot(xl, wh) + _dot(xh, w_refs[lo_idx][0, :, c_start:c_start + c_width])
        for kind, off, width, out_idx, out_off in pieces:
            o_ref = out_refs[out_idx]
            acc = prod[:, off:off + width]
            if kind == "plain":
                o_ref[...] = acc.astype(o_ref.dtype)
            elif kind == "transposed":
                o_ref[...] = acc.T.astype(o_ref.dtype)
            elif kind == "gelu":
                o_ref[...] = jax.nn.gelu(acc).astype(o_ref.dtype)
            elif kind == "keys":
                row = lax.broadcasted_iota(jnp.int32, (tm, D_KV), 0)
                onehot = jnp.where(lane == (row // SLC_BLOCK) % (ATTN_KV // SLC_BLOCK), 1.0, 0.0)
                o_ref[:, :width] = acc.astype(o_ref.dtype)
                o_ref[:, width:] = onehot.astype(o_ref.dtype)
            else:
                ones_col = jnp.where(lane == HEAD_DIM, 1.0, 0.0)
                swapped = pltpu.roll(acc, HEAD_DIM, 1)
                o_ref[:, out_off:out_off + D_KV] = jnp.where(lane < HEAD_DIM, acc, ones_col).astype(o_ref.dtype)
                o_ref[:, out_off + D_KV:out_off + 2 * D_KV] = jnp.where(
                    lane < HEAD_DIM, swapped, ones_col).astype(o_ref.dtype)


def _norm_proj(x2, norm_w, weights, layer, chunks, outputs):
    t, d = x2.shape
    assert PROJ_ROWS % ATTN_KV == 0
    grid = (t // PROJ_ROWS,)
    out_shape, out_specs = [], []
    for w, dt, transposed in outputs:
        if transposed:
            out_shape.append(jax.ShapeDtypeStruct((w, t), dt))
            out_specs.append(pl.BlockSpec((w, PROJ_ROWS), lambda i: (0, i)))
        else:
            out_shape.append(jax.ShapeDtypeStruct((t, w), dt))
            out_specs.append(pl.BlockSpec((PROJ_ROWS, w), lambda i: (i, 0)))
    return pl.pallas_call(
        functools.partial(_norm_proj_kernel, chunks=chunks, n_weights=len(weights)),
        grid=grid,
        in_specs=[pl.BlockSpec((PROJ_ROWS, d), lambda i: (i, 0)), _resident(norm_w.shape)]
        + [_resident_layer(w.shape, layer) for w in weights],
        out_specs=out_specs,
        out_shape=out_shape,
        compiler_params=_params("parallel"),
        name="norm_proj",
    )(x2, norm_w, *weights)


def _compress_kernel(x_ref, pos_ref, w1h_ref, w1l_ref, w2h_ref, w2l_ref, c3_ref, ct_ref):
    ns = x_ref.shape[1] // CMP_STRIDE

    def times_w1(rows, q):
        rh, rl = _split(rows)
        wh = w1h_ref[0, q]
        return (_dot(jnp.concatenate([rh, rl], axis=1), jnp.concatenate([wh, wh], axis=0))
                + _dot(rh, w1l_ref[0, q]))

    first = [jnp.zeros((ns, CMP_HIDDEN), F32) for _ in range(N_GROUPS)]
    second = [jnp.zeros((ns, CMP_HIDDEN), F32) for _ in range(N_GROUPS)]
    posb = jnp.zeros((SUBLANES, CMP_HIDDEN), F32)
    for p in range(CMP_STRIDE):
        xp = x_ref[0, pl.ds(p, ns, stride=CMP_STRIDE), :]
        for g in range(N_GROUPS):
            xg = xp[:, g * HEAD_DIM:(g + 1) * HEAD_DIM]
            first[g] = first[g] + times_w1(xg, p)
            second[g] = second[g] + times_w1(xg, p + CMP_STRIDE)
        posb = posb + times_w1(pos_ref[0, p], p) + times_w1(pos_ref[0, p + CMP_STRIDE], p + CMP_STRIDE)
    row = lax.broadcasted_iota(jnp.int32, (ns, HEAD_DIM), 0)
    for g in range(N_GROUPS):
        pre = first[g] + pltpu.roll(second[g], ns - 1, 0) + posb[0:1]
        hh, hl = _split(jax.nn.gelu(pre))
        c = _dot3(hh, hl, w2h_ref[0], w2l_ref[0])
        c = jnp.where(row < ns - 1, c, 0.0)
        ch, cl = _split(c)
        c3_ref[0, 0, g] = jnp.concatenate([ch, ch, cl, jnp.zeros_like(ch)], axis=1)
        wide = jnp.concatenate([c, jnp.zeros_like(c)], axis=1)
        ct_ref[0, 0, g] = wide.T[:HEAD_DIM].astype(ct_ref.dtype)


def _compress(kv_c, pos, w1h, w1l, w2h, w2l):
    b, s, _ = kv_c.shape
    ns = s // CMP_STRIDE
    g = N_GROUPS
    return pl.pallas_call(
        _compress_kernel,
        grid=(2, b),
        in_specs=[
            pl.BlockSpec((1, s, D_KV), lambda k, i: (i, 0, k)),
            pl.BlockSpec((1, CMP_BLOCK, SUBLANES, HEAD_DIM), lambda k, i: (k, 0, 0, 0)),
            pl.BlockSpec((1, CMP_BLOCK, HEAD_DIM, CMP_HIDDEN), lambda k, i: (k, 0, 0, 0)),
            pl.BlockSpec((1, CMP_BLOCK, HEAD_DIM, CMP_HIDDEN), lambda k, i: (k, 0, 0, 0)),
            pl.BlockSpec((1, CMP_HIDDEN, HEAD_DIM), lambda k, i: (k, 0, 0)),
            pl.BlockSpec((1, CMP_HIDDEN, HEAD_DIM), lambda k, i: (k, 0, 0)),
        ],
        out_specs=[
            pl.BlockSpec((1, 1, g, ns, 4 * HEAD_DIM), lambda k, i: (k, i, 0, 0, 0)),
            pl.BlockSpec((1, 1, g, HEAD_DIM, ns), lambda k, i: (k, i, 0, 0, 0)),
        ],
        out_shape=[
            jax.ShapeDtypeStruct((2, b, g, ns, 4 * HEAD_DIM), BF16),
            jax.ShapeDtypeStruct((2, b, g, HEAD_DIM, ns), BF16),
        ],
        compiler_params=_params("parallel", "parallel"),
        name="compress",
    )(kv_c, pos, w1h, w1l, w2h, w2l)


def _importance_matrix(nc_pad, nb):
    r_s = SLC_BLOCK // CMP_STRIDE
    r_c = CMP_BLOCK // CMP_STRIDE
    mat = np.zeros((nb, nc_pad), np.float32)
    for j in range(nb):
        for m in range(r_s):
            for n in range(r_c):
                i = r_s * j - m - n
                if i >= 0:
                    mat[j, i] += 1.0
    return mat


def _select_kernel(q_ref, kc3_ref, vct_ref, mt_ref, oc_ref, selx_ref, score_ref, rank_ref):
    qi = pl.program_id(2)
    tq = q_ref.shape[1]
    ncp = kc3_ref.shape[3]
    nb = mt_ref.shape[0]
    nh = HEADS_PER_GROUP
    bpt = ATTN_KV // SLC_BLOCK
    half_heads = nh // 2

    t1 = qi * tq + lax.broadcasted_iota(jnp.int32, (1, tq), 1)
    ci = lax.broadcasted_iota(jnp.int32, (ncp, 1), 0)
    visible = (ci * CMP_STRIDE + (CMP_BLOCK - 1)) <= t1
    mask = jnp.concatenate([visible] * half_heads, axis=1)

    def scores(h):
        qt = jnp.concatenate([q_ref[r * HEAD_DIM:(r + 1) * HEAD_DIM, :]
                              for r in range(h * half_heads, (h + 1) * half_heads)], axis=1)
        qh, ql = _split(qt)
        rhs = jnp.concatenate([qh, ql, qh, jnp.zeros_like(qh)], axis=0)
        return _dot(kc3_ref[0, 0, 0], rhs)

    def softmax(s):
        s = jnp.where(mask, s, NEG)
        m = jnp.max(s, axis=0, keepdims=True)
        e = jnp.exp2(s - m)
        den = jnp.sum(e, axis=0, keepdims=True)
        inv = jnp.where(m > 0.5 * NEG, 1.0 / den, 0.0)
        return e * inv

    s_halves = [scores(h) for h in range(2)]
    ps = None
    for h in range(2):
        p = softmax(s_halves[h])
        oc = _dot(vct_ref[0, 0, 0], p.astype(BF16))
        for k in range(half_heads):
            r = h * half_heads + k
            pk = p[:, k * tq:(k + 1) * tq]
            ps = pk if ps is None else ps + pk
            oc_ref[0, 0, 0, r * HEAD_DIM:(r + 1) * HEAD_DIM, :] = oc[:, k * tq:(k + 1) * tq]

    blk = lax.broadcasted_iota(jnp.int32, (nb, 1), 0)
    cur = t1 // SLC_BLOCK
    valid = blk <= cur

    def store_rows(selb):
        for jt in range(nb // bpt):
            rows = jnp.concatenate([selb[jt * bpt:(jt + 1) * bpt], jnp.zeros((SEL_ROWS - bpt, tq), F32)], axis=0)
            selx_ref[0, 0, 0, jt] = rows.astype(selx_ref.dtype)

    need_rank = (qi + 1) * tq > N_SELECT * SLC_BLOCK

    @pl.when(jnp.logical_not(need_rank))
    def _():
        store_rows(jnp.where(valid, 0.0, NEG))

    @pl.when(need_rank)
    def _():
        p1 = ps.astype(BF16)
        r1 = ps - p1.astype(F32)
        p2 = r1.astype(BF16)
        p3 = (r1 - p2.astype(F32)).astype(BF16)
        mt = mt_ref[...]
        imp = _dot(mt, p1) + _dot(mt, p2) + _dot(mt, p3)
        forced = (blk == 0) | (blk > cur - N_LOCAL)
        score_ref[...] = jnp.where(valid, jnp.where(forced, FORCE_SCORE, imp), NEG)
        rank_ref[...] = jnp.zeros(rank_ref.shape, jnp.int32)

    nchunk = nb // SUBLANES
    last_visible = ((qi + 1) * tq - 1) // SLC_BLOCK
    sub = lax.broadcasted_iota(jnp.int32, (SUBLANES, 1), 0)
    for vi in range(nchunk):
        @pl.when(need_rank & (vi * SUBLANES <= last_visible))
        def _():
            for i in range(vi * SUBLANES, (vi + 1) * SUBLANES):
                si = score_ref[i:i + 1, :]
                for v in range(nchunk):
                    rows = slice(SUBLANES * v, SUBLANES * (v + 1))
                    chunk = score_ref[rows, :]
                    if v > vi:
                        beats = jnp.where(si >= chunk, 1, 0)
                    elif v < vi:
                        beats = jnp.where(si > chunk, 1, 0)
                    else:
                        beats = jnp.where(sub > i % SUBLANES, jnp.where(si >= chunk, 1, 0),
                                          jnp.where(si > chunk, 1, 0))
                    rank_ref[rows, :] = rank_ref[rows, :] + beats

    @pl.when(need_rank)
    def _():
        sel = valid & (rank_ref[...] < N_SELECT)
        store_rows(jnp.where(sel, 0.0, NEG))


def _select(qt, kc3, vct, mt, b):
    s = qt.shape[1] // b
    g = N_GROUPS
    ncp = kc3.shape[3]
    nb = mt.shape[0]
    gw = HEADS_PER_GROUP * HEAD_DIM
    n_sel = nb // (ATTN_KV // SLC_BLOCK)
    grid = (b, g, s // ATTN_Q)
    return pl.pallas_call(
        _select_kernel,
        grid=grid,
        in_specs=[
            pl.BlockSpec((gw, ATTN_Q), lambda i, j, k: (j, i * (s // ATTN_Q) + k)),
            pl.BlockSpec((1, 1, 1, ncp, 4 * HEAD_DIM), lambda i, j, k: (0, i, j, 0, 0)),
            pl.BlockSpec((1, 1, 1, HEAD_DIM, ncp), lambda i, j, k: (1, i, j, 0, 0)),
            pl.BlockSpec((nb, ncp), lambda i, j, k: (0, 0)),
        ],
        out_specs=[
            pl.BlockSpec((1, 1, 1, gw, ATTN_Q), lambda i, j, k: (i, j, k, 0, 0)),
            pl.BlockSpec((1, 1, 1, n_sel, SEL_ROWS, ATTN_Q), lambda i, j, k: (i, j, k, 0, 0, 0)),
        ],
        out_shape=[
            jax.ShapeDtypeStruct((b, g, s // ATTN_Q, gw, ATTN_Q), F32),
            jax.ShapeDtypeStruct((b, g, s // ATTN_Q, n_sel, SEL_ROWS, ATTN_Q), BF16),
        ],
        scratch_shapes=[pltpu.VMEM((nb, ATTN_Q), F32), pltpu.VMEM((nb, ATTN_Q), jnp.int32)],
        compiler_params=_params("parallel", "parallel", "parallel"),
        name="cmp_select",
    )(qt, kc3, vct, mt)


def _attn_kernel(q_ref, ks_ref, kw_ref, vs_ref, vw_ref, selx_ref, g_ref, oc_ref, o_ref,
                 m_ref, acc_ref, s_a, s_b, smax_ref, pat_ref):
    grp = pl.program_id(1)
    qi = pl.program_id(2)
    tq = q_ref.shape[1]
    kv = ATTN_KV
    nh = HEADS_PER_GROUP
    n = nh * tq
    slc, win = 0, 1

    qt4 = q_ref[...].astype(BF16)
    qt = jnp.concatenate([qt4[r * HEAD_DIM:(r + 1) * HEAD_DIM] for r in range(nh)], axis=1)
    zq = jnp.zeros_like(qt)
    q2 = jnp.concatenate([jnp.where(grp == 0, qt, zq), jnp.where(grp == 1, qt, zq)], axis=0)
    pad = jnp.zeros((K_AUG - D_KV - SEL_ROWS, n), BF16)

    m_ref[...] = jnp.full(m_ref.shape, NEG, F32)
    acc_ref[...] = jnp.zeros(acc_ref.shape, F32)

    @pl.when(qi == 0)
    def _():
        row = lax.broadcasted_iota(jnp.int32, (kv, tq), 0)
        lane = lax.broadcasted_iota(jnp.int32, (kv, tq), 1)
        pat_ref[PAT_ZERO] = jnp.zeros((kv, tq), F32)
        pat_ref[PAT_CAUSAL] = jnp.where(row <= lane, 0.0, NEG)
        pat_ref[PAT_FAR] = jnp.where(lane < row, 0.0, NEG)
        pat_ref[PAT_NONE] = jnp.full((kv, tq), NEG, F32)

    def lanes4(x):
        return jnp.concatenate([x] * nh, axis=1)

    def rows(j):
        return pl.ds(pl.multiple_of(j * kv, kv), kv)

    def store_scores(s, s_ref, slot):
        s_ref[...] = s
        smax_ref[slot] = jnp.max(s, axis=0, keepdims=True)

    def slc_scores(j):
        rhs = jnp.concatenate([q2, lanes4(selx_ref[0, 0, 0, j]), pad], axis=0)
        return _dot(ks_ref[0, rows(j), :], rhs)

    def produce_slc(s_ref, slot, j):
        store_scores(slc_scores(j), s_ref, slot)

    def produce_diag(s_ref, slot):
        store_scores(slc_scores(qi) + lanes4(pat_ref[PAT_CAUSAL]), s_ref, slot)

    def produce_win(s_ref, slot, j, pat):
        s = _dot(kw_ref[0, rows(j), :], q2)
        store_scores(s if pat is None else s + lanes4(pat_ref[pat]), s_ref, slot)

    def consume(s_ref, slot, v_ref, j, kind):
        m_old = m_ref[kind]
        m_new = jnp.maximum(m_old, smax_ref[slot])
        alpha = jnp.exp2(m_old - m_new)
        p = jnp.exp2(s_ref[...] - m_new)
        v_t = v_ref[0, rows(j), :][:, :V_AUG]
        pv = lax.dot_general(v_t, p.astype(BF16), (((0,), (0,)), ((), ())),
                             preferred_element_type=F32)
        acc_ref[kind] = alpha * acc_ref[kind] + pv
        m_ref[kind] = m_new

    def plain(j):
        return (lambda r, sl: produce_slc(r, sl, j)), (lambda r, sl: consume(r, sl, vs_ref, j, slc))

    def window(j, pat):
        return (lambda r, sl: produce_win(r, sl, j, pat)), (lambda r, sl: consume(r, sl, vw_ref, j, win))

    diag = (produce_diag, lambda r, sl: consume(r, sl, vs_ref, qi, slc))

    def run(stages):
        bufs = ((s_a, 0), (s_b, 1))
        for i, (_, consume_i) in enumerate(stages):
            if i + 1 < len(stages):
                stages[i + 1][0](*bufs[(i + 1) % 2])
            consume_i(*bufs[i % 2])

    @pl.when(qi == 0)
    def _():
        produce_diag(s_a, 0)
        run([diag, window(0, PAT_CAUSAL)])

    @pl.when(qi > 0)
    def _():
        produce_slc(s_a, 0, 0)

        def body(u, carry):
            j = ATTN_UNROLL * u
            bufs = ((s_a, 0), (s_b, 1))
            for k in range(ATTN_UNROLL):
                produce_slc(*bufs[(k + 1) % 2], j + k + 1)
                consume(*bufs[k % 2], vs_ref, j + k, slc)
            return carry

        n_loop = (qi - 1) // ATTN_UNROLL
        lax.fori_loop(0, n_loop, body, 0)
        left = qi - ATTN_UNROLL * n_loop

        for k in range(1, ATTN_UNROLL + 1):
            @pl.when(left == k)
            def _():
                far = jnp.where(qi >= 2, PAT_FAR, PAT_NONE) if k == 1 else PAT_FAR
                run([plain(qi - k + i) for i in range(k)]
                    + [diag, window(qi, PAT_CAUSAL), window(qi - 1, None), window(jnp.maximum(qi - 2, 0), far)])

    gt = _sigmoid(g_ref[0]).T
    o_slc = acc_ref[slc, :HEAD_DIM] * (1.0 / acc_ref[slc, HEAD_DIM:HEAD_DIM + 1])
    o_win = acc_ref[win, :HEAD_DIM] * (1.0 / acc_ref[win, HEAD_DIM:HEAD_DIM + 1])
    outs = []
    for r in range(nh):
        cols = slice(r * tq, (r + 1) * tq)
        o_cmp = oc_ref[0, 0, 0, r * HEAD_DIM:(r + 1) * HEAD_DIM, :]
        outs.append(gt[3 * r:3 * r + 1] * o_cmp + gt[3 * r + 1:3 * r + 2] * o_slc[:, cols]
                    + gt[3 * r + 2:3 * r + 3] * o_win[:, cols])
    o_ref[0] = jnp.concatenate(outs, axis=0).T.astype(o_ref.dtype)


def _attention(qt, ks, kw, vall, selx, gates, oc):
    b, s, _ = ks.shape
    g = N_GROUPS
    gw = HEADS_PER_GROUP * HEAD_DIM
    tq = ATTN_Q
    n = HEADS_PER_GROUP * tq
    n_sel = selx.shape[3]
    return pl.pallas_call(
        _attn_kernel,
        grid=(b, g, s // tq),
        in_specs=[
            pl.BlockSpec((gw, tq), lambda i, j, k: (j, i * (s // tq) + k)),
            pl.BlockSpec((1, s, K_AUG), lambda i, j, k: (i, 0, 0)),
            pl.BlockSpec((1, s, D_KV), lambda i, j, k: (i, 0, 0)),
            pl.BlockSpec((1, s, D_KV), lambda i, j, k: (i, 0, j)),
            pl.BlockSpec((1, s, D_KV), lambda i, j, k: (i, 0, N_GROUPS + j)),
            pl.BlockSpec((1, 1, 1, n_sel, SEL_ROWS, tq), lambda i, j, k: (i, j, k, 0, 0, 0)),
            pl.BlockSpec((1, tq, GATE_LANES), lambda i, j, k: (i, k, j)),
            pl.BlockSpec((1, 1, 1, gw, tq), lambda i, j, k: (i, j, k, 0, 0)),
        ],
        out_specs=pl.BlockSpec((1, tq, gw), lambda i, j, k: (i, k, j)),
        out_shape=jax.ShapeDtypeStruct((b, s, D_ATTN), BF16),
        scratch_shapes=[
            pltpu.VMEM((2, 1, n), F32),
            pltpu.VMEM((2, V_AUG, n), F32),
            pltpu.VMEM((ATTN_KV, n), F32),
            pltpu.VMEM((ATTN_KV, n), F32),
            pltpu.VMEM((2, 1, n), F32),
            pltpu.VMEM((4, ATTN_KV, tq), F32),
        ],
        compiler_params=_params("parallel", "parallel", "arbitrary"),
        name="slc_win_attn",
    )(qt, ks, kw, vall, vall, selx, gates, oc)


def _rnn_kernel(xr_ref, gr_ref, cw_ref, cb_ref, wa_ref, wi_ref, ba_ref, bi_ref, lam_ref, o_ref,
                xprev, hcarry):
    si = pl.program_id(1)
    ts = xr_ref.shape[1]
    d = xr_ref.shape[2]

    @pl.when(si == 0)
    def _():
        xprev[...] = jnp.zeros(xprev.shape, F32)
        hcarry[...] = jnp.zeros(hcarry.shape, F32)

    ng = ts // SUBLANES
    x3 = xr_ref[0].reshape(ng, SUBLANES, d)
    sub = lax.broadcasted_iota(jnp.int32, (ng, SUBLANES, d), 1)
    xc = cb_ref[...] + x3 * cw_ref[CONV_WIDTH - 1:CONV_WIDTH, :]
    for back in range(1, CONV_WIDTH):
        rolled = pltpu.roll(x3, back, 1)
        prev = jnp.concatenate([pltpu.roll(xprev[...], back, 0)[None], rolled[:-1]], axis=0)
        tap = CONV_WIDTH - 1 - back
        xc = xc + jnp.where(sub >= back, rolled, prev) * cw_ref[tap:tap + 1, :]
    xprev[...] = x3[ng - 1]
    xc = xc.reshape(ts, d)

    xcb = xc.astype(BF16)
    n_pack = d // LRU_PACK
    ra = jnp.concatenate(
        [_dot(xcb[:, c * LRU_PACK:(c + 1) * LRU_PACK], wa_ref[c]) for c in range(n_pack)], axis=1)
    ri = jnp.concatenate(
        [_dot(xcb[:, c * LRU_PACK:(c + 1) * LRU_PACK], wi_ref[c]) for c in range(n_pack)], axis=1)
    tr1 = jnp.tanh(ra + ba_ref[...]) + 1.0
    ig = 0.5 * jnp.tanh(ri + bi_ref[...]) + 0.5
    neg_log_a = tr1 * (0.5 * LRU_C * jax.nn.softplus(-lam_ref[...]))
    a = jnp.exp2(neg_log_a * (-LOG2_E))
    one_m_a2 = jnp.tanh(neg_log_a) * (a * a + 1.0)
    root = jnp.where(one_m_a2 > 0.0, one_m_a2 * lax.rsqrt(one_m_a2), 0.0)
    bt = root * (ig * xc)

    a3 = a.reshape(ng, SUBLANES, d)
    b3 = bt.reshape(ng, SUBLANES, d)
    step = 1
    while step < SUBLANES:
        keep = sub >= step
        a_prev = jnp.where(keep, pltpu.roll(a3, step, 1), 1.0)
        b_prev = jnp.where(keep, pltpu.roll(b3, step, 1), 0.0)
        b3 = b3 + a3 * b_prev
        a3 = a3 * a_prev
        step *= 2
    h = hcarry[0:1, :]
    hs = []
    for k in range(ng):
        hk = b3[k] + a3[k] * h
        hs.append(hk)
        h = hk[SUBLANES - 1:SUBLANES]
    hcarry[0:1, :] = h
    hfull = jnp.concatenate(hs, axis=0)
    o_ref[0] = (hfull * gr_ref[0]).astype(o_ref.dtype)


def _rnn(xr, gr, cw, cb, wa, wi, ba, bi, lam):
    b, s, d = xr.shape
    ts = RNN_ROWS
    row = lambda a: _resident(a.shape)
    return pl.pallas_call(
        _rnn_kernel,
        grid=(b, s // ts),
        in_specs=[
            pl.BlockSpec((1, ts, d), lambda i, j: (i, j, 0)),
            pl.BlockSpec((1, ts, d), lambda i, j: (i, j, 0)),
            row(cw), row(cb), row(wa), row(wi), row(ba), row(bi), row(lam),
        ],
        out_specs=pl.BlockSpec((1, ts, d), lambda i, j: (i, j, 0)),
        out_shape=jax.ShapeDtypeStruct((b, s, d), BF16),
        scratch_shapes=[pltpu.VMEM((SUBLANES, d), F32), pltpu.VMEM((SUBLANES, d), F32)],
        compiler_params=_params("parallel", "arbitrary"),
        name="rg_lru",
    )(xr, gr, cw, cb, wa, wi, ba, bi, lam)


def _merge_mlp_kernel(x_ref, attn_ref, rnn_ref, ga_ref, gb_ref, wua_ref, wur_ref, wo_ref,
                      n2_ref, w1_ref, w2_ref, fn_ref, o_ref, *, final):
    up_a = _dot(attn_ref[...], wua_ref[0])
    up_r = _dot(rnn_ref[...], wur_ref[0])
    merged = _sigmoid(ga_ref[...]) * up_a + _sigmoid(gb_ref[...]) * up_r
    x = x_ref[...] + _dot(merged.astype(BF16), wo_ref[0])
    var = jnp.mean(x * x, axis=-1, keepdims=True)
    hn = (x * lax.rsqrt(var + EPS) * n2_ref[...]).astype(BF16)
    d_ff = w1_ref.shape[2]
    acc = jnp.zeros_like(x)
    for c in range(d_ff // MLP_FF_CHUNK):
        lo, hi = c * MLP_FF_CHUNK, (c + 1) * MLP_FF_CHUNK
        h1 = jnp.maximum(_dot(hn, w1_ref[0, :, lo:hi]), 0.0)
        acc = acc + _dot((h1 * h1).astype(BF16), w2_ref[0, lo:hi, :])
    x = x + acc
    if final:
        var = jnp.mean(x * x, axis=-1, keepdims=True)
        x = x * lax.rsqrt(var + EPS) * fn_ref[...]
    o_ref[...] = x


def _merge_mlp(x2, attn, rnn, ga, gb, wua, wur, wo, n2, w1, w2, fn, layer, final):
    t, d = x2.shape
    tm = MLP_ROWS
    rows = lambda w: pl.BlockSpec((tm, w), lambda i: (i, 0))
    return pl.pallas_call(
        functools.partial(_merge_mlp_kernel, final=final),
        grid=(t // tm,),
        in_specs=[
            rows(d), rows(attn.shape[1]), rows(rnn.shape[1]), rows(d), rows(d),
            _resident_layer(wua.shape, layer), _resident_layer(wur.shape, layer), _resident_layer(wo.shape, layer),
            _resident(n2.shape), _resident_layer(w1.shape, layer), _resident_layer(w2.shape, layer),
            _resident(fn.shape),
        ],
        out_specs=rows(d),
        out_shape=jax.ShapeDtypeStruct((t, d), F32),
        compiler_params=_params("parallel"),
        name="merge_mlp",
    )(x2, attn, rnn, ga, gb, wua, wur, wo, n2, w1, w2, fn)


def _block_diag_pack(w, pack):
    nblk, k, _ = w.shape
    per = pack // k
    w4 = w.reshape(nblk // per, per, k, k)
    eye = jnp.eye(per, dtype=w.dtype)
    out = jnp.einsum("cpij,pq->cpiqj", w4, eye)
    return out.reshape(nblk // per, pack, pack)


def _in_proj_layout(d):
    dkv = D_KV
    outputs = ((D_ATTN, F32, True), (2 * dkv, F32, False), (N_GROUPS * GATE_LANES, F32, False),
               (K_AUG, BF16, False), (dkv, BF16, False), (4 * dkv, BF16, False),
               (d, F32, False), (d, F32, False), (d, F32, False), (d, F32, False))
    chunks = (
        (0, 1, 0, D_ATTN, (("transposed", 0, D_ATTN, 0, 0),)),
        (0, 1, D_ATTN, 2 * dkv, (("plain", 0, 2 * dkv, 1, 0),)),
        (2, None, 0, N_GROUPS * GATE_LANES, (("plain", 0, N_GROUPS * GATE_LANES, 2, 0),)),
        (0, None, D_ATTN + 2 * dkv, 2 * dkv, (("keys", 0, dkv, 3, 0), ("values", dkv, dkv, 5, 0))),
        (0, None, D_ATTN + 4 * dkv, 2 * dkv, (("plain", 0, dkv, 4, 0), ("values", dkv, dkv, 5, 2 * dkv))),
        (3, None, 0, d, (("plain", 0, d, 6, 0),)),
        (3, None, d, d, (("gelu", 0, d, 7, 0),)),
        (3, None, 2 * d, d, (("plain", 0, d, 8, 0),)),
        (3, None, 3 * d, d, (("plain", 0, d, 9, 0),)),
    )
    return chunks, outputs


def _in_proj_weights(w_in):
    head = D_ATTN + 6 * D_KV
    n_gates = N_HEADS * 3
    n_precise = D_ATTN + 2 * D_KV
    scale = jnp.where(jnp.arange(head) < D_ATTN, ATTN_SCALE * LOG2_E, 1.0).astype(F32)
    wa = w_in[..., :head] * scale
    wa_hi = wa.astype(BF16)
    wa_lo = (wa[..., :n_precise] - wa_hi[..., :n_precise].astype(F32)).astype(BF16)
    g_nsa = w_in[..., head:head + n_gates]
    per_group = HEADS_PER_GROUP * 3
    wg = jnp.concatenate(
        [jnp.pad(g_nsa[..., g * per_group:(g + 1) * per_group], ((0, 0), (0, 0), (0, GATE_LANES - per_group)))
         for g in range(N_GROUPS)], axis=-1).astype(BF16)
    wb = w_in[..., head + n_gates:].astype(BF16)
    return wa_hi, wa_lo, wg, wb


def _compress_weights(w1_k, w1_v, w2_k, w2_v, pos_k, pos_v):
    w1 = jnp.stack([w1_k, w1_v]).reshape(2, CMP_BLOCK, HEAD_DIM, CMP_HIDDEN)
    pos = jnp.stack([pos_k, pos_v])[:, :, None, :]
    pos = jnp.broadcast_to(pos, (2, CMP_BLOCK, SUBLANES, HEAD_DIM))
    w1h, w1l = _split(w1)
    w2h, w2l = _split(jnp.stack([w2_k, w2_v]))
    return pos, w1h, w1l, w2h, w2l


def kernel(x, norm1_w, w_in, cmp_pos_k, cmp_pos_v, cmp_k_w1, cmp_k_w2, cmp_v_w1, cmp_v_w2, conv_w, conv_b, lru_w_a, lru_b_a, lru_w_i, lru_b_i, lru_lambda, w_up_attn, w_up_rnn, w_out, norm2_w, mlp_w1, mlp_w2, final_norm_w):
    b, s, d = x.shape
    depth = w_in.shape[0]
    t = b * s
    ns = s // CMP_STRIDE
    nb = s // SLC_BLOCK
    assert s % ATTN_Q == 0 and s % RNN_ROWS == 0 and t % PROJ_ROWS == 0 and t % MLP_ROWS == 0
    assert ATTN_Q == ATTN_KV and WINDOW == 2 * ATTN_KV and nb >= N_SELECT
    mt = jnp.asarray(_importance_matrix(ns, nb), BF16)
    x2 = x.reshape(t, d)
    fn = final_norm_w.reshape(1, d)
    chunks, proj_outputs = _in_proj_layout(d)
    proj_w = _in_proj_weights(w_in)
    wua, wur, wo, w1, w2 = (w.astype(BF16) for w in (w_up_attn, w_up_rnn, w_out, mlp_w1, mlp_w2))
    for l in range(depth):
        (qt, kv_c, gates, k_s, k_w, vall, xr, gr, g_a, g_b) = _norm_proj(
            x2, norm1_w[l].reshape(1, d), proj_w, l, chunks, proj_outputs)

        kvc, kvct = _compress(
            kv_c.reshape(b, s, 2 * D_KV),
            *_compress_weights(cmp_k_w1[l], cmp_v_w1[l], cmp_k_w2[l], cmp_v_w2[l], cmp_pos_k[l], cmp_pos_v[l]))

        oc, selx = _select(qt, kvc, kvct, mt, b)
        attn = _attention(qt, k_s.reshape(b, s, K_AUG), k_w.reshape(b, s, D_KV),
                          vall.reshape(b, s, 4 * D_KV), selx,
                          gates.reshape(b, s, N_GROUPS * GATE_LANES), oc)

        rnn = _rnn(xr.reshape(b, s, d), gr.reshape(b, s, d),
                   jnp.pad(conv_w[l], ((0, SUBLANES - CONV_WIDTH), (0, 0))), conv_b[l].reshape(1, d),
                   _block_diag_pack(0.5 * lru_w_a[l], LRU_PACK).astype(BF16),
                   _block_diag_pack(0.5 * lru_w_i[l], LRU_PACK).astype(BF16),
                   0.5 * lru_b_a[l].reshape(1, d), 0.5 * lru_b_i[l].reshape(1, d), lru_lambda[l].reshape(1, d))

        x2 = _merge_mlp(x2, attn.reshape(t, D_ATTN), rnn.reshape(t, d), g_a, g_b, wua, wur, wo,
                        norm2_w[l].reshape(1, d), w1, w2, fn, layer=l, final=(l == depth - 1))
    return x2.reshape(b, s, d)
```

```python
import functools

import numpy as np
import jax
import jax.numpy as jnp
from jax import lax
from jax.experimental import pallas as pl
from jax.experimental.pallas import tpu as pltpu

F32 = jnp.float32
BF16 = jnp.bfloat16

N_HEADS = 8
HEAD_DIM = 64
N_GROUPS = 2
HEADS_PER_GROUP = N_HEADS // N_GROUPS
D_ATTN = N_HEADS * HEAD_DIM
D_KV = N_GROUPS * HEAD_DIM
CMP_BLOCK = 32
CMP_STRIDE = 16
CMP_HIDDEN = 256
SLC_BLOCK = 64
N_SELECT = 16
N_LOCAL = 2
WINDOW = 512
ATTN_SCALE = HEAD_DIM ** -0.5
NEG = -1e30
FORCE_SCORE = 1e4
N_RNN_BLOCKS = 16
CONV_WIDTH = 4
LRU_C = 8.0
EPS = 1e-6
LOG2_E = 1.4426950408889634

VMEM_LIMIT_BYTES = 56 * 1024 * 1024
PROJ_ROWS = 512
MLP_ROWS = 512
MLP_FF_CHUNK = 1024
ATTN_Q = 256
ATTN_KV = 256
ATTN_UNROLL = 6
RNN_ROWS = 1024
LRU_PACK = 256
SUBLANES = 8
GATE_LANES = 128
PAT_ZERO, PAT_CAUSAL, PAT_FAR, PAT_NONE = 0, 1, 2, 3
SEL_ROWS = 16
K_AUG = 4 * HEAD_DIM
V_AUG = HEAD_DIM + 16


def _dot(a, b):
    return jnp.dot(a, b, preferred_element_type=F32)


def _dot_nt(a, b):
    return lax.dot_general(a, b, (((1,), (1,)), ((), ())), preferred_element_type=F32)


def _sigmoid(x):
    return 0.5 * jnp.tanh(0.5 * x) + 0.5


def _split(a):
    hi = a.astype(BF16)
    lo = (a - hi.astype(F32)).astype(BF16)
    return hi, lo


def _dot3(ah, al, bh, bl):
    return _dot(ah, bh) + _dot(ah, bl) + _dot(al, bh)


def _params(*sem):
    return pltpu.CompilerParams(dimension_semantics=sem, vmem_limit_bytes=VMEM_LIMIT_BYTES)


def _resident(shape):
    nd = len(shape)
    return pl.BlockSpec(shape, lambda *_: (0,) * nd, pipeline_mode=pl.Buffered(1))


def _resident_layer(stacked_shape, layer):
    nd = len(stacked_shape)
    return pl.BlockSpec((1,) + tuple(stacked_shape[1:]), lambda *_: (layer,) + (0,) * (nd - 1),
                        pipeline_mode=pl.Buffered(1))


def _norm_proj_kernel(x_ref, nw_ref, *refs, chunks, n_weights):
    w_refs, out_refs = refs[:n_weights], refs[n_weights:]
    x = x_ref[...]
    tm = x.shape[0]
    var = jnp.mean(x * x, axis=-1, keepdims=True)
    xn = x * lax.rsqrt(var + EPS) * nw_ref[...]
    xh, xl = _split(xn)
    lane = lax.broadcasted_iota(jnp.int32, (tm, D_KV), 1)
    for w_idx, lo_idx, c_start, c_width, pieces in chunks:
        wh = w_refs[w_idx][0, c_start:c_start + c_width, :]
        prod = _dot_nt(xh, wh)
        if lo_idx is not None:
            prod = prod + _dot_nt(xl, wh) + _dot_nt(xh, w_refs[lo_idx][0, c_start:c_start + c_width, :])
        for kind, off, width, out_idx, out_off in pieces:
            o_ref = out_refs[out_idx]
            acc = prod[:, off:off + width]
            if kind == "plain":
                o_ref[...] = acc.astype(o_ref.dtype)
            elif kind == "transposed":
                o_ref[...] = acc.T.astype(o_ref.dtype)
            elif kind == "gelu":
                o_ref[...] = jax.nn.gelu(acc).astype(o_ref.dtype)
            elif kind == "keys":
                row = lax.broadcasted_iota(jnp.int32, (tm, D_KV), 0)
                onehot = jnp.where(lane == (row // SLC_BLOCK) % (ATTN_KV // SLC_BLOCK), 1.0, 0.0)
                o_ref[:, :width] = acc.astype(o_ref.dtype)
                o_ref[:, width:] = onehot.astype(o_ref.dtype)
            else:
                ones_col = jnp.where(lane == HEAD_DIM, 1.0, 0.0)
                swapped = pltpu.roll(acc, HEAD_DIM, 1)
                o_ref[:, out_off:out_off + D_KV] = jnp.where(lane < HEAD_DIM, acc, ones_col).astype(o_ref.dtype)
                o_ref[:, out_off + D_KV:out_off + 2 * D_KV] = jnp.where(
                    lane < HEAD_DIM, swapped, ones_col).astype(o_ref.dtype)


def _norm_proj(x2, norm_w, weights, layer, chunks, outputs):
    t, d = x2.shape
    assert PROJ_ROWS % ATTN_KV == 0
    grid = (t // PROJ_ROWS,)
    out_shape, out_specs = [], []
    for w, dt, transposed in outputs:
        if transposed:
            out_shape.append(jax.ShapeDtypeStruct((w, t), dt))
            out_specs.append(pl.BlockSpec((w, PROJ_ROWS), lambda i: (0, i)))
        else:
            out_shape.append(jax.ShapeDtypeStruct((t, w), dt))
            out_specs.append(pl.BlockSpec((PROJ_ROWS, w), lambda i: (i, 0)))
    return pl.pallas_call(
        functools.partial(_norm_proj_kernel, chunks=chunks, n_weights=len(weights)),
        grid=grid,
        in_specs=[pl.BlockSpec((PROJ_ROWS, d), lambda i: (i, 0)), _resident(norm_w.shape)]
        + [_resident_layer(w.shape, layer) for w in weights],
        out_specs=out_specs,
        out_shape=out_shape,
        compiler_params=_params("parallel"),
        name="norm_proj",
    )(x2, norm_w, *weights)


def _compress_kernel(x_ref, pos_ref, w1h_ref, w1l_ref, w2h_ref, w2l_ref, c3_ref, ct_ref):
    ns = x_ref.shape[1] // CMP_STRIDE

    def times_w1(rows, q):
        rh, rl = _split(rows)
        wh = w1h_ref[0, q]
        return (_dot(jnp.concatenate([rh, rl], axis=1), jnp.concatenate([wh, wh], axis=0))
                + _dot(rh, w1l_ref[0, q]))

    first = [jnp.zeros((ns, CMP_HIDDEN), F32) for _ in range(N_GROUPS)]
    second = [jnp.zeros((ns, CMP_HIDDEN), F32) for _ in range(N_GROUPS)]
    posb = jnp.zeros((SUBLANES, CMP_HIDDEN), F32)
    for p in range(CMP_STRIDE):
        xp = x_ref[0, pl.ds(p, ns, stride=CMP_STRIDE), :]
        for g in range(N_GROUPS):
            xg = xp[:, g * HEAD_DIM:(g + 1) * HEAD_DIM]
            first[g] = first[g] + times_w1(xg, p)
            second[g] = second[g] + times_w1(xg, p + CMP_STRIDE)
        posb = posb + times_w1(pos_ref[0, p], p) + times_w1(pos_ref[0, p + CMP_STRIDE], p + CMP_STRIDE)
    row = lax.broadcasted_iota(jnp.int32, (ns, HEAD_DIM), 0)
    for g in range(N_GROUPS):
        pre = first[g] + pltpu.roll(second[g], ns - 1, 0) + posb[0:1]
        hh, hl = _split(jax.nn.gelu(pre))
        c = _dot3(hh, hl, w2h_ref[0], w2l_ref[0])
        c = jnp.where(row < ns - 1, c, 0.0)
        ch, cl = _split(c)
        c3_ref[0, 0, g] = jnp.concatenate([ch, ch, cl, jnp.zeros_like(ch)], axis=1)
        wide = jnp.concatenate([c, jnp.zeros_like(c)], axis=1)
        ct_ref[0, 0, g] = wide.T[:HEAD_DIM].astype(ct_ref.dtype)


def _compress(kv_c, pos, w1h, w1l, w2h, w2l):
    b, s, _ = kv_c.shape
    ns = s // CMP_STRIDE
    g = N_GROUPS
    return pl.pallas_call(
        _compress_kernel,
        grid=(2, b),
        in_specs=[
            pl.BlockSpec((1, s, D_KV), lambda k, i: (i, 0, k)),
            pl.BlockSpec((1, CMP_BLOCK, SUBLANES, HEAD_DIM), lambda k, i: (k, 0, 0, 0)),
            pl.BlockSpec((1, CMP_BLOCK, HEAD_DIM, CMP_HIDDEN), lambda k, i: (k, 0, 0, 0)),
            pl.BlockSpec((1, CMP_BLOCK, HEAD_DIM, CMP_HIDDEN), lambda k, i: (k, 0, 0, 0)),
            pl.BlockSpec((1, CMP_HIDDEN, HEAD_DIM), lambda k, i: (k, 0, 0)),
            pl.BlockSpec((1, CMP_HIDDEN, HEAD_DIM), lambda k, i: (k, 0, 0)),
        ],
        out_specs=[
            pl.BlockSpec((1, 1, g, ns, 4 * HEAD_DIM), lambda k, i: (k, i, 0, 0, 0)),
            pl.BlockSpec((1, 1, g, HEAD_DIM, ns), lambda k, i: (k, i, 0, 0, 0)),
        ],
        out_shape=[
            jax.ShapeDtypeStruct((2, b, g, ns, 4 * HEAD_DIM), BF16),
            jax.ShapeDtypeStruct((2, b, g, HEAD_DIM, ns), BF16),
        ],
        compiler_params=_params("parallel", "parallel"),
        name="compress",
    )(kv_c, pos, w1h, w1l, w2h, w2l)


def _importance_matrix(nc_pad, nb):
    r_s = SLC_BLOCK // CMP_STRIDE
    r_c = CMP_BLOCK // CMP_STRIDE
    mat = np.zeros((nb, nc_pad), np.float32)
    for j in range(nb):
        for m in range(r_s):
            for n in range(r_c):
                i = r_s * j - m - n
                if i >= 0:
                    mat[j, i] += 1.0
    return mat


def _select_kernel(q_ref, kc3_ref, vct_ref, mt_ref, oc_ref, selx_ref, score_ref, rank_ref):
    qi = pl.program_id(2)
    tq = q_ref.shape[1]
    ncp = kc3_ref.shape[3]
    nb = mt_ref.shape[0]
    nh = HEADS_PER_GROUP
    bpt = ATTN_KV // SLC_BLOCK
    half_heads = nh // 2

    t1 = qi * tq + lax.broadcasted_iota(jnp.int32, (1, tq), 1)
    ci = lax.broadcasted_iota(jnp.int32, (ncp, 1), 0)
    visible = (ci * CMP_STRIDE + (CMP_BLOCK - 1)) <= t1
    mask = jnp.concatenate([visible] * half_heads, axis=1)

    def scores(h):
        qt = jnp.concatenate([q_ref[r * HEAD_DIM:(r + 1) * HEAD_DIM, :]
                              for r in range(h * half_heads, (h + 1) * half_heads)], axis=1)
        qh, ql = _split(qt)
        rhs = jnp.concatenate([qh, ql, qh, jnp.zeros_like(qh)], axis=0)
        return _dot(kc3_ref[0, 0, 0], rhs)

    def softmax(s):
        s = jnp.where(mask, s, NEG)
        m = jnp.max(s, axis=0, keepdims=True)
        e = jnp.exp2(s - m)
        den = jnp.sum(e, axis=0, keepdims=True)
        inv = jnp.where(m > 0.5 * NEG, 1.0 / den, 0.0)
        return e * inv

    s_halves = [scores(h) for h in range(2)]
    ps = None
    for h in range(2):
        p = softmax(s_halves[h])
        oc = _dot(vct_ref[0, 0, 0], p.astype(BF16))
        for k in range(half_heads):
            r = h * half_heads + k
            pk = p[:, k * tq:(k + 1) * tq]
            ps = pk if ps is None else ps + pk
            oc_ref[0, 0, 0, r * HEAD_DIM:(r + 1) * HEAD_DIM, :] = oc[:, k * tq:(k + 1) * tq]

    blk = lax.broadcasted_iota(jnp.int32, (nb, 1), 0)
    cur = t1 // SLC_BLOCK
    valid = blk <= cur

    def store_rows(selb):
        for jt in range(nb // bpt):
            rows = jnp.concatenate([selb[jt * bpt:(jt + 1) * bpt], jnp.zeros((SEL_ROWS - bpt, tq), F32)], axis=0)
            selx_ref[0, 0, 0, jt] = rows.astype(selx_ref.dtype)

    need_rank = (qi + 1) * tq > N_SELECT * SLC_BLOCK

    @pl.when(jnp.logical_not(need_rank))
    def _():
        store_rows(jnp.where(valid, 0.0, NEG))

    @pl.when(need_rank)
    def _():
        p1 = ps.astype(BF16)
        r1 = ps - p1.astype(F32)
        p2 = r1.astype(BF16)
        p3 = (r1 - p2.astype(F32)).astype(BF16)
        mt = mt_ref[...]
        imp = _dot(mt, p1) + _dot(mt, p2) + _dot(mt, p3)
        forced = (blk == 0) | (blk > cur - N_LOCAL)
        score_ref[...] = jnp.where(valid, jnp.where(forced, FORCE_SCORE, imp), NEG)
        rank_ref[...] = jnp.zeros(rank_ref.shape, jnp.int32)

    nchunk = nb // SUBLANES
    last_visible = ((qi + 1) * tq - 1) // SLC_BLOCK
    sub = lax.broadcasted_iota(jnp.int32, (SUBLANES, 1), 0)
    for vi in range(nchunk):
        @pl.when(need_rank & (vi * SUBLANES <= last_visible))
        def _():
            for i in range(vi * SUBLANES, (vi + 1) * SUBLANES):
                si = score_ref[i:i + 1, :]
                for v in range(nchunk):
                    rows = slice(SUBLANES * v, SUBLANES * (v + 1))
                    chunk = score_ref[rows, :]
                    if v > vi:
                        beats = jnp.where(si >= chunk, 1, 0)
                    elif v < vi:
                        beats = jnp.where(si > chunk, 1, 0)
                    else:
                        beats = jnp.where(sub > i % SUBLANES, jnp.where(si >= chunk, 1, 0),
                                          jnp.where(si > chunk, 1, 0))
                    rank_ref[rows, :] = rank_ref[rows, :] + beats

    @pl.when(need_rank)
    def _():
        sel = valid & (rank_ref[...] < N_SELECT)
        store_rows(jnp.where(sel, 0.0, NEG))


def _select(qt, kc3, vct, mt, b):
    s = qt.shape[1] // b
    g = N_GROUPS
    ncp = kc3.shape[3]
    nb = mt.shape[0]
    gw = HEADS_PER_GROUP * HEAD_DIM
    n_sel = nb // (ATTN_KV // SLC_BLOCK)
    grid = (b, g, s // ATTN_Q)
    return pl.pallas_call(
        _select_kernel,
        grid=grid,
        in_specs=[
            pl.BlockSpec((gw, ATTN_Q), lambda i, j, k: (j, i * (s // ATTN_Q) + k)),
            pl.BlockSpec((1, 1, 1, ncp, 4 * HEAD_DIM), lambda i, j, k: (0, i, j, 0, 0)),
            pl.BlockSpec((1, 1, 1, HEAD_DIM, ncp), lambda i, j, k: (1, i, j, 0, 0)),
            pl.BlockSpec((nb, ncp), lambda i, j, k: (0, 0)),
        ],
        out_specs=[
            pl.BlockSpec((1, 1, 1, gw, ATTN_Q), lambda i, j, k: (i, j, k, 0, 0)),
            pl.BlockSpec((1, 1, 1, n_sel, SEL_ROWS, ATTN_Q), lambda i, j, k: (i, j, k, 0, 0, 0)),
        ],
        out_shape=[
            jax.ShapeDtypeStruct((b, g, s // ATTN_Q, gw, ATTN_Q), F32),
            jax.ShapeDtypeStruct((b, g, s // ATTN_Q, n_sel, SEL_ROWS, ATTN_Q), BF16),
        ],
        scratch_shapes=[pltpu.VMEM((nb, ATTN_Q), F32), pltpu.VMEM((nb, ATTN_Q), jnp.int32)],
        compiler_params=_params("parallel", "parallel", "parallel"),
        name="cmp_select",
    )(qt, kc3, vct, mt)


def _attn_kernel(q_ref, ks_ref, kw_ref, vs_ref, vw_ref, selx_ref, g_ref, oc_ref, o_ref,
                 m_ref, acc_ref, s_a, s_b, smax_ref, pat_ref):
    grp = pl.program_id(1)
    qi = pl.program_id(2)
    tq = q_ref.shape[1]
    kv = ATTN_KV
    nh = HEADS_PER_GROUP
    n = nh * tq
    slc, win = 0, 1

    qt4 = q_ref[...].astype(BF16)
    qt = jnp.concatenate([qt4[r * HEAD_DIM:(r + 1) * HEAD_DIM] for r in range(nh)], axis=1)
    zq = jnp.zeros_like(qt)
    q2 = jnp.concatenate([jnp.where(grp == 0, qt, zq), jnp.where(grp == 1, qt, zq)], axis=0)
    pad = jnp.zeros((K_AUG - D_KV - SEL_ROWS, n), BF16)

    m_ref[...] = jnp.full(m_ref.shape, NEG, F32)
    acc_ref[...] = jnp.zeros(acc_ref.shape, F32)

    @pl.when(qi == 0)
    def _():
        row = lax.broadcasted_iota(jnp.int32, (kv, tq), 0)
        lane = lax.broadcasted_iota(jnp.int32, (kv, tq), 1)
        pat_ref[PAT_ZERO] = jnp.zeros((kv, tq), F32)
        pat_ref[PAT_CAUSAL] = jnp.where(row <= lane, 0.0, NEG)
        pat_ref[PAT_FAR] = jnp.where(lane < row, 0.0, NEG)
        pat_ref[PAT_NONE] = jnp.full((kv, tq), NEG, F32)

    def lanes4(x):
        return jnp.concatenate([x] * nh, axis=1)

    def rows(j):
        return pl.ds(pl.multiple_of(j * kv, kv), kv)

    def store_scores(s, s_ref, slot):
        s_ref[...] = s
        smax_ref[slot] = jnp.max(s, axis=0, keepdims=True)

    def slc_scores(j):
        rhs = jnp.concatenate([q2, lanes4(selx_ref[0, 0, 0, j]), pad], axis=0)
        return _dot(ks_ref[0, rows(j), :], rhs)

    def produce_slc(s_ref, slot, j):
        store_scores(slc_scores(j), s_ref, slot)

    def produce_diag(s_ref, slot):
        store_scores(slc_scores(qi) + lanes4(pat_ref[PAT_CAUSAL]), s_ref, slot)

    def produce_win(s_ref, slot, j, pat):
        s = _dot(kw_ref[0, rows(j), :], q2)
        store_scores(s if pat is None else s + lanes4(pat_ref[pat]), s_ref, slot)

    def consume(s_ref, slot, v_ref, j, kind):
        m_old = m_ref[kind]
        m_new = jnp.maximum(m_old, smax_ref[slot])
        alpha = jnp.exp2(m_old - m_new)
        p = jnp.exp2(s_ref[...] - m_new)
        v_t = v_ref[0, rows(j), :][:, :V_AUG]
        pv = lax.dot_general(v_t, p.astype(BF16), (((0,), (0,)), ((), ())),
                             preferred_element_type=F32)
        acc_ref[kind] = alpha * acc_ref[kind] + pv
        m_ref[kind] = m_new

    def plain(j):
        return (lambda r, sl: produce_slc(r, sl, j)), (lambda r, sl: consume(r, sl, vs_ref, j, slc))

    def window(j, pat):
        return (lambda r, sl: produce_win(r, sl, j, pat)), (lambda r, sl: consume(r, sl, vw_ref, j, win))

    diag = (produce_diag, lambda r, sl: consume(r, sl, vs_ref, qi, slc))

    def run(stages):
        bufs = ((s_a, 0), (s_b, 1))
        for i, (_, consume_i) in enumerate(stages):
            if i + 1 < len(stages):
                stages[i + 1][0](*bufs[(i + 1) % 2])
            consume_i(*bufs[i % 2])

    @pl.when(qi == 0)
    def _():
        produce_diag(s_a, 0)
        run([diag, window(0, PAT_CAUSAL)])

    @pl.when(qi > 0)
    def _():
        produce_slc(s_a, 0, 0)

        def body(u, carry):
            j = ATTN_UNROLL * u
            bufs = ((s_a, 0), (s_b, 1))
            for k in range(ATTN_UNROLL):
                produce_slc(*bufs[(k + 1) % 2], j + k + 1)
                consume(*bufs[k % 2], vs_ref, j + k, slc)
            return carry

        n_loop = (qi - 1) // ATTN_UNROLL
        lax.fori_loop(0, n_loop, body, 0)
        left = qi - ATTN_UNROLL * n_loop

        for k in range(1, ATTN_UNROLL + 1):
            @pl.when(left == k)
            def _():
                far = jnp.where(qi >= 2, PAT_FAR, PAT_NONE) if k == 1 else PAT_FAR
                run([plain(qi - k + i) for i in range(k)]
                    + [diag, window(qi, PAT_CAUSAL), window(qi - 1, None), window(jnp.maximum(qi - 2, 0), far)])

    gt = _sigmoid(g_ref[0]).T
    o_slc = acc_ref[slc, :HEAD_DIM] * (1.0 / acc_ref[slc, HEAD_DIM:HEAD_DIM + 1])
    o_win = acc_ref[win, :HEAD_DIM] * (1.0 / acc_ref[win, HEAD_DIM:HEAD_DIM + 1])
    outs = []
    for r in range(nh):
        cols = slice(r * tq, (r + 1) * tq)
        o_cmp = oc_ref[0, 0, 0, r * HEAD_DIM:(r + 1) * HEAD_DIM, :]
        outs.append(gt[3 * r:3 * r + 1] * o_cmp + gt[3 * r + 1:3 * r + 2] * o_slc[:, cols]
                    + gt[3 * r + 2:3 * r + 3] * o_win[:, cols])
    o_ref[0] = jnp.concatenate(outs, axis=0).T.astype(o_ref.dtype)


def _attention(qt, ks, kw, vall, selx, gates, oc):
    b, s, _ = ks.shape
    g = N_GROUPS
    gw = HEADS_PER_GROUP * HEAD_DIM
    tq = ATTN_Q
    n = HEADS_PER_GROUP * tq
    n_sel = selx.shape[3]
    return pl.pallas_call(
        _attn_kernel,
        grid=(b, g, s // tq),
        in_specs=[
            pl.BlockSpec((gw, tq), lambda i, j, k: (j, i * (s // tq) + k)),
            pl.BlockSpec((1, s, K_AUG), lambda i, j, k: (i, 0, 0)),
            pl.BlockSpec((1, s, D_KV), lambda i, j, k: (i, 0, 0)),
            pl.BlockSpec((1, s, D_KV), lambda i, j, k: (i, 0, j)),
            pl.BlockSpec((1, s, D_KV), lambda i, j, k: (i, 0, N_GROUPS + j)),
            pl.BlockSpec((1, 1, 1, n_sel, SEL_ROWS, tq), lambda i, j, k: (i, j, k, 0, 0, 0)),
            pl.BlockSpec((1, tq, GATE_LANES), lambda i, j, k: (i, k, j)),
            pl.BlockSpec((1, 1, 1, gw, tq), lambda i, j, k: (i, j, k, 0, 0)),
        ],
        out_specs=pl.BlockSpec((1, tq, gw), lambda i, j, k: (i, k, j)),
        out_shape=jax.ShapeDtypeStruct((b, s, D_ATTN), BF16),
        scratch_shapes=[
            pltpu.VMEM((2, 1, n), F32),
            pltpu.VMEM((2, V_AUG, n), F32),
            pltpu.VMEM((ATTN_KV, n), F32),
            pltpu.VMEM((ATTN_KV, n), F32),
            pltpu.VMEM((2, 1, n), F32),
            pltpu.VMEM((4, ATTN_KV, tq), F32),
        ],
        compiler_params=_params("parallel", "parallel", "arbitrary"),
        name="slc_win_attn",
    )(qt, ks, kw, vall, vall, selx, gates, oc)


def _rnn_kernel(xr_ref, gr_ref, cw_ref, cb_ref, wa_ref, wi_ref, ba_ref, bi_ref, lam_ref, o_ref,
                xprev, hcarry):
    si = pl.program_id(1)
    ts = xr_ref.shape[1]
    d = xr_ref.shape[2]

    @pl.when(si == 0)
    def _():
        xprev[...] = jnp.zeros(xprev.shape, F32)
        hcarry[...] = jnp.zeros(hcarry.shape, F32)

    ng = ts // SUBLANES
    x3 = xr_ref[0].reshape(ng, SUBLANES, d)
    sub = lax.broadcasted_iota(jnp.int32, (ng, SUBLANES, d), 1)
    xc = cb_ref[...] + x3 * cw_ref[CONV_WIDTH - 1:CONV_WIDTH, :]
    for back in range(1, CONV_WIDTH):
        rolled = pltpu.roll(x3, back, 1)
        prev = jnp.concatenate([pltpu.roll(xprev[...], back, 0)[None], rolled[:-1]], axis=0)
        tap = CONV_WIDTH - 1 - back
        xc = xc + jnp.where(sub >= back, rolled, prev) * cw_ref[tap:tap + 1, :]
    xprev[...] = x3[ng - 1]
    xc = xc.reshape(ts, d)

    xcb = xc.astype(BF16)
    n_pack = d // LRU_PACK
    ra = jnp.concatenate(
        [_dot(xcb[:, c * LRU_PACK:(c + 1) * LRU_PACK], wa_ref[c]) for c in range(n_pack)], axis=1)
    ri = jnp.concatenate(
        [_dot(xcb[:, c * LRU_PACK:(c + 1) * LRU_PACK], wi_ref[c]) for c in range(n_pack)], axis=1)
    tr1 = jnp.tanh(ra + ba_ref[...]) + 1.0
    ig = 0.5 * jnp.tanh(ri + bi_ref[...]) + 0.5
    neg_log_a = tr1 * (0.5 * LRU_C * jax.nn.softplus(-lam_ref[...]))
    a = jnp.exp2(neg_log_a * (-LOG2_E))
    one_m_a2 = jnp.tanh(neg_log_a) * (a * a + 1.0)
    root = jnp.where(one_m_a2 > 0.0, one_m_a2 * lax.rsqrt(one_m_a2), 0.0)
    bt = root * (ig * xc)

    a3 = a.reshape(ng, SUBLANES, d)
    b3 = bt.reshape(ng, SUBLANES, d)
    step = 1
    while step < SUBLANES:
        keep = sub >= step
        a_prev = jnp.where(keep, pltpu.roll(a3, step, 1), 1.0)
        b_prev = jnp.where(keep, pltpu.roll(b3, step, 1), 0.0)
        b3 = b3 + a3 * b_prev
        a3 = a3 * a_prev
        step *= 2
    h = hcarry[0:1, :]
    hs = []
    for k in range(ng):
        hk = b3[k] + a3[k] * h
        hs.append(hk)
        h = hk[SUBLANES - 1:SUBLANES]
    hcarry[0:1, :] = h
    hfull = jnp.concatenate(hs, axis=0)
    o_ref[0] = (hfull * gr_ref[0]).astype(o_ref.dtype)


def _rnn(xr, gr, cw, cb, wa, wi, ba, bi, lam):
    b, s, d = xr.shape
    ts = RNN_ROWS
    row = lambda a: _resident(a.shape)
    return pl.pallas_call(
        _rnn_kernel,
        grid=(b, s // ts),
        in_specs=[
            pl.BlockSpec((1, ts, d), lambda i, j: (i, j, 0)),
            pl.BlockSpec((1, ts, d), lambda i, j: (i, j, 0)),
            row(cw), row(cb), row(wa), row(wi), row(ba), row(bi), row(lam),
        ],
        out_specs=pl.BlockSpec((1, ts, d), lambda i, j: (i, j, 0)),
        out_shape=jax.ShapeDtypeStruct((b, s, d), BF16),
        scratch_shapes=[pltpu.VMEM((SUBLANES, d), F32), pltpu.VMEM((SUBLANES, d), F32)],
        compiler_params=_params("parallel", "arbitrary"),
        name="rg_lru",
    )(xr, gr, cw, cb, wa, wi, ba, bi, lam)


def _merge_mlp_kernel(x_ref, attn_ref, rnn_ref, ga_ref, gb_ref, wua_ref, wur_ref, wo_ref,
                      n2_ref, w1_ref, w2_ref, fn_ref, o_ref, *, final):
    up_a = _dot(attn_ref[...], wua_ref[0])
    up_r = _dot(rnn_ref[...], wur_ref[0])
    merged = _sigmoid(ga_ref[...]) * up_a + _sigmoid(gb_ref[...]) * up_r
    x = x_ref[...] + _dot(merged.astype(BF16), wo_ref[0])
    var = jnp.mean(x * x, axis=-1, keepdims=True)
    hn = (x * lax.rsqrt(var + EPS) * n2_ref[...]).astype(BF16)
    d_ff = w1_ref.shape[2]
    acc = jnp.zeros_like(x)
    for c in range(d_ff // MLP_FF_CHUNK):
        lo, hi = c * MLP_FF_CHUNK, (c + 1) * MLP_FF_CHUNK
        h1 = jnp.maximum(_dot(hn, w1_ref[0, :, lo:hi]), 0.0)
        acc = acc + _dot((h1 * h1).astype(BF16), w2_ref[0, lo:hi, :])
    x = x + acc
    if final:
        var = jnp.mean(x * x, axis=-1, keepdims=True)
        x = x * lax.rsqrt(var + EPS) * fn_ref[...]
    o_ref[...] = x


def _merge_mlp(x2, attn, rnn, ga, gb, wua, wur, wo, n2, w1, w2, fn, layer, final):
    t, d = x2.shape
    tm = MLP_ROWS
    rows = lambda w: pl.BlockSpec((tm, w), lambda i: (i, 0))
    return pl.pallas_call(
        functools.partial(_merge_mlp_kernel, final=final),
        grid=(t // tm,),
        in_specs=[
            rows(d), rows(attn.shape[1]), rows(rnn.shape[1]), rows(d), rows(d),
            _resident_layer(wua.shape, layer), _resident_layer(wur.shape, layer), _resident_layer(wo.shape, layer),
            _resident(n2.shape), _resident_layer(w1.shape, layer), _resident_layer(w2.shape, layer),
            _resident(fn.shape),
        ],
        out_specs=rows(d),
        out_shape=jax.ShapeDtypeStruct((t, d), F32),
        compiler_params=_params("parallel"),
        name="merge_mlp",
    )(x2, attn, rnn, ga, gb, wua, wur, wo, n2, w1, w2, fn)


def _block_diag_pack(w, pack):
    nblk, k, _ = w.shape
    per = pack // k
    w4 = w.reshape(nblk // per, per, k, k)
    eye = jnp.eye(per, dtype=w.dtype)
    out = jnp.einsum("cpij,pq->cpiqj", w4, eye)
    return out.reshape(nblk // per, pack, pack)


def _in_proj_layout(d):
    dkv = D_KV
    outputs = ((D_ATTN, F32, True), (2 * dkv, F32, False), (N_GROUPS * GATE_LANES, F32, False),
               (K_AUG, BF16, False), (dkv, BF16, False), (4 * dkv, BF16, False),
               (d, F32, False), (d, F32, False), (d, F32, False), (d, F32, False))
    chunks = (
        (0, 1, 0, D_ATTN, (("transposed", 0, D_ATTN, 0, 0),)),
        (0, 1, D_ATTN, 2 * dkv, (("plain", 0, 2 * dkv, 1, 0),)),
        (2, None, 0, N_GROUPS * GATE_LANES, (("plain", 0, N_GROUPS * GATE_LANES, 2, 0),)),
        (0, None, D_ATTN + 2 * dkv, 2 * dkv, (("keys", 0, dkv, 3, 0), ("values", dkv, dkv, 5, 0))),
        (0, None, D_ATTN + 4 * dkv, 2 * dkv, (("plain", 0, dkv, 4, 0), ("values", dkv, dkv, 5, 2 * dkv))),
        (3, None, 0, d, (("plain", 0, d, 6, 0),)),
        (3, None, d, d, (("gelu", 0, d, 7, 0),)),
        (3, None, 2 * d, d, (("plain", 0, d, 8, 0),)),
        (3, None, 3 * d, d, (("plain", 0, d, 9, 0),)),
    )
    return chunks, outputs


def _in_proj_weights(w_in):
    head = D_ATTN + 6 * D_KV
    n_gates = N_HEADS * 3
    n_precise = D_ATTN + 2 * D_KV
    wt = jnp.swapaxes(w_in, 1, 2)
    scale = jnp.where(jnp.arange(head) < D_ATTN, ATTN_SCALE * LOG2_E, 1.0).astype(F32)[:, None]
    wa = wt[:, :head] * scale
    wa_hi = wa.astype(BF16)
    wa_lo = (wa[:, :n_precise] - wa_hi[:, :n_precise].astype(F32)).astype(BF16)
    g_nsa = wt[:, head:head + n_gates]
    per_group = HEADS_PER_GROUP * 3
    wg = jnp.concatenate(
        [jnp.pad(g_nsa[:, g * per_group:(g + 1) * per_group], ((0, 0), (0, GATE_LANES - per_group), (0, 0)))
         for g in range(N_GROUPS)], axis=1).astype(BF16)
    wb = wt[:, head + n_gates:].astype(BF16)
    return wa_hi, wa_lo, wg, wb


def _compress_weights(w1_k, w1_v, w2_k, w2_v, pos_k, pos_v):
    w1 = jnp.stack([w1_k, w1_v]).reshape(2, CMP_BLOCK, HEAD_DIM, CMP_HIDDEN)
    pos = jnp.stack([pos_k, pos_v])[:, :, None, :]
    pos = jnp.broadcast_to(pos, (2, CMP_BLOCK, SUBLANES, HEAD_DIM))
    w1h, w1l = _split(w1)
    w2h, w2l = _split(jnp.stack([w2_k, w2_v]))
    return pos, w1h, w1l, w2h, w2l


def kernel(x, norm1_w, w_in, cmp_pos_k, cmp_pos_v, cmp_k_w1, cmp_k_w2, cmp_v_w1, cmp_v_w2, conv_w, conv_b, lru_w_a, lru_b_a, lru_w_i, lru_b_i, lru_lambda, w_up_attn, w_up_rnn, w_out, norm2_w, mlp_w1, mlp_w2, final_norm_w):
    b, s, d = x.shape
    depth = w_in.shape[0]
    t = b * s
    ns = s // CMP_STRIDE
    nb = s // SLC_BLOCK
    assert s % ATTN_Q == 0 and s % RNN_ROWS == 0 and t % PROJ_ROWS == 0 and t % MLP_ROWS == 0
    assert ATTN_Q == ATTN_KV and WINDOW == 2 * ATTN_KV and nb >= N_SELECT
    mt = jnp.asarray(_importance_matrix(ns, nb), BF16)
    x2 = x.reshape(t, d)
    fn = final_norm_w.reshape(1, d)
    chunks, proj_outputs = _in_proj_layout(d)
    proj_w = _in_proj_weights(w_in)
    wua, wur, wo, w1, w2 = (w.astype(BF16) for w in (w_up_attn, w_up_rnn, w_out, mlp_w1, mlp_w2))
    for l in range(depth):
        (qt, kv_c, gates, k_s, k_w, vall, xr, gr, g_a, g_b) = _norm_proj(
            x2, norm1_w[l].reshape(1, d), proj_w, l, chunks, proj_outputs)

        kvc, kvct = _compress(
            kv_c.reshape(b, s, 2 * D_KV),
            *_compress_weights(cmp_k_w1[l], cmp_v_w1[l], cmp_k_w2[l], cmp_v_w2[l], cmp_pos_k[l], cmp_pos_v[l]))

        oc, selx = _select(qt, kvc, kvct, mt, b)
        attn = _attention(qt, k_s.reshape(b, s, K_AUG), k_w.reshape(b, s, D_KV),
                          vall.reshape(b, s, 4 * D_KV), selx,
                          gates.reshape(b, s, N_GROUPS * GATE_LANES), oc)

        rnn = _rnn(xr.reshape(b, s, d), gr.reshape(b, s, d),
                   jnp.pad(conv_w[l], ((0, SUBLANES - CONV_WIDTH), (0, 0))), conv_b[l].reshape(1, d),
                   _block_diag_pack(0.5 * lru_w_a[l], LRU_PACK).astype(BF16),
                   _block_diag_pack(0.5 * lru_w_i[l], LRU_PACK).astype(BF16),
                   0.5 * lru_b_a[l].reshape(1, d), 0.5 * lru_b_i[l].reshape(1, d), lru_lambda[l].reshape(1, d))

        x2 = _merge_mlp(x2, attn.reshape(t, D_ATTN), rnn.reshape(t, d), g_a, g_b, wua, wur, wo,
                        norm2_w[l].reshape(1, d), w1, w2, fn, layer=l, final=(l == depth - 1))
    return x2.reshape(b, s, d)
```

```python
import functools

import numpy as np
import jax
import jax.numpy as jnp
from jax import lax
from jax.experimental import pallas as pl
from jax.experimental.pallas import tpu as pltpu

F32 = jnp.float32
BF16 = jnp.bfloat16

N_HEADS = 8
HEAD_DIM = 64
N_GROUPS = 2
HEADS_PER_GROUP = N_HEADS // N_GROUPS
D_ATTN = N_HEADS * HEAD_DIM
D_KV = N_GROUPS * HEAD_DIM
CMP_BLOCK = 32
CMP_STRIDE = 16
CMP_HIDDEN = 256
SLC_BLOCK = 64
N_SELECT = 16
N_LOCAL = 2
WINDOW = 512
ATTN_SCALE = HEAD_DIM ** -0.5
NEG = -1e30
FORCE_SCORE = 1e4
N_RNN_BLOCKS = 16
CONV_WIDTH = 4
LRU_C = 8.0
EPS = 1e-6
LOG2_E = 1.4426950408889634

VMEM_LIMIT_BYTES = 56 * 1024 * 1024
PROJ_ROWS = 512
MLP_ROWS = 512
MLP_FF_CHUNK = 1024
ATTN_Q = 256
ATTN_KV = 256
ATTN_UNROLL = 6
RNN_ROWS = 1024
LRU_PACK = 256
SUBLANES = 8
GATE_LANES = 128
PAT_ZERO, PAT_CAUSAL, PAT_FAR, PAT_NONE = 0, 1, 2, 3
SEL_ROWS = 16
K_AUG = 4 * HEAD_DIM
V_AUG = HEAD_DIM + 16


def _dot(a, b):
    return jnp.dot(a, b, preferred_element_type=F32)


def _dot_nt(a, b):
    return lax.dot_general(a, b, (((1,), (1,)), ((), ())), preferred_element_type=F32)


def _sigmoid(x):
    return 0.5 * jnp.tanh(0.5 * x) + 0.5


def _split(a):
    hi = a.astype(BF16)
    lo = (a - hi.astype(F32)).astype(BF16)
    return hi, lo


def _dot3(ah, al, bh, bl):
    return _dot(ah, bh) + _dot(ah, bl) + _dot(al, bh)


def _params(*sem):
    return pltpu.CompilerParams(dimension_semantics=sem, vmem_limit_bytes=VMEM_LIMIT_BYTES)


def _resident(shape):
    nd = len(shape)
    return pl.BlockSpec(shape, lambda *_: (0,) * nd, pipeline_mode=pl.Buffered(1))


def _resident_layer(stacked_shape, layer):
    nd = len(stacked_shape)
    return pl.BlockSpec((1,) + tuple(stacked_shape[1:]), lambda *_: (layer,) + (0,) * (nd - 1),
                        pipeline_mode=pl.Buffered(1))


def _norm_proj_kernel(x_ref, nw_ref, *refs, chunks, n_weights):
    w_refs, out_refs = refs[:n_weights], refs[n_weights:]
    x = x_ref[...]
    tm = x.shape[0]
    var = jnp.mean(x * x, axis=-1, keepdims=True)
    xn = x * lax.rsqrt(var + EPS) * nw_ref[...]
    xh, xl = _split(xn)
    lane = lax.broadcasted_iota(jnp.int32, (tm, D_KV), 1)
    for w_idx, lo_idx, c_start, c_width, pieces in chunks:
        wh = w_refs[w_idx][0, c_start:c_start + c_width, :]
        prod = _dot_nt(xh, wh)
        if lo_idx is not None:
            prod = prod + _dot_nt(xl, wh) + _dot_nt(xh, w_refs[lo_idx][0, c_start:c_start + c_width, :])
        for kind, off, width, out_idx, out_off in pieces:
            o_ref = out_refs[out_idx]
            acc = prod[:, off:off + width]
            if kind == "plain":
                o_ref[...] = acc.astype(o_ref.dtype)
            elif kind == "transposed":
                o_ref[...] = acc.T.astype(o_ref.dtype)
            elif kind == "gelu":
                o_ref[...] = jax.nn.gelu(acc).astype(o_ref.dtype)
            elif kind == "keys":
                row = lax.broadcasted_iota(jnp.int32, (tm, D_KV), 0)
                onehot = jnp.where(lane == (row // SLC_BLOCK) % (ATTN_KV // SLC_BLOCK), 1.0, 0.0)
                o_ref[:, :width] = acc.astype(o_ref.dtype)
                o_ref[:, width:] = onehot.astype(o_ref.dtype)
            else:
                ones_col = jnp.where(lane == HEAD_DIM, 1.0, 0.0)
                swapped = pltpu.roll(acc, HEAD_DIM, 1)
                o_ref[:, out_off:out_off + D_KV] = jnp.where(lane < HEAD_DIM, acc, ones_col).astype(o_ref.dtype)
                o_ref[:, out_off + D_KV:out_off + 2 * D_KV] = jnp.where(
                    lane < HEAD_DIM, swapped, ones_col).astype(o_ref.dtype)


def _norm_proj(x2, norm_w, weights, layer, chunks, outputs):
    t, d = x2.shape
    assert PROJ_ROWS % ATTN_KV == 0
    grid = (t // PROJ_ROWS,)
    out_shape, out_specs = [], []
    for w, dt, transposed in outputs:
        if transposed:
            out_shape.append(jax.ShapeDtypeStruct((w, t), dt))
            out_specs.append(pl.BlockSpec((w, PROJ_ROWS), lambda i: (0, i)))
        else:
            out_shape.append(jax.ShapeDtypeStruct((t, w), dt))
            out_specs.append(pl.BlockSpec((PROJ_ROWS, w), lambda i: (i, 0)))
    return pl.pallas_call(
        functools.partial(_norm_proj_kernel, chunks=chunks, n_weights=len(weights)),
        grid=grid,
        in_specs=[pl.BlockSpec((PROJ_ROWS, d), lambda i: (i, 0)), _resident(norm_w.shape)]
        + [_resident_layer(w.shape, layer) for w in weights],
        out_specs=out_specs,
        out_shape=out_shape,
        compiler_params=_params("parallel"),
        name="norm_proj",
    )(x2, norm_w, *weights)


def _compress_kernel(x_ref, pos_ref, w1h_ref, w1l_ref, w2h_ref, w2l_ref, o_ref, *, precise):
    ns = x_ref.shape[1] // CMP_STRIDE

    def times_w1(rows, q):
        wh = w1h_ref[0, q]
        if not precise:
            return _dot(rows.astype(BF16), wh)
        rh, rl = _split(rows)
        return (_dot(jnp.concatenate([rh, rl], axis=1), jnp.concatenate([wh, wh], axis=0))
                + _dot(rh, w1l_ref[0, q]))

    first = [jnp.zeros((ns, CMP_HIDDEN), F32) for _ in range(N_GROUPS)]
    second = [jnp.zeros((ns, CMP_HIDDEN), F32) for _ in range(N_GROUPS)]
    posb = jnp.zeros((SUBLANES, CMP_HIDDEN), F32)
    for p in range(CMP_STRIDE):
        xp = x_ref[0, pl.ds(p, ns, stride=CMP_STRIDE), :]
        for g in range(N_GROUPS):
            xg = xp[:, g * HEAD_DIM:(g + 1) * HEAD_DIM]
            first[g] = first[g] + times_w1(xg, p)
            second[g] = second[g] + times_w1(xg, p + CMP_STRIDE)
        posb = posb + times_w1(pos_ref[0, p], p) + times_w1(pos_ref[0, p + CMP_STRIDE], p + CMP_STRIDE)
    row = lax.broadcasted_iota(jnp.int32, (ns, HEAD_DIM), 0)
    for g in range(N_GROUPS):
        hid = jax.nn.gelu(first[g] + pltpu.roll(second[g], ns - 1, 0) + posb[0:1])
        if precise:
            hh, hl = _split(hid)
            c = _dot3(hh, hl, w2h_ref[0], w2l_ref[0])
        else:
            c = _dot(hid.astype(BF16), w2h_ref[0])
        c = jnp.where(row < ns - 1, c, 0.0)
        if precise:
            ch, cl = _split(c)
            o_ref[0, g] = jnp.concatenate([ch, ch, cl, jnp.zeros_like(ch)], axis=1)
        else:
            wide = jnp.concatenate([c, jnp.zeros_like(c)], axis=1)
            o_ref[0, g] = wide.T[:HEAD_DIM].astype(o_ref.dtype)


def _compress(kv_c, pos, w1h, w1l, w2h, w2l, which):
    b, s, _ = kv_c.shape
    ns = s // CMP_STRIDE
    g = N_GROUPS
    precise = which == 0
    out_tail = (ns, 4 * HEAD_DIM) if precise else (HEAD_DIM, ns)
    return pl.pallas_call(
        functools.partial(_compress_kernel, precise=precise),
        grid=(b,),
        in_specs=[
            pl.BlockSpec((1, s, D_KV), lambda i: (i, 0, which)),
            pl.BlockSpec((1, CMP_BLOCK, SUBLANES, HEAD_DIM), lambda i: (which, 0, 0, 0)),
            pl.BlockSpec((1, CMP_BLOCK, HEAD_DIM, CMP_HIDDEN), lambda i: (which, 0, 0, 0)),
            pl.BlockSpec((1, CMP_BLOCK, HEAD_DIM, CMP_HIDDEN), lambda i: (which, 0, 0, 0)),
            pl.BlockSpec((1, CMP_HIDDEN, HEAD_DIM), lambda i: (which, 0, 0)),
            pl.BlockSpec((1, CMP_HIDDEN, HEAD_DIM), lambda i: (which, 0, 0)),
        ],
        out_specs=pl.BlockSpec((1, g) + out_tail, lambda i: (i, 0, 0, 0)),
        out_shape=jax.ShapeDtypeStruct((b, g) + out_tail, BF16),
        compiler_params=_params("parallel"),
        name="compress",
    )(kv_c, pos, w1h, w1l, w2h, w2l)


def _importance_matrix(nc_pad, nb):
    r_s = SLC_BLOCK // CMP_STRIDE
    r_c = CMP_BLOCK // CMP_STRIDE
    mat = np.zeros((nb, nc_pad), np.float32)
    for j in range(nb):
        for m in range(r_s):
            for n in range(r_c):
                i = r_s * j - m - n
                if i >= 0:
                    mat[j, i] += 1.0
    return mat


def _select_kernel(q_ref, kc3_ref, vct_ref, mt_ref, oc_ref, selx_ref, score_ref, rank_ref):
    qi = pl.program_id(2)
    tq = q_ref.shape[1]
    ncp = kc3_ref.shape[2]
    nb = mt_ref.shape[0]
    nh = HEADS_PER_GROUP
    bpt = ATTN_KV // SLC_BLOCK
    half_heads = nh // 2

    t1 = qi * tq + lax.broadcasted_iota(jnp.int32, (1, tq), 1)
    ci = lax.broadcasted_iota(jnp.int32, (ncp, 1), 0)
    visible = (ci * CMP_STRIDE + (CMP_BLOCK - 1)) <= t1
    mask = jnp.concatenate([visible] * half_heads, axis=1)

    def scores(h):
        qt = jnp.concatenate([q_ref[r * HEAD_DIM:(r + 1) * HEAD_DIM, :]
                              for r in range(h * half_heads, (h + 1) * half_heads)], axis=1)
        qh, ql = _split(qt)
        rhs = jnp.concatenate([qh, ql, qh, jnp.zeros_like(qh)], axis=0)
        return _dot(kc3_ref[0, 0], rhs)

    def softmax(s):
        s = jnp.where(mask, s, NEG)
        m = jnp.max(s, axis=0, keepdims=True)
        e = jnp.exp2(s - m)
        den = jnp.sum(e, axis=0, keepdims=True)
        inv = jnp.where(m > 0.5 * NEG, 1.0 / den, 0.0)
        return e * inv

    s_halves = [scores(h) for h in range(2)]
    ps = None
    for h in range(2):
        p = softmax(s_halves[h])
        oc = _dot(vct_ref[0, 0], p.astype(BF16))
        for k in range(half_heads):
            r = h * half_heads + k
            pk = p[:, k * tq:(k + 1) * tq]
            ps = pk if ps is None else ps + pk
            oc_ref[0, 0, 0, r * HEAD_DIM:(r + 1) * HEAD_DIM, :] = oc[:, k * tq:(k + 1) * tq]

    blk = lax.broadcasted_iota(jnp.int32, (nb, 1), 0)
    cur = t1 // SLC_BLOCK
    valid = blk <= cur

    def store_rows(selb):
        for jt in range(nb // bpt):
            rows = jnp.concatenate([selb[jt * bpt:(jt + 1) * bpt], jnp.zeros((SEL_ROWS - bpt, tq), F32)], axis=0)
            selx_ref[0, 0, 0, jt] = rows.astype(selx_ref.dtype)

    need_rank = (qi + 1) * tq > N_SELECT * SLC_BLOCK

    @pl.when(jnp.logical_not(need_rank))
    def _():
        store_rows(jnp.where(valid, 0.0, NEG))

    @pl.when(need_rank)
    def _():
        p1 = ps.astype(BF16)
        r1 = ps - p1.astype(F32)
        p2 = r1.astype(BF16)
        p3 = (r1 - p2.astype(F32)).astype(BF16)
        mt = mt_ref[...]
        imp = _dot(mt, p1) + _dot(mt, p2) + _dot(mt, p3)
        forced = (blk == 0) | (blk > cur - N_LOCAL)
        score_ref[...] = jnp.where(valid, jnp.where(forced, FORCE_SCORE, imp), NEG)
        rank_ref[...] = jnp.zeros(rank_ref.shape, jnp.int32)

    nchunk = nb // SUBLANES
    last_visible = ((qi + 1) * tq - 1) // SLC_BLOCK
    sub = lax.broadcasted_iota(jnp.int32, (SUBLANES, 1), 0)
    for vi in range(nchunk):
        @pl.when(need_rank & (vi * SUBLANES <= last_visible))
        def _():
            for i in range(vi * SUBLANES, (vi + 1) * SUBLANES):
                si = score_ref[i:i + 1, :]
                for v in range(nchunk):
                    rows = slice(SUBLANES * v, SUBLANES * (v + 1))
                    chunk = score_ref[rows, :]
                    if v > vi:
                        beats = jnp.where(si >= chunk, 1, 0)
                    elif v < vi:
                        beats = jnp.where(si > chunk, 1, 0)
                    else:
                        beats = jnp.where(sub > i % SUBLANES, jnp.where(si >= chunk, 1, 0),
                                          jnp.where(si > chunk, 1, 0))
                    rank_ref[rows, :] = rank_ref[rows, :] + beats

    @pl.when(need_rank)
    def _():
        sel = valid & (rank_ref[...] < N_SELECT)
        store_rows(jnp.where(sel, 0.0, NEG))


def _select(qt, kc3, vct, mt, b):
    s = qt.shape[1] // b
    g = N_GROUPS
    ncp = kc3.shape[2]
    nb = mt.shape[0]
    gw = HEADS_PER_GROUP * HEAD_DIM
    n_sel = nb // (ATTN_KV // SLC_BLOCK)
    grid = (b, g, s // ATTN_Q)
    return pl.pallas_call(
        _select_kernel,
        grid=grid,
        in_specs=[
            pl.BlockSpec((gw, ATTN_Q), lambda i, j, k: (j, i * (s // ATTN_Q) + k)),
            pl.BlockSpec((1, 1, ncp, 4 * HEAD_DIM), lambda i, j, k: (i, j, 0, 0)),
            pl.BlockSpec((1, 1, HEAD_DIM, ncp), lambda i, j, k: (i, j, 0, 0)),
            pl.BlockSpec((nb, ncp), lambda i, j, k: (0, 0)),
        ],
        out_specs=[
            pl.BlockSpec((1, 1, 1, gw, ATTN_Q), lambda i, j, k: (i, j, k, 0, 0)),
            pl.BlockSpec((1, 1, 1, n_sel, SEL_ROWS, ATTN_Q), lambda i, j, k: (i, j, k, 0, 0, 0)),
        ],
        out_shape=[
            jax.ShapeDtypeStruct((b, g, s // ATTN_Q, gw, ATTN_Q), F32),
            jax.ShapeDtypeStruct((b, g, s // ATTN_Q, n_sel, SEL_ROWS, ATTN_Q), BF16),
        ],
        scratch_shapes=[pltpu.VMEM((nb, ATTN_Q), F32), pltpu.VMEM((nb, ATTN_Q), jnp.int32)],
        compiler_params=_params("parallel", "parallel", "parallel"),
        name="cmp_select",
    )(qt, kc3, vct, mt)


def _attn_kernel(q_ref, ks_ref, kw_ref, vs_ref, vw_ref, selx_ref, g_ref, oc_ref, o_ref,
                 m_ref, acc_ref, s_a, s_b, smax_ref, pat_ref):
    grp = pl.program_id(1)
    qi = pl.program_id(2)
    tq = q_ref.shape[1]
    kv = ATTN_KV
    nh = HEADS_PER_GROUP
    n = nh * tq
    slc, win = 0, 1

    qt4 = q_ref[...].astype(BF16)
    qt = jnp.concatenate([qt4[r * HEAD_DIM:(r + 1) * HEAD_DIM] for r in range(nh)], axis=1)
    zq = jnp.zeros_like(qt)
    q2 = jnp.concatenate([jnp.where(grp == 0, qt, zq), jnp.where(grp == 1, qt, zq)], axis=0)
    pad = jnp.zeros((K_AUG - D_KV - SEL_ROWS, n), BF16)

    m_ref[...] = jnp.full(m_ref.shape, NEG, F32)
    acc_ref[...] = jnp.zeros(acc_ref.shape, F32)

    @pl.when(qi == 0)
    def _():
        row = lax.broadcasted_iota(jnp.int32, (kv, tq), 0)
        lane = lax.broadcasted_iota(jnp.int32, (kv, tq), 1)
        pat_ref[PAT_ZERO] = jnp.zeros((kv, tq), F32)
        pat_ref[PAT_CAUSAL] = jnp.where(row <= lane, 0.0, NEG)
        pat_ref[PAT_FAR] = jnp.where(lane < row, 0.0, NEG)
        pat_ref[PAT_NONE] = jnp.full((kv, tq), NEG, F32)

    def lanes4(x):
        return jnp.concatenate([x] * nh, axis=1)

    def rows(j):
        return pl.ds(pl.multiple_of(j * kv, kv), kv)

    def store_scores(s, s_ref, slot):
        s_ref[...] = s
        smax_ref[slot] = jnp.max(s, axis=0, keepdims=True)

    def slc_scores(j):
        rhs = jnp.concatenate([q2, lanes4(selx_ref[0, 0, 0, j]), pad], axis=0)
        return _dot(ks_ref[0, rows(j), :], rhs)

    def produce_slc(s_ref, slot, j):
        store_scores(slc_scores(j), s_ref, slot)

    def produce_diag(s_ref, slot):
        store_scores(slc_scores(qi) + lanes4(pat_ref[PAT_CAUSAL]), s_ref, slot)

    def produce_win(s_ref, slot, j, pat):
        s = _dot(kw_ref[0, rows(j), :], q2)
        store_scores(s if pat is None else s + lanes4(pat_ref[pat]), s_ref, slot)

    def consume(s_ref, slot, v_ref, j, kind):
        m_old = m_ref[kind]
        m_new = jnp.maximum(m_old, smax_ref[slot])
        alpha = jnp.exp2(m_old - m_new)
        p = jnp.exp2(s_ref[...] - m_new)
        v_t = v_ref[0, rows(j), :][:, :V_AUG]
        pv = lax.dot_general(v_t, p.astype(BF16), (((0,), (0,)), ((), ())),
                             preferred_element_type=F32)
        acc_ref[kind] = alpha * acc_ref[kind] + pv
        m_ref[kind] = m_new

    def plain(j):
        return (lambda r, sl: produce_slc(r, sl, j)), (lambda r, sl: consume(r, sl, vs_ref, j, slc))

    def window(j, pat):
        return (lambda r, sl: produce_win(r, sl, j, pat)), (lambda r, sl: consume(r, sl, vw_ref, j, win))

    diag = (produce_diag, lambda r, sl: consume(r, sl, vs_ref, qi, slc))

    def run(stages):
        bufs = ((s_a, 0), (s_b, 1))
        for i, (_, consume_i) in enumerate(stages):
            if i + 1 < len(stages):
                stages[i + 1][0](*bufs[(i + 1) % 2])
            consume_i(*bufs[i % 2])

    @pl.when(qi == 0)
    def _():
        produce_diag(s_a, 0)
        run([diag, window(0, PAT_CAUSAL)])

    @pl.when(qi > 0)
    def _():
        produce_slc(s_a, 0, 0)

        def body(u, carry):
            j = ATTN_UNROLL * u
            bufs = ((s_a, 0), (s_b, 1))
            for k in range(ATTN_UNROLL):
                produce_slc(*bufs[(k + 1) % 2], j + k + 1)
                consume(*bufs[k % 2], vs_ref, j + k, slc)
            return carry

        n_loop = (qi - 1) // ATTN_UNROLL
        lax.fori_loop(0, n_loop, body, 0)
        left = qi - ATTN_UNROLL * n_loop

        for k in range(1, ATTN_UNROLL + 1):
            @pl.when(left == k)
            def _():
                far = jnp.where(qi >= 2, PAT_FAR, PAT_NONE) if k == 1 else PAT_FAR
                run([plain(qi - k + i) for i in range(k)]
                    + [diag, window(qi, PAT_CAUSAL), window(qi - 1, None), window(jnp.maximum(qi - 2, 0), far)])

    gt = _sigmoid(g_ref[0]).T
    o_slc = acc_ref[slc, :HEAD_DIM] * (1.0 / acc_ref[slc, HEAD_DIM:HEAD_DIM + 1])
    o_win = acc_ref[win, :HEAD_DIM] * (1.0 / acc_ref[win, HEAD_DIM:HEAD_DIM + 1])
    outs = []
    for r in range(nh):
        cols = slice(r * tq, (r + 1) * tq)
        o_cmp = oc_ref[0, 0, 0, r * HEAD_DIM:(r + 1) * HEAD_DIM, :]
        outs.append(gt[3 * r:3 * r + 1] * o_cmp + gt[3 * r + 1:3 * r + 2] * o_slc[:, cols]
                    + gt[3 * r + 2:3 * r + 3] * o_win[:, cols])
    o_ref[0] = jnp.concatenate(outs, axis=0).T.astype(o_ref.dtype)


def _attention(qt, ks, kw, vall, selx, gates, oc):
    b, s, _ = ks.shape
    g = N_GROUPS
    gw = HEADS_PER_GROUP * HEAD_DIM
    tq = ATTN_Q
    n = HEADS_PER_GROUP * tq
    n_sel = selx.shape[3]
    return pl.pallas_call(
        _attn_kernel,
        grid=(b, g, s // tq),
        in_specs=[
            pl.BlockSpec((gw, tq), lambda i, j, k: (j, i * (s // tq) + k)),
            pl.BlockSpec((1, s, K_AUG), lambda i, j, k: (i, 0, 0)),
            pl.BlockSpec((1, s, D_KV), lambda i, j, k: (i, 0, 0)),
            pl.BlockSpec((1, s, D_KV), lambda i, j, k: (i, 0, j)),
            pl.BlockSpec((1, s, D_KV), lambda i, j, k: (i, 0, N_GROUPS + j)),
            pl.BlockSpec((1, 1, 1, n_sel, SEL_ROWS, tq), lambda i, j, k: (i, j, k, 0, 0, 0)),
            pl.BlockSpec((1, tq, GATE_LANES), lambda i, j, k: (i, k, j)),
            pl.BlockSpec((1, 1, 1, gw, tq), lambda i, j, k: (i, j, k, 0, 0)),
        ],
        out_specs=pl.BlockSpec((1, tq, gw), lambda i, j, k: (i, k, j)),
        out_shape=jax.ShapeDtypeStruct((b, s, D_ATTN), BF16),
        scratch_shapes=[
            pltpu.VMEM((2, 1, n), F32),
            pltpu.VMEM((2, V_AUG, n), F32),
            pltpu.VMEM((ATTN_KV, n), F32),
            pltpu.VMEM((ATTN_KV, n), F32),
            pltpu.VMEM((2, 1, n), F32),
            pltpu.VMEM((4, ATTN_KV, tq), F32),
        ],
        compiler_params=_params("parallel", "parallel", "arbitrary"),
        name="slc_win_attn",
    )(qt, ks, kw, vall, vall, selx, gates, oc)


def _rnn_kernel(xr_ref, gr_ref, cw_ref, cb_ref, wa_ref, wi_ref, ba_ref, bi_ref, lam_ref, o_ref,
                xprev, hcarry):
    si = pl.program_id(1)
    ts = xr_ref.shape[1]
    d = xr_ref.shape[2]

    @pl.when(si == 0)
    def _():
        xprev[...] = jnp.zeros(xprev.shape, F32)
        hcarry[...] = jnp.zeros(hcarry.shape, F32)

    ng = ts // SUBLANES
    x3 = xr_ref[0].reshape(ng, SUBLANES, d)
    sub = lax.broadcasted_iota(jnp.int32, (ng, SUBLANES, d), 1)
    xc = cb_ref[...] + x3 * cw_ref[CONV_WIDTH - 1:CONV_WIDTH, :]
    for back in range(1, CONV_WIDTH):
        rolled = pltpu.roll(x3, back, 1)
        prev = jnp.concatenate([pltpu.roll(xprev[...], back, 0)[None], rolled[:-1]], axis=0)
        tap = CONV_WIDTH - 1 - back
        xc = xc + jnp.where(sub >= back, rolled, prev) * cw_ref[tap:tap + 1, :]
    xprev[...] = x3[ng - 1]
    xc = xc.reshape(ts, d)

    xcb = xc.astype(BF16)
    n_pack = d // LRU_PACK
    ra = jnp.concatenate(
        [_dot(xcb[:, c * LRU_PACK:(c + 1) * LRU_PACK], wa_ref[c]) for c in range(n_pack)], axis=1)
    ri = jnp.concatenate(
        [_dot(xcb[:, c * LRU_PACK:(c + 1) * LRU_PACK], wi_ref[c]) for c in range(n_pack)], axis=1)
    tr1 = jnp.tanh(ra + ba_ref[...]) + 1.0
    ig = 0.5 * jnp.tanh(ri + bi_ref[...]) + 0.5
    neg_log_a = tr1 * (0.5 * LRU_C * jax.nn.softplus(-lam_ref[...]))
    a = jnp.exp2(neg_log_a * (-LOG2_E))
    one_m_a2 = jnp.tanh(neg_log_a) * (a * a + 1.0)
    root = jnp.where(one_m_a2 > 0.0, one_m_a2 * lax.rsqrt(one_m_a2), 0.0)
    bt = root * (ig * xc)

    a3 = a.reshape(ng, SUBLANES, d)
    b3 = bt.reshape(ng, SUBLANES, d)
    step = 1
    while step < SUBLANES:
        keep = sub >= step
        a_prev = jnp.where(keep, pltpu.roll(a3, step, 1), 1.0)
        b_prev = jnp.where(keep, pltpu.roll(b3, step, 1), 0.0)
        b3 = b3 + a3 * b_prev
        a3 = a3 * a_prev
        step *= 2
    h = hcarry[0:1, :]
    hs = []
    for k in range(ng):
        hk = b3[k] + a3[k] * h
        hs.append(hk)
        h = hk[SUBLANES - 1:SUBLANES]
    hcarry[0:1, :] = h
    hfull = jnp.concatenate(hs, axis=0)
    o_ref[0] = (hfull * gr_ref[0]).astype(o_ref.dtype)


def _rnn(xr, gr, cw, cb, wa, wi, ba, bi, lam):
    b, s, d = xr.shape
    ts = RNN_ROWS
    row = lambda a: _resident(a.shape)
    return pl.pallas_call(
        _rnn_kernel,
        grid=(b, s // ts),
        in_specs=[
            pl.BlockSpec((1, ts, d), lambda i, j: (i, j, 0)),
            pl.BlockSpec((1, ts, d), lambda i, j: (i, j, 0)),
            row(cw), row(cb), row(wa), row(wi), row(ba), row(bi), row(lam),
        ],
        out_specs=pl.BlockSpec((1, ts, d), lambda i, j: (i, j, 0)),
        out_shape=jax.ShapeDtypeStruct((b, s, d), BF16),
        scratch_shapes=[pltpu.VMEM((SUBLANES, d), F32), pltpu.VMEM((SUBLANES, d), F32)],
        compiler_params=_params("parallel", "arbitrary"),
        name="rg_lru",
    )(xr, gr, cw, cb, wa, wi, ba, bi, lam)


def _merge_mlp_kernel(x_ref, attn_ref, rnn_ref, ga_ref, gb_ref, wua_ref, wur_ref, wo_ref,
                      n2_ref, w1_ref, w2_ref, fn_ref, o_ref, *, final):
    up_a = _dot(attn_ref[...], wua_ref[0])
    up_r = _dot(rnn_ref[...], wur_ref[0])
    merged = _sigmoid(ga_ref[...]) * up_a + _sigmoid(gb_ref[...]) * up_r
    x = x_ref[...] + _dot(merged.astype(BF16), wo_ref[0])
    var = jnp.mean(x * x, axis=-1, keepdims=True)
    hn = (x * lax.rsqrt(var + EPS) * n2_ref[...]).astype(BF16)
    d_ff = w1_ref.shape[2]
    acc = jnp.zeros_like(x)
    for c in range(d_ff // MLP_FF_CHUNK):
        lo, hi = c * MLP_FF_CHUNK, (c + 1) * MLP_FF_CHUNK
        h1 = jnp.maximum(_dot(hn, w1_ref[0, :, lo:hi]), 0.0)
        acc = acc + _dot((h1 * h1).astype(BF16), w2_ref[0, lo:hi, :])
    x = x + acc
    if final:
        var = jnp.mean(x * x, axis=-1, keepdims=True)
        x = x * lax.rsqrt(var + EPS) * fn_ref[...]
    o_ref[...] = x


def _merge_mlp(x2, attn, rnn, ga, gb, wua, wur, wo, n2, w1, w2, fn, layer, final):
    t, d = x2.shape
    tm = MLP_ROWS
    rows = lambda w: pl.BlockSpec((tm, w), lambda i: (i, 0))
    return pl.pallas_call(
        functools.partial(_merge_mlp_kernel, final=final),
        grid=(t // tm,),
        in_specs=[
            rows(d), rows(attn.shape[1]), rows(rnn.shape[1]), rows(d), rows(d),
            _resident_layer(wua.shape, layer), _resident_layer(wur.shape, layer), _resident_layer(wo.shape, layer),
            _resident(n2.shape), _resident_layer(w1.shape, layer), _resident_layer(w2.shape, layer),
            _resident(fn.shape),
        ],
        out_specs=rows(d),
        out_shape=jax.ShapeDtypeStruct((t, d), F32),
        compiler_params=_params("parallel"),
        name="merge_mlp",
    )(x2, attn, rnn, ga, gb, wua, wur, wo, n2, w1, w2, fn)


def _block_diag_pack(w, pack):
    nblk, k, _ = w.shape
    per = pack // k
    w4 = w.reshape(nblk // per, per, k, k)
    eye = jnp.eye(per, dtype=w.dtype)
    out = jnp.einsum("cpij,pq->cpiqj", w4, eye)
    return out.reshape(nblk // per, pack, pack)


def _in_proj_layout(d):
    dkv = D_KV
    outputs = ((D_ATTN, F32, True), (2 * dkv, F32, False), (N_GROUPS * GATE_LANES, F32, False),
               (K_AUG, BF16, False), (dkv, BF16, False), (4 * dkv, BF16, False),
               (d, F32, False), (d, F32, False), (d, F32, False), (d, F32, False))
    chunks = (
        (0, 1, 0, D_ATTN, (("transposed", 0, D_ATTN, 0, 0),)),
        (0, 1, D_ATTN, 2 * dkv, (("plain", 0, 2 * dkv, 1, 0),)),
        (2, None, 0, N_GROUPS * GATE_LANES, (("plain", 0, N_GROUPS * GATE_LANES, 2, 0),)),
        (0, None, D_ATTN + 2 * dkv, 2 * dkv, (("keys", 0, dkv, 3, 0), ("values", dkv, dkv, 5, 0))),
        (0, None, D_ATTN + 4 * dkv, 2 * dkv, (("plain", 0, dkv, 4, 0), ("values", dkv, dkv, 5, 2 * dkv))),
        (3, None, 0, d, (("plain", 0, d, 6, 0),)),
        (3, None, d, d, (("gelu", 0, d, 7, 0),)),
        (3, None, 2 * d, d, (("plain", 0, d, 8, 0),)),
        (3, None, 3 * d, d, (("plain", 0, d, 9, 0),)),
    )
    return chunks, outputs


def _in_proj_weights(w_in):
    head = D_ATTN + 6 * D_KV
    n_gates = N_HEADS * 3
    n_precise = D_ATTN + 2 * D_KV
    wt = jnp.swapaxes(w_in, 1, 2)
    scale = jnp.where(jnp.arange(head) < D_ATTN, ATTN_SCALE * LOG2_E, 1.0).astype(F32)[:, None]
    wa = wt[:, :head] * scale
    wa_hi = wa.astype(BF16)
    wa_lo = (wa[:, :n_precise] - wa_hi[:, :n_precise].astype(F32)).astype(BF16)
    g_nsa = wt[:, head:head + n_gates]
    per_group = HEADS_PER_GROUP * 3
    wg = jnp.concatenate(
        [jnp.pad(g_nsa[:, g * per_group:(g + 1) * per_group], ((0, 0), (0, GATE_LANES - per_group), (0, 0)))
         for g in range(N_GROUPS)], axis=1).astype(BF16)
    wb = wt[:, head + n_gates:].astype(BF16)
    return wa_hi, wa_lo, wg, wb


def _compress_weights(w1_k, w1_v, w2_k, w2_v, pos_k, pos_v):
    w1 = jnp.stack([w1_k, w1_v]).reshape(2, CMP_BLOCK, HEAD_DIM, CMP_HIDDEN)
    pos = jnp.stack([pos_k, pos_v])[:, :, None, :]
    pos = jnp.broadcast_to(pos, (2, CMP_BLOCK, SUBLANES, HEAD_DIM))
    w1h, w1l = _split(w1)
    w2h, w2l = _split(jnp.stack([w2_k, w2_v]))
    return pos, w1h, w1l, w2h, w2l


def kernel(x, norm1_w, w_in, cmp_pos_k, cmp_pos_v, cmp_k_w1, cmp_k_w2, cmp_v_w1, cmp_v_w2, conv_w, conv_b, lru_w_a, lru_b_a, lru_w_i, lru_b_i, lru_lambda, w_up_attn, w_up_rnn, w_out, norm2_w, mlp_w1, mlp_w2, final_norm_w):
    b, s, d = x.shape
    depth = w_in.shape[0]
    t = b * s
    ns = s // CMP_STRIDE
    nb = s // SLC_BLOCK
    assert s % ATTN_Q == 0 and s % RNN_ROWS == 0 and t % PROJ_ROWS == 0 and t % MLP_ROWS == 0
    assert ATTN_Q == ATTN_KV and WINDOW == 2 * ATTN_KV and nb >= N_SELECT
    mt = jnp.asarray(_importance_matrix(ns, nb), BF16)
    x2 = x.reshape(t, d)
    fn = final_norm_w.reshape(1, d)
    chunks, proj_outputs = _in_proj_layout(d)
    proj_w = _in_proj_weights(w_in)
    wua, wur, wo, w1, w2 = (w.astype(BF16) for w in (w_up_attn, w_up_rnn, w_out, mlp_w1, mlp_w2))
    for l in range(depth):
        (qt, kv_c, gates, k_s, k_w, vall, xr, gr, g_a, g_b) = _norm_proj(
            x2, norm1_w[l].reshape(1, d), proj_w, l, chunks, proj_outputs)

        cmp_w = _compress_weights(cmp_k_w1[l], cmp_v_w1[l], cmp_k_w2[l], cmp_v_w2[l], cmp_pos_k[l], cmp_pos_v[l])
        kvc = _compress(kv_c.reshape(b, s, 2 * D_KV), *cmp_w, which=0)
        kvct = _compress(kv_c.reshape(b, s, 2 * D_KV), *cmp_w, which=1)

        oc, selx = _select(qt, kvc, kvct, mt, b)
        attn = _attention(qt, k_s.reshape(b, s, K_AUG), k_w.reshape(b, s, D_KV),
                          vall.reshape(b, s, 4 * D_KV), selx,
                          gates.reshape(b, s, N_GROUPS * GATE_LANES), oc)

        rnn = _rnn(xr.reshape(b, s, d), gr.reshape(b, s, d),
                   jnp.pad(conv_w[l], ((0, SUBLANES - CONV_WIDTH), (0, 0))), conv_b[l].reshape(1, d),
                   _block_diag_pack(0.5 * lru_w_a[l], LRU_PACK).astype(BF16),
                   _block_diag_pack(0.5 * lru_w_i[l], LRU_PACK).astype(BF16),
                   0.5 * lru_b_a[l].reshape(1, d), 0.5 * lru_b_i[l].reshape(1, d), lru_lambda[l].reshape(1, d))

        x2 = _merge_mlp(x2, attn.reshape(t, D_ATTN), rnn.reshape(t, d), g_a, g_b, wua, wur, wo,
                        norm2_w[l].reshape(1, d), w1, w2, fn, layer=l, final=(l == depth - 1))
    return x2.reshape(b, s, d)
```

```python
import functools

import numpy as np
import jax
import jax.numpy as jnp
from jax import lax
from jax.experimental import pallas as pl
from jax.experimental.pallas import tpu as pltpu

F32 = jnp.float32
BF16 = jnp.bfloat16

N_HEADS = 8
HEAD_DIM = 64
N_GROUPS = 2
HEADS_PER_GROUP = N_HEADS // N_GROUPS
D_ATTN = N_HEADS * HEAD_DIM
D_KV = N_GROUPS * HEAD_DIM
CMP_BLOCK = 32
CMP_STRIDE = 16
CMP_HIDDEN = 256
SLC_BLOCK = 64
N_SELECT = 16
N_LOCAL = 2
WINDOW = 512
ATTN_SCALE = HEAD_DIM ** -0.5
NEG = -1e30
FORCE_SCORE = 1e4
N_RNN_BLOCKS = 16
CONV_WIDTH = 4
LRU_C = 8.0
EPS = 1e-6
LOG2_E = 1.4426950408889634

VMEM_LIMIT_BYTES = 56 * 1024 * 1024
PROJ_ROWS = 512
MLP_ROWS = 512
MLP_FF_CHUNK = 1024
ATTN_Q = 256
ATTN_KV = 256
ATTN_UNROLL = 8
RNN_ROWS = 1024
LRU_PACK = 256
SUBLANES = 8
GATE_LANES = 128
PAT_ZERO, PAT_CAUSAL, PAT_FAR, PAT_NONE = 0, 1, 2, 3
SEL_ROWS = 16
K_AUG = 4 * HEAD_DIM
V_AUG = HEAD_DIM + 16


def _dot(a, b):
    return jnp.dot(a, b, preferred_element_type=F32)


def _dot_nt(a, b):
    return lax.dot_general(a, b, (((1,), (1,)), ((), ())), preferred_element_type=F32)


def _sigmoid(x):
    return 0.5 * jnp.tanh(0.5 * x) + 0.5


def _split(a):
    hi = a.astype(BF16)
    lo = (a - hi.astype(F32)).astype(BF16)
    return hi, lo


def _dot3(ah, al, bh, bl):
    return _dot(ah, bh) + _dot(ah, bl) + _dot(al, bh)


def _params(*sem):
    return pltpu.CompilerParams(dimension_semantics=sem, vmem_limit_bytes=VMEM_LIMIT_BYTES)


def _resident(shape):
    nd = len(shape)
    return pl.BlockSpec(shape, lambda *_: (0,) * nd, pipeline_mode=pl.Buffered(1))


def _resident_layer(stacked_shape, layer):
    nd = len(stacked_shape)
    return pl.BlockSpec((1,) + tuple(stacked_shape[1:]), lambda *_: (layer,) + (0,) * (nd - 1),
                        pipeline_mode=pl.Buffered(1))


def _norm_proj_kernel(x_ref, nw_ref, *refs, chunks, n_weights):
    w_refs, out_refs = refs[:n_weights], refs[n_weights:]
    x = x_ref[...]
    tm = x.shape[0]
    var = jnp.mean(x * x, axis=-1, keepdims=True)
    xn = x * lax.rsqrt(var + EPS) * nw_ref[...]
    xh, xl = _split(xn)
    lane = lax.broadcasted_iota(jnp.int32, (tm, D_KV), 1)
    for w_idx, lo_idx, c_start, c_width, pieces in chunks:
        wh = w_refs[w_idx][0, c_start:c_start + c_width, :]
        prod = _dot_nt(xh, wh)
        if lo_idx is not None:
            prod = prod + _dot_nt(xl, wh) + _dot_nt(xh, w_refs[lo_idx][0, c_start:c_start + c_width, :])
        for kind, off, width, out_idx, out_off in pieces:
            o_ref = out_refs[out_idx]
            acc = prod[:, off:off + width]
            if kind == "plain":
                o_ref[...] = acc.astype(o_ref.dtype)
            elif kind == "transposed":
                o_ref[...] = acc.T.astype(o_ref.dtype)
            elif kind == "gelu":
                o_ref[...] = jax.nn.gelu(acc).astype(o_ref.dtype)
            elif kind == "keys":
                row = lax.broadcasted_iota(jnp.int32, (tm, D_KV), 0)
                onehot = jnp.where(lane == (row // SLC_BLOCK) % (ATTN_KV // SLC_BLOCK), 1.0, 0.0)
                o_ref[:, :width] = acc.astype(o_ref.dtype)
                o_ref[:, width:] = onehot.astype(o_ref.dtype)
            else:
                ones_col = jnp.where(lane == HEAD_DIM, 1.0, 0.0)
                swapped = pltpu.roll(acc, HEAD_DIM, 1)
                o_ref[:, out_off:out_off + D_KV] = jnp.where(lane < HEAD_DIM, acc, ones_col).astype(o_ref.dtype)
                o_ref[:, out_off + D_KV:out_off + 2 * D_KV] = jnp.where(
                    lane < HEAD_DIM, swapped, ones_col).astype(o_ref.dtype)


def _norm_proj(x2, norm_w, weights, layer, chunks, outputs):
    t, d = x2.shape
    assert PROJ_ROWS % ATTN_KV == 0
    grid = (t // PROJ_ROWS,)
    out_shape, out_specs = [], []
    for w, dt, transposed in outputs:
        if transposed:
            out_shape.append(jax.ShapeDtypeStruct((w, t), dt))
            out_specs.append(pl.BlockSpec((w, PROJ_ROWS), lambda i: (0, i)))
        else:
            out_shape.append(jax.ShapeDtypeStruct((t, w), dt))
            out_specs.append(pl.BlockSpec((PROJ_ROWS, w), lambda i: (i, 0)))
    return pl.pallas_call(
        functools.partial(_norm_proj_kernel, chunks=chunks, n_weights=len(weights)),
        grid=grid,
        in_specs=[pl.BlockSpec((PROJ_ROWS, d), lambda i: (i, 0)), _resident(norm_w.shape)]
        + [_resident_layer(w.shape, layer) for w in weights],
        out_specs=out_specs,
        out_shape=out_shape,
        compiler_params=_params("parallel"),
        name="norm_proj",
    )(x2, norm_w, *weights)


def _compress_kernel(x_ref, pos_ref, w1h_ref, w1l_ref, w2h_ref, w2l_ref, o_ref, *, precise):
    ns = x_ref.shape[1] // CMP_STRIDE

    def times_w1(rows, q):
        wh = w1h_ref[0, q]
        if not precise:
            return _dot(rows.astype(BF16), wh)
        rh, rl = _split(rows)
        return (_dot(jnp.concatenate([rh, rl], axis=1), jnp.concatenate([wh, wh], axis=0))
                + _dot(rh, w1l_ref[0, q]))

    first = [jnp.zeros((ns, CMP_HIDDEN), F32) for _ in range(N_GROUPS)]
    second = [jnp.zeros((ns, CMP_HIDDEN), F32) for _ in range(N_GROUPS)]
    posb = jnp.zeros((SUBLANES, CMP_HIDDEN), F32)
    for p in range(CMP_STRIDE):
        xp = x_ref[0, pl.ds(p, ns, stride=CMP_STRIDE), :]
        for g in range(N_GROUPS):
            xg = xp[:, g * HEAD_DIM:(g + 1) * HEAD_DIM]
            first[g] = first[g] + times_w1(xg, p)
            second[g] = second[g] + times_w1(xg, p + CMP_STRIDE)
        posb = posb + times_w1(pos_ref[0, p], p) + times_w1(pos_ref[0, p + CMP_STRIDE], p + CMP_STRIDE)
    row = lax.broadcasted_iota(jnp.int32, (ns, HEAD_DIM), 0)
    for g in range(N_GROUPS):
        hid = jax.nn.gelu(first[g] + pltpu.roll(second[g], ns - 1, 0) + posb[0:1])
        if precise:
            hh, hl = _split(hid)
            c = _dot3(hh, hl, w2h_ref[0], w2l_ref[0])
        else:
            c = _dot(hid.astype(BF16), w2h_ref[0])
        c = jnp.where(row < ns - 1, c, 0.0)
        if precise:
            ch, cl = _split(c)
            o_ref[0, g] = jnp.concatenate([ch, ch, cl, jnp.zeros_like(ch)], axis=1)
        else:
            wide = jnp.concatenate([c, jnp.zeros_like(c)], axis=1)
            o_ref[0, g] = wide.T[:HEAD_DIM].astype(o_ref.dtype)


def _compress(kv_c, pos, w1h, w1l, w2h, w2l, which):
    b, s, _ = kv_c.shape
    ns = s // CMP_STRIDE
    g = N_GROUPS
    precise = which == 0
    out_tail = (ns, 4 * HEAD_DIM) if precise else (HEAD_DIM, ns)
    return pl.pallas_call(
        functools.partial(_compress_kernel, precise=precise),
        grid=(b,),
        in_specs=[
            pl.BlockSpec((1, s, D_KV), lambda i: (i, 0, which)),
            pl.BlockSpec((1, CMP_BLOCK, SUBLANES, HEAD_DIM), lambda i: (which, 0, 0, 0)),
            pl.BlockSpec((1, CMP_BLOCK, HEAD_DIM, CMP_HIDDEN), lambda i: (which, 0, 0, 0)),
            pl.BlockSpec((1, CMP_BLOCK, HEAD_DIM, CMP_HIDDEN), lambda i: (which, 0, 0, 0)),
            pl.BlockSpec((1, CMP_HIDDEN, HEAD_DIM), lambda i: (which, 0, 0)),
            pl.BlockSpec((1, CMP_HIDDEN, HEAD_DIM), lambda i: (which, 0, 0)),
        ],
        out_specs=pl.BlockSpec((1, g) + out_tail, lambda i: (i, 0, 0, 0)),
        out_shape=jax.ShapeDtypeStruct((b, g) + out_tail, BF16),
        compiler_params=_params("parallel"),
        name="compress",
    )(kv_c, pos, w1h, w1l, w2h, w2l)


def _importance_matrix(nc_pad, nb):
    r_s = SLC_BLOCK // CMP_STRIDE
    r_c = CMP_BLOCK // CMP_STRIDE
    mat = np.zeros((nb, nc_pad), np.float32)
    for j in range(nb):
        for m in range(r_s):
            for n in range(r_c):
                i = r_s * j - m - n
                if i >= 0:
                    mat[j, i] += 1.0
    return mat


def _select_kernel(q_ref, kc3_ref, vct_ref, mt_ref, oc_ref, selx_ref, score_ref, rank_ref):
    qi = pl.program_id(2)
    tq = q_ref.shape[1]
    ncp = kc3_ref.shape[2]
    nb = mt_ref.shape[0]
    nh = HEADS_PER_GROUP
    bpt = ATTN_KV // SLC_BLOCK
    half_heads = nh // 2

    t1 = qi * tq + lax.broadcasted_iota(jnp.int32, (1, tq), 1)
    ci = lax.broadcasted_iota(jnp.int32, (ncp, 1), 0)
    visible = (ci * CMP_STRIDE + (CMP_BLOCK - 1)) <= t1
    mask = jnp.concatenate([visible] * half_heads, axis=1)

    def scores(h):
        qt = jnp.concatenate([q_ref[r * HEAD_DIM:(r + 1) * HEAD_DIM, :]
                              for r in range(h * half_heads, (h + 1) * half_heads)], axis=1)
        qh, ql = _split(qt)
        rhs = jnp.concatenate([qh, ql, qh, jnp.zeros_like(qh)], axis=0)
        return _dot(kc3_ref[0, 0], rhs)

    def softmax(s):
        s = jnp.where(mask, s, NEG)
        m = jnp.max(s, axis=0, keepdims=True)
        e = jnp.exp2(s - m)
        den = jnp.sum(e, axis=0, keepdims=True)
        inv = jnp.where(m > 0.5 * NEG, 1.0 / den, 0.0)
        return e * inv

    s_halves = [scores(h) for h in range(2)]
    ps = None
    for h in range(2):
        p = softmax(s_halves[h])
        oc = _dot(vct_ref[0, 0], p.astype(BF16))
        for k in range(half_heads):
            r = h * half_heads + k
            pk = p[:, k * tq:(k + 1) * tq]
            ps = pk if ps is None else ps + pk
            oc_ref[0, 0, 0, r * HEAD_DIM:(r + 1) * HEAD_DIM, :] = oc[:, k * tq:(k + 1) * tq]

    blk = lax.broadcasted_iota(jnp.int32, (nb, 1), 0)
    cur = t1 // SLC_BLOCK
    valid = blk <= cur

    def store_rows(selb):
        for jt in range(nb // bpt):
            rows = jnp.concatenate([selb[jt * bpt:(jt + 1) * bpt], jnp.zeros((SEL_ROWS - bpt, tq), F32)], axis=0)
            selx_ref[0, 0, 0, jt] = rows.astype(selx_ref.dtype)

    need_rank = (qi + 1) * tq > N_SELECT * SLC_BLOCK

    @pl.when(jnp.logical_not(need_rank))
    def _():
        store_rows(jnp.where(valid, 0.0, NEG))

    @pl.when(need_rank)
    def _():
        p1 = ps.astype(BF16)
        r1 = ps - p1.astype(F32)
        p2 = r1.astype(BF16)
        p3 = (r1 - p2.astype(F32)).astype(BF16)
        mt = mt_ref[...]
        imp = _dot(mt, p1) + _dot(mt, p2) + _dot(mt, p3)
        forced = (blk == 0) | (blk > cur - N_LOCAL)
        score_ref[...] = jnp.where(valid, jnp.where(forced, FORCE_SCORE, imp), NEG)
        rank_ref[...] = jnp.zeros(rank_ref.shape, jnp.int32)

    nchunk = nb // SUBLANES
    last_visible = ((qi + 1) * tq - 1) // SLC_BLOCK
    sub = lax.broadcasted_iota(jnp.int32, (SUBLANES, 1), 0)
    for vi in range(nchunk):
        @pl.when(need_rank & (vi * SUBLANES <= last_visible))
        def _():
            for i in range(vi * SUBLANES, (vi + 1) * SUBLANES):
                si = score_ref[i:i + 1, :]
                for v in range(nchunk):
                    rows = slice(SUBLANES * v, SUBLANES * (v + 1))
                    chunk = score_ref[rows, :]
                    if v > vi:
                        beats = jnp.where(si >= chunk, 1, 0)
                    elif v < vi:
                        beats = jnp.where(si > chunk, 1, 0)
                    else:
                        beats = jnp.where(sub > i % SUBLANES, jnp.where(si >= chunk, 1, 0),
                                          jnp.where(si > chunk, 1, 0))
                    rank_ref[rows, :] = rank_ref[rows, :] + beats

    @pl.when(need_rank)
    def _():
        sel = valid & (rank_ref[...] < N_SELECT)
        store_rows(jnp.where(sel, 0.0, NEG))


def _select(qt, kc3, vct, mt, b):
    s = qt.shape[1] // b
    g = N_GROUPS
    ncp = kc3.shape[2]
    nb = mt.shape[0]
    gw = HEADS_PER_GROUP * HEAD_DIM
    n_sel = nb // (ATTN_KV // SLC_BLOCK)
    grid = (b, g, s // ATTN_Q)
    return pl.pallas_call(
        _select_kernel,
        grid=grid,
        in_specs=[
            pl.BlockSpec((gw, ATTN_Q), lambda i, j, k: (j, i * (s // ATTN_Q) + k)),
            pl.BlockSpec((1, 1, ncp, 4 * HEAD_DIM), lambda i, j, k: (i, j, 0, 0)),
            pl.BlockSpec((1, 1, HEAD_DIM, ncp), lambda i, j, k: (i, j, 0, 0)),
            pl.BlockSpec((nb, ncp), lambda i, j, k: (0, 0)),
        ],
        out_specs=[
            pl.BlockSpec((1, 1, 1, gw, ATTN_Q), lambda i, j, k: (i, j, k, 0, 0)),
            pl.BlockSpec((1, 1, 1, n_sel, SEL_ROWS, ATTN_Q), lambda i, j, k: (i, j, k, 0, 0, 0)),
        ],
        out_shape=[
            jax.ShapeDtypeStruct((b, g, s // ATTN_Q, gw, ATTN_Q), F32),
            jax.ShapeDtypeStruct((b, g, s // ATTN_Q, n_sel, SEL_ROWS, ATTN_Q), BF16),
        ],
        scratch_shapes=[pltpu.VMEM((nb, ATTN_Q), F32), pltpu.VMEM((nb, ATTN_Q), jnp.int32)],
        compiler_params=_params("parallel", "parallel", "parallel"),
        name="cmp_select",
    )(qt, kc3, vct, mt)


def _attn_kernel(q_ref, ks_ref, kw_ref, vs_ref, vw_ref, selx_ref, g_ref, oc_ref, o_ref,
                 m_ref, acc_ref, s_a, s_b, smax_ref, pat_ref):
    grp = pl.program_id(1)
    qi = pl.program_id(2)
    tq = q_ref.shape[1]
    kv = ATTN_KV
    nh = HEADS_PER_GROUP
    n = nh * tq
    slc, win = 0, 1

    qt4 = q_ref[...].astype(BF16)
    qt = jnp.concatenate([qt4[r * HEAD_DIM:(r + 1) * HEAD_DIM] for r in range(nh)], axis=1)
    zq = jnp.zeros_like(qt)
    q2 = jnp.concatenate([jnp.where(grp == 0, qt, zq), jnp.where(grp == 1, qt, zq)], axis=0)
    pad = jnp.zeros((K_AUG - D_KV - SEL_ROWS, n), BF16)

    m_ref[...] = jnp.full(m_ref.shape, NEG, F32)
    acc_ref[...] = jnp.zeros(acc_ref.shape, F32)

    @pl.when(qi == 0)
    def _():
        row = lax.broadcasted_iota(jnp.int32, (kv, tq), 0)
        lane = lax.broadcasted_iota(jnp.int32, (kv, tq), 1)
        pat_ref[PAT_ZERO] = jnp.zeros((kv, tq), F32)
        pat_ref[PAT_CAUSAL] = jnp.where(row <= lane, 0.0, NEG)
        pat_ref[PAT_FAR] = jnp.where(lane < row, 0.0, NEG)
        pat_ref[PAT_NONE] = jnp.full((kv, tq), NEG, F32)

    def lanes4(x):
        return jnp.concatenate([x] * nh, axis=1)

    def rows(j):
        return pl.ds(pl.multiple_of(j * kv, kv), kv)

    def store_scores(s, s_ref, slot):
        s_ref[...] = s
        smax_ref[slot] = jnp.max(s, axis=0, keepdims=True)

    def slc_scores(j):
        rhs = jnp.concatenate([q2, lanes4(selx_ref[0, 0, 0, j]), pad], axis=0)
        return _dot(ks_ref[0, rows(j), :], rhs)

    def produce_slc(s_ref, slot, j):
        store_scores(slc_scores(j), s_ref, slot)

    def produce_diag(s_ref, slot):
        store_scores(slc_scores(qi) + lanes4(pat_ref[PAT_CAUSAL]), s_ref, slot)

    def produce_win(s_ref, slot, j, pat):
        s = _dot(kw_ref[0, rows(j), :], q2)
        store_scores(s if pat is None else s + lanes4(pat_ref[pat]), s_ref, slot)

    def consume(s_ref, slot, v_ref, j, kind):
        m_old = m_ref[kind]
        m_new = jnp.maximum(m_old, smax_ref[slot])
        alpha = jnp.exp2(m_old - m_new)
        p = jnp.exp2(s_ref[...] - m_new)
        v_t = v_ref[0, rows(j), :][:, :V_AUG]
        pv = lax.dot_general(v_t, p.astype(BF16), (((0,), (0,)), ((), ())),
                             preferred_element_type=F32)
        acc_ref[kind] = alpha * acc_ref[kind] + pv
        m_ref[kind] = m_new

    def plain(j):
        return (lambda r, sl: produce_slc(r, sl, j)), (lambda r, sl: consume(r, sl, vs_ref, j, slc))

    def window(j, pat):
        return (lambda r, sl: produce_win(r, sl, j, pat)), (lambda r, sl: consume(r, sl, vw_ref, j, win))

    diag = (produce_diag, lambda r, sl: consume(r, sl, vs_ref, qi, slc))

    def run(stages):
        bufs = ((s_a, 0), (s_b, 1))
        for i, (_, consume_i) in enumerate(stages):
            if i + 1 < len(stages):
                stages[i + 1][0](*bufs[(i + 1) % 2])
            consume_i(*bufs[i % 2])

    @pl.when(qi == 0)
    def _():
        produce_diag(s_a, 0)
        run([diag, window(0, PAT_CAUSAL)])

    @pl.when(qi > 0)
    def _():
        produce_slc(s_a, 0, 0)

        def body(u, carry):
            j = ATTN_UNROLL * u
            bufs = ((s_a, 0), (s_b, 1))
            for k in range(ATTN_UNROLL):
                produce_slc(*bufs[(k + 1) % 2], j + k + 1)
                consume(*bufs[k % 2], vs_ref, j + k, slc)
            return carry

        n_loop = (qi - 1) // ATTN_UNROLL
        lax.fori_loop(0, n_loop, body, 0)
        left = qi - ATTN_UNROLL * n_loop

        for k in range(1, ATTN_UNROLL + 1):
            @pl.when(left == k)
            def _():
                far = jnp.where(qi >= 2, PAT_FAR, PAT_NONE) if k == 1 else PAT_FAR
                run([plain(qi - k + i) for i in range(k)]
                    + [diag, window(qi, PAT_CAUSAL), window(qi - 1, None), window(jnp.maximum(qi - 2, 0), far)])

    gt = _sigmoid(g_ref[0]).T
    o_slc = acc_ref[slc, :HEAD_DIM] * (1.0 / acc_ref[slc, HEAD_DIM:HEAD_DIM + 1])
    o_win = acc_ref[win, :HEAD_DIM] * (1.0 / acc_ref[win, HEAD_DIM:HEAD_DIM + 1])
    outs = []
    for r in range(nh):
        cols = slice(r * tq, (r + 1) * tq)
        o_cmp = oc_ref[0, 0, 0, r * HEAD_DIM:(r + 1) * HEAD_DIM, :]
        outs.append(gt[3 * r:3 * r + 1] * o_cmp + gt[3 * r + 1:3 * r + 2] * o_slc[:, cols]
                    + gt[3 * r + 2:3 * r + 3] * o_win[:, cols])
    o_ref[0] = jnp.concatenate(outs, axis=0).T.astype(o_ref.dtype)


def _attention(qt, ks, kw, vall, selx, gates, oc):
    b, s, _ = ks.shape
    g = N_GROUPS
    gw = HEADS_PER_GROUP * HEAD_DIM
    tq = ATTN_Q
    n = HEADS_PER_GROUP * tq
    n_sel = selx.shape[3]
    return pl.pallas_call(
        _attn_kernel,
        grid=(b, g, s // tq),
        in_specs=[
            pl.BlockSpec((gw, tq), lambda i, j, k: (j, i * (s // tq) + k)),
            pl.BlockSpec((1, s, K_AUG), lambda i, j, k: (i, 0, 0)),
            pl.BlockSpec((1, s, D_KV), lambda i, j, k: (i, 0, 0)),
            pl.BlockSpec((1, s, D_KV), lambda i, j, k: (i, 0, j)),
            pl.BlockSpec((1, s, D_KV), lambda i, j, k: (i, 0, N_GROUPS + j)),
            pl.BlockSpec((1, 1, 1, n_sel, SEL_ROWS, tq), lambda i, j, k: (i, j, k, 0, 0, 0)),
            pl.BlockSpec((1, tq, GATE_LANES), lambda i, j, k: (i, k, j)),
            pl.BlockSpec((1, 1, 1, gw, tq), lambda i, j, k: (i, j, k, 0, 0)),
        ],
        out_specs=pl.BlockSpec((1, tq, gw), lambda i, j, k: (i, k, j)),
        out_shape=jax.ShapeDtypeStruct((b, s, D_ATTN), BF16),
        scratch_shapes=[
            pltpu.VMEM((2, 1, n), F32),
            pltpu.VMEM((2, V_AUG, n), F32),
            pltpu.VMEM((ATTN_KV, n), F32),
            pltpu.VMEM((ATTN_KV, n), F32),
            pltpu.VMEM((2, 1, n), F32),
            pltpu.VMEM((4, ATTN_KV, tq), F32),
        ],
        compiler_params=_params("parallel", "parallel", "arbitrary"),
        name="slc_win_attn",
    )(qt, ks, kw, vall, vall, selx, gates, oc)


def _rnn_kernel(xr_ref, gr_ref, cw_ref, cb_ref, wa_ref, wi_ref, ba_ref, bi_ref, lam_ref, o_ref,
                xprev, hcarry):
    si = pl.program_id(1)
    ts = xr_ref.shape[1]
    d = xr_ref.shape[2]

    @pl.when(si == 0)
    def _():
        xprev[...] = jnp.zeros(xprev.shape, F32)
        hcarry[...] = jnp.zeros(hcarry.shape, F32)

    ng = ts // SUBLANES
    x3 = xr_ref[0].reshape(ng, SUBLANES, d)
    sub = lax.broadcasted_iota(jnp.int32, (ng, SUBLANES, d), 1)
    xc = cb_ref[...] + x3 * cw_ref[CONV_WIDTH - 1:CONV_WIDTH, :]
    for back in range(1, CONV_WIDTH):
        rolled = pltpu.roll(x3, back, 1)
        prev = jnp.concatenate([pltpu.roll(xprev[...], back, 0)[None], rolled[:-1]], axis=0)
        tap = CONV_WIDTH - 1 - back
        xc = xc + jnp.where(sub >= back, rolled, prev) * cw_ref[tap:tap + 1, :]
    xprev[...] = x3[ng - 1]
    xc = xc.reshape(ts, d)

    xcb = xc.astype(BF16)
    n_pack = d // LRU_PACK
    ra = jnp.concatenate(
        [_dot(xcb[:, c * LRU_PACK:(c + 1) * LRU_PACK], wa_ref[c]) for c in range(n_pack)], axis=1)
    ri = jnp.concatenate(
        [_dot(xcb[:, c * LRU_PACK:(c + 1) * LRU_PACK], wi_ref[c]) for c in range(n_pack)], axis=1)
    tr1 = jnp.tanh(ra + ba_ref[...]) + 1.0
    ig = 0.5 * jnp.tanh(ri + bi_ref[...]) + 0.5
    neg_log_a = tr1 * (0.5 * LRU_C * jax.nn.softplus(-lam_ref[...]))
    a = jnp.exp2(neg_log_a * (-LOG2_E))
    one_m_a2 = jnp.tanh(neg_log_a) * (a * a + 1.0)
    root = jnp.where(one_m_a2 > 0.0, one_m_a2 * lax.rsqrt(one_m_a2), 0.0)
    bt = root * (ig * xc)

    a3 = a.reshape(ng, SUBLANES, d)
    b3 = bt.reshape(ng, SUBLANES, d)
    step = 1
    while step < SUBLANES:
        keep = sub >= step
        a_prev = jnp.where(keep, pltpu.roll(a3, step, 1), 1.0)
        b_prev = jnp.where(keep, pltpu.roll(b3, step, 1), 0.0)
        b3 = b3 + a3 * b_prev
        a3 = a3 * a_prev
        step *= 2
    h = hcarry[0:1, :]
    hs = []
    for k in range(ng):
        hk = b3[k] + a3[k] * h
        hs.append(hk)
        h = hk[SUBLANES - 1:SUBLANES]
    hcarry[0:1, :] = h
    hfull = jnp.concatenate(hs, axis=0)
    o_ref[0] = (hfull * gr_ref[0]).astype(o_ref.dtype)


def _rnn(xr, gr, cw, cb, wa, wi, ba, bi, lam):
    b, s, d = xr.shape
    ts = RNN_ROWS
    row = lambda a: _resident(a.shape)
    return pl.pallas_call(
        _rnn_kernel,
        grid=(b, s // ts),
        in_specs=[
            pl.BlockSpec((1, ts, d), lambda i, j: (i, j, 0)),
            pl.BlockSpec((1, ts, d), lambda i, j: (i, j, 0)),
            row(cw), row(cb), row(wa), row(wi), row(ba), row(bi), row(lam),
        ],
        out_specs=pl.BlockSpec((1, ts, d), lambda i, j: (i, j, 0)),
        out_shape=jax.ShapeDtypeStruct((b, s, d), BF16),
        scratch_shapes=[pltpu.VMEM((SUBLANES, d), F32), pltpu.VMEM((SUBLANES, d), F32)],
        compiler_params=_params("parallel", "arbitrary"),
        name="rg_lru",
    )(xr, gr, cw, cb, wa, wi, ba, bi, lam)


def _merge_mlp_kernel(x_ref, attn_ref, rnn_ref, ga_ref, gb_ref, wua_ref, wur_ref, wo_ref,
                      n2_ref, w1_ref, w2_ref, fn_ref, o_ref, *, final):
    up_a = _dot(attn_ref[...], wua_ref[0])
    up_r = _dot(rnn_ref[...], wur_ref[0])
    merged = _sigmoid(ga_ref[...]) * up_a + _sigmoid(gb_ref[...]) * up_r
    x = x_ref[...] + _dot(merged.astype(BF16), wo_ref[0])
    var = jnp.mean(x * x, axis=-1, keepdims=True)
    hn = (x * lax.rsqrt(var + EPS) * n2_ref[...]).astype(BF16)
    d_ff = w1_ref.shape[2]
    acc = jnp.zeros_like(x)
    for c in range(d_ff // MLP_FF_CHUNK):
        lo, hi = c * MLP_FF_CHUNK, (c + 1) * MLP_FF_CHUNK
        h1 = jnp.maximum(_dot(hn, w1_ref[0, :, lo:hi]), 0.0)
        acc = acc + _dot((h1 * h1).astype(BF16), w2_ref[0, lo:hi, :])
    x = x + acc
    if final:
        var = jnp.mean(x * x, axis=-1, keepdims=True)
        x = x * lax.rsqrt(var + EPS) * fn_ref[...]
    o_ref[...] = x


def _merge_mlp(x2, attn, rnn, ga, gb, wua, wur, wo, n2, w1, w2, fn, layer, final):
    t, d = x2.shape
    tm = MLP_ROWS
    rows = lambda w: pl.BlockSpec((tm, w), lambda i: (i, 0))
    return pl.pallas_call(
        functools.partial(_merge_mlp_kernel, final=final),
        grid=(t // tm,),
        in_specs=[
            rows(d), rows(attn.shape[1]), rows(rnn.shape[1]), rows(d), rows(d),
            _resident_layer(wua.shape, layer), _resident_layer(wur.shape, layer), _resident_layer(wo.shape, layer),
            _resident(n2.shape), _resident_layer(w1.shape, layer), _resident_layer(w2.shape, layer),
            _resident(fn.shape),
        ],
        out_specs=rows(d),
        out_shape=jax.ShapeDtypeStruct((t, d), F32),
        compiler_params=_params("parallel"),
        name="merge_mlp",
    )(x2, attn, rnn, ga, gb, wua, wur, wo, n2, w1, w2, fn)


def _block_diag_pack(w, pack):
    nblk, k, _ = w.shape
    per = pack // k
    w4 = w.reshape(nblk // per, per, k, k)
    eye = jnp.eye(per, dtype=w.dtype)
    out = jnp.einsum("cpij,pq->cpiqj", w4, eye)
    return out.reshape(nblk // per, pack, pack)


def _in_proj_layout(d):
    dkv = D_KV
    outputs = ((D_ATTN, F32, True), (2 * dkv, F32, False), (N_GROUPS * GATE_LANES, F32, False),
               (K_AUG, BF16, False), (dkv, BF16, False), (4 * dkv, BF16, False),
               (d, F32, False), (d, F32, False), (d, F32, False), (d, F32, False))
    chunks = (
        (0, 1, 0, D_ATTN, (("transposed", 0, D_ATTN, 0, 0),)),
        (0, 1, D_ATTN, 2 * dkv, (("plain", 0, 2 * dkv, 1, 0),)),
        (2, None, 0, N_GROUPS * GATE_LANES, (("plain", 0, N_GROUPS * GATE_LANES, 2, 0),)),
        (0, None, D_ATTN + 2 * dkv, 2 * dkv, (("keys", 0, dkv, 3, 0), ("values", dkv, dkv, 5, 0))),
        (0, None, D_ATTN + 4 * dkv, 2 * dkv, (("plain", 0, dkv, 4, 0), ("values", dkv, dkv, 5, 2 * dkv))),
        (3, None, 0, d, (("plain", 0, d, 6, 0),)),
        (3, None, d, d, (("gelu", 0, d, 7, 0),)),
        (3, None, 2 * d, d, (("plain", 0, d, 8, 0),)),
        (3, None, 3 * d, d, (("plain", 0, d, 9, 0),)),
    )
    return chunks, outputs


def _in_proj_weights(w_in):
    head = D_ATTN + 6 * D_KV
    n_gates = N_HEADS * 3
    n_precise = D_ATTN + 2 * D_KV
    wt = jnp.swapaxes(w_in, 1, 2)
    scale = jnp.where(jnp.arange(head) < D_ATTN, ATTN_SCALE * LOG2_E, 1.0).astype(F32)[:, None]
    wa = wt[:, :head] * scale
    wa_hi = wa.astype(BF16)
    wa_lo = (wa[:, :n_precise] - wa_hi[:, :n_precise].astype(F32)).astype(BF16)
    g_nsa = wt[:, head:head + n_gates]
    per_group = HEADS_PER_GROUP * 3
    wg = jnp.concatenate(
        [jnp.pad(g_nsa[:, g * per_group:(g + 1) * per_group], ((0, 0), (0, GATE_LANES - per_group), (0, 0)))
         for g in range(N_GROUPS)], axis=1).astype(BF16)
    wb = wt[:, head + n_gates:].astype(BF16)
    return wa_hi, wa_lo, wg, wb


def _compress_weights(w1_k, w1_v, w2_k, w2_v, pos_k, pos_v):
    w1 = jnp.stack([w1_k, w1_v]).reshape(2, CMP_BLOCK, HEAD_DIM, CMP_HIDDEN)
    pos = jnp.stack([pos_k, pos_v])[:, :, None, :]
    pos = jnp.broadcast_to(pos, (2, CMP_BLOCK, SUBLANES, HEAD_DIM))
    w1h, w1l = _split(w1)
    w2h, w2l = _split(jnp.stack([w2_k, w2_v]))
    return pos, w1h, w1l, w2h, w2l


def kernel(x, norm1_w, w_in, cmp_pos_k, cmp_pos_v, cmp_k_w1, cmp_k_w2, cmp_v_w1, cmp_v_w2, conv_w, conv_b, lru_w_a, lru_b_a, lru_w_i, lru_b_i, lru_lambda, w_up_attn, w_up_rnn, w_out, norm2_w, mlp_w1, mlp_w2, final_norm_w):
    b, s, d = x.shape
    depth = w_in.shape[0]
    t = b * s
    ns = s // CMP_STRIDE
    nb = s // SLC_BLOCK
    assert s % ATTN_Q == 0 and s % RNN_ROWS == 0 and t % PROJ_ROWS == 0 and t % MLP_ROWS == 0
    assert ATTN_Q == ATTN_KV and WINDOW == 2 * ATTN_KV and nb >= N_SELECT
    mt = jnp.asarray(_importance_matrix(ns, nb), BF16)
    x2 = x.reshape(t, d)
    fn = final_norm_w.reshape(1, d)
    chunks, proj_outputs = _in_proj_layout(d)
    proj_w = _in_proj_weights(w_in)
    wua, wur, wo, w1, w2 = (w.astype(BF16) for w in (w_up_attn, w_up_rnn, w_out, mlp_w1, mlp_w2))
    for l in range(depth):
        (qt, kv_c, gates, k_s, k_w, vall, xr, gr, g_a, g_b) = _norm_proj(
            x2, norm1_w[l].reshape(1, d), proj_w, l, chunks, proj_outputs)

        cmp_w = _compress_weights(cmp_k_w1[l], cmp_v_w1[l], cmp_k_w2[l], cmp_v_w2[l], cmp_pos_k[l], cmp_pos_v[l])
        kvc = _compress(kv_c.reshape(b, s, 2 * D_KV), *cmp_w, which=0)
        kvct = _compress(kv_c.reshape(b, s, 2 * D_KV), *cmp_w, which=1)

        oc, selx = _select(qt, kvc, kvct, mt, b)
        attn = _attention(qt, k_s.reshape(b, s, K_AUG), k_w.reshape(b, s, D_KV),
                          vall.reshape(b, s, 4 * D_KV), selx,
                          gates.reshape(b, s, N_GROUPS * GATE_LANES), oc)

        rnn = _rnn(xr.reshape(b, s, d), gr.reshape(b, s, d),
                   jnp.pad(conv_w[l], ((0, SUBLANES - CONV_WIDTH), (0, 0))), conv_b[l].reshape(1, d),
                   _block_diag_pack(0.5 * lru_w_a[l], LRU_PACK).astype(BF16),
                   _block_diag_pack(0.5 * lru_w_i[l], LRU_PACK).astype(BF16),
                   0.5 * lru_b_a[l].reshape(1, d), 0.5 * lru_b_i[l].reshape(1, d), lru_lambda[l].reshape(1, d))

        x2 = _merge_mlp(x2, attn.reshape(t, D_ATTN), rnn.reshape(t, d), g_a, g_b, wua, wur, wo,
                        norm2_w[l].reshape(1, d), w1, w2, fn, layer=l, final=(l == depth - 1))
    return x2.reshape(b, s, d)
```

```python
import functools

import numpy as np
import jax
import jax.numpy as jnp
from jax import lax
from jax.experimental import pallas as pl
from jax.experimental.pallas import tpu as pltpu

F32 = jnp.float32
BF16 = jnp.bfloat16

N_HEADS = 8
HEAD_DIM = 64
N_GROUPS = 2
HEADS_PER_GROUP = N_HEADS // N_GROUPS
D_ATTN = N_HEADS * HEAD_DIM
D_KV = N_GROUPS * HEAD_DIM
CMP_BLOCK = 32
CMP_STRIDE = 16
CMP_HIDDEN = 256
SLC_BLOCK = 64
N_SELECT = 16
N_LOCAL = 2
WINDOW = 512
ATTN_SCALE = HEAD_DIM ** -0.5
NEG = -1e30
FORCE_SCORE = 1e4
N_RNN_BLOCKS = 16
CONV_WIDTH = 4
LRU_C = 8.0
EPS = 1e-6
LOG2_E = 1.4426950408889634

VMEM_LIMIT_BYTES = 56 * 1024 * 1024
PROJ_ROWS = 512
MLP_ROWS = 512
MLP_FF_CHUNK = 1024
ATTN_Q = 256
ATTN_KV = 256
ATTN_UNROLL = 8
RNN_ROWS = 1024
LRU_PACK = 256
SUBLANES = 8
GATE_LANES = 128
PAT_ZERO, PAT_CAUSAL, PAT_FAR, PAT_NONE = 0, 1, 2, 3
SEL_ROWS = 16
K_AUG = 4 * HEAD_DIM
V_AUG = HEAD_DIM + 16


def _dot(a, b):
    return jnp.dot(a, b, preferred_element_type=F32)


def _dot_nt(a, b):
    return lax.dot_general(a, b, (((1,), (1,)), ((), ())), preferred_element_type=F32)


def _sigmoid(x):
    return 0.5 * jnp.tanh(0.5 * x) + 0.5


def _split(a):
    hi = a.astype(BF16)
    lo = (a - hi.astype(F32)).astype(BF16)
    return hi, lo


def _dot3(ah, al, bh, bl):
    return _dot(ah, bh) + _dot(ah, bl) + _dot(al, bh)


def _params(*sem):
    return pltpu.CompilerParams(dimension_semantics=sem, vmem_limit_bytes=VMEM_LIMIT_BYTES)


def _resident(shape):
    nd = len(shape)
    return pl.BlockSpec(shape, lambda *_: (0,) * nd, pipeline_mode=pl.Buffered(1))


def _resident_layer(stacked_shape, layer):
    nd = len(stacked_shape)
    return pl.BlockSpec((1,) + tuple(stacked_shape[1:]), lambda *_: (layer,) + (0,) * (nd - 1),
                        pipeline_mode=pl.Buffered(1))


def _norm_proj_kernel(x_ref, nw_ref, *refs, chunks, n_weights):
    w_refs, out_refs = refs[:n_weights], refs[n_weights:]
    x = x_ref[...]
    tm = x.shape[0]
    var = jnp.mean(x * x, axis=-1, keepdims=True)
    xn = x * lax.rsqrt(var + EPS) * nw_ref[...]
    xh, xl = _split(xn)
    lane = lax.broadcasted_iota(jnp.int32, (tm, D_KV), 1)
    for w_idx, lo_idx, c_start, c_width, pieces in chunks:
        wh = w_refs[w_idx][0, c_start:c_start + c_width, :]
        prod = _dot_nt(xh, wh)
        if lo_idx is not None:
            prod = prod + _dot_nt(xl, wh) + _dot_nt(xh, w_refs[lo_idx][0, c_start:c_start + c_width, :])
        for kind, off, width, out_idx, out_off in pieces:
            o_ref = out_refs[out_idx]
            acc = prod[:, off:off + width]
            if kind == "plain":
                o_ref[...] = acc.astype(o_ref.dtype)
            elif kind == "transposed":
                o_ref[...] = acc.T.astype(o_ref.dtype)
            elif kind == "gelu":
                o_ref[...] = jax.nn.gelu(acc).astype(o_ref.dtype)
            elif kind == "keys":
                row = lax.broadcasted_iota(jnp.int32, (tm, D_KV), 0)
                onehot = jnp.where(lane == (row // SLC_BLOCK) % (ATTN_KV // SLC_BLOCK), 1.0, 0.0)
                o_ref[:, :width] = acc.astype(o_ref.dtype)
                o_ref[:, width:] = onehot.astype(o_ref.dtype)
            else:
                ones_col = jnp.where(lane == HEAD_DIM, 1.0, 0.0)
                swapped = pltpu.roll(acc, HEAD_DIM, 1)
                o_ref[:, out_off:out_off + D_KV] = jnp.where(lane < HEAD_DIM, acc, ones_col).astype(o_ref.dtype)
                o_ref[:, out_off + D_KV:out_off + 2 * D_KV] = jnp.where(
                    lane < HEAD_DIM, swapped, ones_col).astype(o_ref.dtype)


def _norm_proj(x2, norm_w, weights, layer, chunks, outputs):
    t, d = x2.shape
    assert PROJ_ROWS % ATTN_KV == 0
    grid = (t // PROJ_ROWS,)
    out_shape, out_specs = [], []
    for w, dt, transposed in outputs:
        if transposed:
            out_shape.append(jax.ShapeDtypeStruct((w, t), dt))
            out_specs.append(pl.BlockSpec((w, PROJ_ROWS), lambda i: (0, i)))
        else:
            out_shape.append(jax.ShapeDtypeStruct((t, w), dt))
            out_specs.append(pl.BlockSpec((PROJ_ROWS, w), lambda i: (i, 0)))
    return pl.pallas_call(
        functools.partial(_norm_proj_kernel, chunks=chunks, n_weights=len(weights)),
        grid=grid,
        in_specs=[pl.BlockSpec((PROJ_ROWS, d), lambda i: (i, 0)), _resident(norm_w.shape)]
        + [_resident_layer(w.shape, layer) for w in weights],
        out_specs=out_specs,
        out_shape=out_shape,
        compiler_params=_params("parallel"),
        name="norm_proj",
    )(x2, norm_w, *weights)


def _compress_kernel(x_ref, pos_ref, w1h_ref, w1l_ref, w2h_ref, w2l_ref, o_ref, *, precise):
    ns = x_ref.shape[1] // CMP_STRIDE

    def times_w1(rows, q):
        wh = w1h_ref[0, q]
        if not precise:
            return _dot(rows.astype(BF16), wh)
        rh, rl = _split(rows)
        return (_dot(jnp.concatenate([rh, rl], axis=1), jnp.concatenate([wh, wh], axis=0))
                + _dot(rh, w1l_ref[0, q]))

    first = [jnp.zeros((ns, CMP_HIDDEN), F32) for _ in range(N_GROUPS)]
    second = [jnp.zeros((ns, CMP_HIDDEN), F32) for _ in range(N_GROUPS)]
    posb = jnp.zeros((SUBLANES, CMP_HIDDEN), F32)
    for p in range(CMP_STRIDE):
        xp = x_ref[0, pl.ds(p, ns, stride=CMP_STRIDE), :]
        for g in range(N_GROUPS):
            xg = xp[:, g * HEAD_DIM:(g + 1) * HEAD_DIM]
            first[g] = first[g] + times_w1(xg, p)
            second[g] = second[g] + times_w1(xg, p + CMP_STRIDE)
        posb = posb + times_w1(pos_ref[0, p], p) + times_w1(pos_ref[0, p + CMP_STRIDE], p + CMP_STRIDE)
    row = lax.broadcasted_iota(jnp.int32, (ns, HEAD_DIM), 0)
    for g in range(N_GROUPS):
        hid = jax.nn.gelu(first[g] + pltpu.roll(second[g], ns - 1, 0) + posb[0:1])
        if precise:
            hh, hl = _split(hid)
            c = _dot3(hh, hl, w2h_ref[0], w2l_ref[0])
        else:
            c = _dot(hid.astype(BF16), w2h_ref[0])
        c = jnp.where(row < ns - 1, c, 0.0)
        if precise:
            ch, cl = _split(c)
            o_ref[0, g] = jnp.concatenate([ch, ch, cl, jnp.zeros_like(ch)], axis=1)
        else:
            wide = jnp.concatenate([c, jnp.zeros_like(c)], axis=1)
            o_ref[0, g] = wide.T[:HEAD_DIM].astype(o_ref.dtype)


def _compress(kv_c, pos, w1h, w1l, w2h, w2l, which):
    b, s, _ = kv_c.shape
    ns = s // CMP_STRIDE
    g = N_GROUPS
    precise = which == 0
    out_tail = (ns, 4 * HEAD_DIM) if precise else (HEAD_DIM, ns)
    return pl.pallas_call(
        functools.partial(_compress_kernel, precise=precise),
        grid=(b,),
        in_specs=[
            pl.BlockSpec((1, s, D_KV), lambda i: (i, 0, which)),
            pl.BlockSpec((1, CMP_BLOCK, SUBLANES, HEAD_DIM), lambda i: (which, 0, 0, 0)),
            pl.BlockSpec((1, CMP_BLOCK, HEAD_DIM, CMP_HIDDEN), lambda i: (which, 0, 0, 0)),
            pl.BlockSpec((1, CMP_BLOCK, HEAD_DIM, CMP_HIDDEN), lambda i: (which, 0, 0, 0)),
            pl.BlockSpec((1, CMP_HIDDEN, HEAD_DIM), lambda i: (which, 0, 0)),
            pl.BlockSpec((1, CMP_HIDDEN, HEAD_DIM), lambda i: (which, 0, 0)),
        ],
        out_specs=pl.BlockSpec((1, g) + out_tail, lambda i: (i, 0, 0, 0)),
        out_shape=jax.ShapeDtypeStruct((b, g) + out_tail, BF16),
        compiler_params=_params("parallel"),
        name="compress",
    )(kv_c, pos, w1h, w1l, w2h, w2l)


def _importance_matrix(nc_pad, nb):
    r_s = SLC_BLOCK // CMP_STRIDE
    r_c = CMP_BLOCK // CMP_STRIDE
    mat = np.zeros((nb, nc_pad), np.float32)
    for j in range(nb):
        for m in range(r_s):
            for n in range(r_c):
                i = r_s * j - m - n
                if i >= 0:
                    mat[j, i] += 1.0
    return mat


def _select_kernel(q_ref, kc3_ref, vct_ref, mt_ref, oc_ref, selx_ref, score_ref, rank_ref):
    qi = pl.program_id(2)
    tq = q_ref.shape[1]
    ncp = kc3_ref.shape[2]
    nb = mt_ref.shape[0]
    nh = HEADS_PER_GROUP
    bpt = ATTN_KV // SLC_BLOCK
    half_heads = nh // 2

    t1 = qi * tq + lax.broadcasted_iota(jnp.int32, (1, tq), 1)
    ci = lax.broadcasted_iota(jnp.int32, (ncp, 1), 0)
    visible = (ci * CMP_STRIDE + (CMP_BLOCK - 1)) <= t1
    mask = jnp.concatenate([visible] * half_heads, axis=1)

    def scores(h):
        qt = jnp.concatenate([q_ref[r * HEAD_DIM:(r + 1) * HEAD_DIM, :]
                              for r in range(h * half_heads, (h + 1) * half_heads)], axis=1)
        qh, ql = _split(qt)
        rhs = jnp.concatenate([qh, ql, qh, jnp.zeros_like(qh)], axis=0)
        return _dot(kc3_ref[0, 0], rhs)

    def softmax(s):
        s = jnp.where(mask, s, NEG)
        m = jnp.max(s, axis=0, keepdims=True)
        e = jnp.exp2(s - m)
        den = jnp.sum(e, axis=0, keepdims=True)
        inv = jnp.where(m > 0.5 * NEG, 1.0 / den, 0.0)
        return e * inv

    s_halves = [scores(h) for h in range(2)]
    ps = None
    for h in range(2):
        p = softmax(s_halves[h])
        oc = _dot(vct_ref[0, 0], p.astype(BF16))
        for k in range(half_heads):
            r = h * half_heads + k
            pk = p[:, k * tq:(k + 1) * tq]
            ps = pk if ps is None else ps + pk
            oc_ref[0, 0, 0, r * HEAD_DIM:(r + 1) * HEAD_DIM, :] = oc[:, k * tq:(k + 1) * tq]

    blk = lax.broadcasted_iota(jnp.int32, (nb, 1), 0)
    cur = t1 // SLC_BLOCK
    valid = blk <= cur

    def store_rows(selb):
        for jt in range(nb // bpt):
            rows = jnp.concatenate([selb[jt * bpt:(jt + 1) * bpt], jnp.zeros((SEL_ROWS - bpt, tq), F32)], axis=0)
            selx_ref[0, 0, 0, jt] = rows.astype(selx_ref.dtype)

    need_rank = (qi + 1) * tq > N_SELECT * SLC_BLOCK

    @pl.when(jnp.logical_not(need_rank))
    def _():
        store_rows(jnp.where(valid, 0.0, NEG))

    @pl.when(need_rank)
    def _():
        p1 = ps.astype(BF16)
        r1 = ps - p1.astype(F32)
        p2 = r1.astype(BF16)
        p3 = (r1 - p2.astype(F32)).astype(BF16)
        mt = mt_ref[...]
        imp = _dot(mt, p1) + _dot(mt, p2) + _dot(mt, p3)
        forced = (blk == 0) | (blk > cur - N_LOCAL)
        score_ref[...] = jnp.where(valid, jnp.where(forced, FORCE_SCORE, imp), NEG)
        rank_ref[...] = jnp.zeros(rank_ref.shape, jnp.int32)

    nchunk = nb // SUBLANES
    last_visible = ((qi + 1) * tq - 1) // SLC_BLOCK
    sub = lax.broadcasted_iota(jnp.int32, (SUBLANES, 1), 0)
    for vi in range(nchunk):
        @pl.when(need_rank & (vi * SUBLANES <= last_visible))
        def _():
            for i in range(vi * SUBLANES, (vi + 1) * SUBLANES):
                si = score_ref[i:i + 1, :]
                for v in range(nchunk):
                    rows = slice(SUBLANES * v, SUBLANES * (v + 1))
                    chunk = score_ref[rows, :]
                    if v > vi:
                        beats = jnp.where(si >= chunk, 1, 0)
                    elif v < vi:
                        beats = jnp.where(si > chunk, 1, 0)
                    else:
                        beats = jnp.where(sub > i % SUBLANES, jnp.where(si >= chunk, 1, 0),
                                          jnp.where(si > chunk, 1, 0))
                    rank_ref[rows, :] = rank_ref[rows, :] + beats

    @pl.when(need_rank)
    def _():
        sel = valid & (rank_ref[...] < N_SELECT)
        store_rows(jnp.where(sel, 0.0, NEG))


def _select(qt, kc3, vct, mt, b):
    s = qt.shape[1] // b
    g = N_GROUPS
    ncp = kc3.shape[2]
    nb = mt.shape[0]
    gw = HEADS_PER_GROUP * HEAD_DIM
    n_sel = nb // (ATTN_KV // SLC_BLOCK)
    grid = (b, g, s // ATTN_Q)
    return pl.pallas_call(
        _select_kernel,
        grid=grid,
        in_specs=[
            pl.BlockSpec((gw, ATTN_Q), lambda i, j, k: (j, i * (s // ATTN_Q) + k)),
            pl.BlockSpec((1, 1, ncp, 4 * HEAD_DIM), lambda i, j, k: (i, j, 0, 0)),
            pl.BlockSpec((1, 1, HEAD_DIM, ncp), lambda i, j, k: (i, j, 0, 0)),
            pl.BlockSpec((nb, ncp), lambda i, j, k: (0, 0)),
        ],
        out_specs=[
            pl.BlockSpec((1, 1, 1, gw, ATTN_Q), lambda i, j, k: (i, j, k, 0, 0)),
            pl.BlockSpec((1, 1, 1, n_sel, SEL_ROWS, ATTN_Q), lambda i, j, k: (i, j, k, 0, 0, 0)),
        ],
        out_shape=[
            jax.ShapeDtypeStruct((b, g, s // ATTN_Q, gw, ATTN_Q), F32),
            jax.ShapeDtypeStruct((b, g, s // ATTN_Q, n_sel, SEL_ROWS, ATTN_Q), BF16),
        ],
        scratch_shapes=[pltpu.VMEM((nb, ATTN_Q), F32), pltpu.VMEM((nb, ATTN_Q), jnp.int32)],
        compiler_params=_params("parallel", "parallel", "parallel"),
        name="cmp_select",
    )(qt, kc3, vct, mt)


def _attn_kernel(q_ref, ks_ref, kw_ref, vs_ref, vw_ref, selx_ref, g_ref, oc_ref, o_ref,
                 m_ref, acc_ref, s_a, s_b, smax_ref, pat_ref):
    grp = pl.program_id(1)
    qi = pl.program_id(2)
    tq = q_ref.shape[1]
    kv = ATTN_KV
    nh = HEADS_PER_GROUP
    n = nh * tq
    slc, win = 0, 1

    qt4 = q_ref[...].astype(BF16)
    qt = jnp.concatenate([qt4[r * HEAD_DIM:(r + 1) * HEAD_DIM] for r in range(nh)], axis=1)
    zq = jnp.zeros_like(qt)
    q2 = jnp.concatenate([jnp.where(grp == 0, qt, zq), jnp.where(grp == 1, qt, zq)], axis=0)
    pad = jnp.zeros((K_AUG - D_KV - SEL_ROWS, n), BF16)

    m_ref[...] = jnp.full(m_ref.shape, NEG, F32)
    acc_ref[...] = jnp.zeros(acc_ref.shape, F32)

    @pl.when(qi == 0)
    def _():
        row = lax.broadcasted_iota(jnp.int32, (kv, tq), 0)
        lane = lax.broadcasted_iota(jnp.int32, (kv, tq), 1)
        pat_ref[PAT_ZERO] = jnp.zeros((kv, tq), F32)
        pat_ref[PAT_CAUSAL] = jnp.where(row <= lane, 0.0, NEG)
        pat_ref[PAT_FAR] = jnp.where(lane < row, 0.0, NEG)
        pat_ref[PAT_NONE] = jnp.full((kv, tq), NEG, F32)

    def lanes4(x):
        return jnp.concatenate([x] * nh, axis=1)

    def rows(j):
        return pl.ds(pl.multiple_of(j * kv, kv), kv)

    def store_scores(s, s_ref, slot):
        s_ref[...] = s
        smax_ref[slot] = jnp.max(s, axis=0, keepdims=True)

    def slc_scores(j):
        rhs = jnp.concatenate([q2, lanes4(selx_ref[0, 0, 0, j]), pad], axis=0)
        return _dot(ks_ref[0, rows(j), :], rhs)

    def produce_slc(s_ref, slot, j):
        store_scores(slc_scores(j), s_ref, slot)

    def produce_diag(s_ref, slot):
        store_scores(slc_scores(qi) + lanes4(pat_ref[PAT_CAUSAL]), s_ref, slot)

    def produce_win(s_ref, slot, j, pat):
        s = _dot(kw_ref[0, rows(j), :], q2)
        store_scores(s if pat is None else s + lanes4(pat_ref[pat]), s_ref, slot)

    def consume(s_ref, slot, v_ref, j, kind):
        m_old = m_ref[kind]
        m_new = jnp.maximum(m_old, smax_ref[slot])
        alpha = jnp.exp2(m_old - m_new)
        p = jnp.exp2(s_ref[...] - m_new)
        v_t = v_ref[0, rows(j), :][:, :V_AUG]
        pv = lax.dot_general(v_t, p.astype(BF16), (((0,), (0,)), ((), ())),
                             preferred_element_type=F32)
        acc_ref[kind] = alpha * acc_ref[kind] + pv
        m_ref[kind] = m_new

    def plain(j):
        return (lambda r, sl: produce_slc(r, sl, j)), (lambda r, sl: consume(r, sl, vs_ref, j, slc))

    def window(j, pat):
        return (lambda r, sl: produce_win(r, sl, j, pat)), (lambda r, sl: consume(r, sl, vw_ref, j, win))

    diag = (produce_diag, lambda r, sl: consume(r, sl, vs_ref, qi, slc))

    def run(stages):
        bufs = ((s_a, 0), (s_b, 1))
        for i, (_, consume_i) in enumerate(stages):
            if i + 1 < len(stages):
                stages[i + 1][0](*bufs[(i + 1) % 2])
            consume_i(*bufs[i % 2])

    @pl.when(qi == 0)
    def _():
        produce_diag(s_a, 0)
        run([diag, window(0, PAT_CAUSAL)])

    @pl.when(qi > 0)
    def _():
        produce_slc(s_a, 0, 0)

        def body(u, carry):
            j = ATTN_UNROLL * u
            bufs = ((s_a, 0), (s_b, 1))
            for k in range(ATTN_UNROLL):
                produce_slc(*bufs[(k + 1) % 2], j + k + 1)
                consume(*bufs[k % 2], vs_ref, j + k, slc)
            return carry

        n_loop = (qi - 1) // ATTN_UNROLL
        lax.fori_loop(0, n_loop, body, 0)
        left = qi - ATTN_UNROLL * n_loop

        for k in range(1, ATTN_UNROLL + 1):
            @pl.when(left == k)
            def _():
                far = jnp.where(qi >= 2, PAT_FAR, PAT_NONE) if k == 1 else PAT_FAR
                run([plain(qi - k + i) for i in range(k)]
                    + [diag, window(qi, PAT_CAUSAL), window(qi - 1, None), window(jnp.maximum(qi - 2, 0), far)])

    gt = _sigmoid(g_ref[0]).T
    o_slc = acc_ref[slc, :HEAD_DIM] * (1.0 / acc_ref[slc, HEAD_DIM:HEAD_DIM + 1])
    o_win = acc_ref[win, :HEAD_DIM] * (1.0 / acc_ref[win, HEAD_DIM:HEAD_DIM + 1])
    outs = []
    for r in range(nh):
        cols = slice(r * tq, (r + 1) * tq)
        o_cmp = oc_ref[0, 0, 0, r * HEAD_DIM:(r + 1) * HEAD_DIM, :]
        outs.append(gt[3 * r:3 * r + 1] * o_cmp + gt[3 * r + 1:3 * r + 2] * o_slc[:, cols]
                    + gt[3 * r + 2:3 * r + 3] * o_win[:, cols])
    o_ref[0] = jnp.concatenate(outs, axis=0).T.astype(o_ref.dtype)


def _attention(qt, ks, kw, vall, selx, gates, oc):
    b, s, _ = ks.shape
    g = N_GROUPS
    gw = HEADS_PER_GROUP * HEAD_DIM
    tq = ATTN_Q
    n = HEADS_PER_GROUP * tq
    n_sel = selx.shape[3]
    return pl.pallas_call(
        _attn_kernel,
        grid=(b, g, s // tq),
        in_specs=[
            pl.BlockSpec((gw, tq), lambda i, j, k: (j, i * (s // tq) + k)),
            pl.BlockSpec((1, s, K_AUG), lambda i, j, k: (i, 0, 0)),
            pl.BlockSpec((1, s, D_KV), lambda i, j, k: (i, 0, 0)),
            pl.BlockSpec((1, s, D_KV), lambda i, j, k: (i, 0, j)),
            pl.BlockSpec((1, s, D_KV), lambda i, j, k: (i, 0, N_GROUPS + j)),
            pl.BlockSpec((1, 1, 1, n_sel, SEL_ROWS, tq), lambda i, j, k: (i, j, k, 0, 0, 0)),
            pl.BlockSpec((1, tq, GATE_LANES), lambda i, j, k: (i, k, j)),
            pl.BlockSpec((1, 1, 1, gw, tq), lambda i, j, k: (i, j, k, 0, 0)),
        ],
        out_specs=pl.BlockSpec((1, tq, gw), lambda i, j, k: (i, k, j)),
        out_shape=jax.ShapeDtypeStruct((b, s, D_ATTN), BF16),
        scratch_shapes=[
            pltpu.VMEM((2, 1, n), F32),
            pltpu.VMEM((2, V_AUG, n), F32),
            pltpu.VMEM((ATTN_KV, n), F32),
            pltpu.VMEM((ATTN_KV, n), F32),
            pltpu.VMEM((2, 1, n), F32),
            pltpu.VMEM((4, ATTN_KV, tq), F32),
        ],
        compiler_params=_params("parallel", "parallel", "arbitrary"),
        name="slc_win_attn",
    )(qt, ks, kw, vall, vall, selx, gates, oc)


def _rnn_kernel(xr_ref, gr_ref, cw_ref, cb_ref, wa_ref, wi_ref, ba_ref, bi_ref, lam_ref, o_ref,
                xprev, hcarry):
    si = pl.program_id(1)
    ts = xr_ref.shape[1]
    d = xr_ref.shape[2]

    @pl.when(si == 0)
    def _():
        xprev[...] = jnp.zeros(xprev.shape, F32)
        hcarry[...] = jnp.zeros(hcarry.shape, F32)

    ng = ts // SUBLANES
    w = LRU_PACK
    sub = lax.broadcasted_iota(jnp.int32, (ng, SUBLANES, w), 1)
    for c in range(d // w):
        cols = slice(c * w, (c + 1) * w)
        x3 = xr_ref[0, :, cols].reshape(ng, SUBLANES, w)
        xc = cb_ref[:, cols] + x3 * cw_ref[CONV_WIDTH - 1:CONV_WIDTH, cols]
        for back in range(1, CONV_WIDTH):
            rolled = pltpu.roll(x3, back, 1)
            prev = jnp.concatenate([pltpu.roll(xprev[:, cols], back, 0)[None], rolled[:-1]], axis=0)
            tap = CONV_WIDTH - 1 - back
            xc = xc + jnp.where(sub >= back, rolled, prev) * cw_ref[tap:tap + 1, cols]
        xprev[:, cols] = x3[ng - 1]
        xc = xc.reshape(ts, w)
        xcb = xc.astype(BF16)
        ra = _dot(xcb, wa_ref[c])
        ri = _dot(xcb, wi_ref[c])
        tr1 = jnp.tanh(ra + ba_ref[:, cols]) + 1.0
        ig = 0.5 * jnp.tanh(ri + bi_ref[:, cols]) + 0.5
        neg_log_a = tr1 * (0.5 * LRU_C * jax.nn.softplus(-lam_ref[:, cols]))
        a = jnp.exp2(neg_log_a * (-LOG2_E))
        one_m_a2 = jnp.tanh(neg_log_a) * (a * a + 1.0)
        root = jnp.where(one_m_a2 > 0.0, one_m_a2 * lax.rsqrt(one_m_a2), 0.0)
        bt = root * (ig * xc)
        a3 = a.reshape(ng, SUBLANES, w)
        b3 = bt.reshape(ng, SUBLANES, w)
        step = 1
        while step < SUBLANES:
            keep = sub >= step
            a_prev = jnp.where(keep, pltpu.roll(a3, step, 1), 1.0)
            b_prev = jnp.where(keep, pltpu.roll(b3, step, 1), 0.0)
            b3 = b3 + a3 * b_prev
            a3 = a3 * a_prev
            step *= 2
        h = hcarry[0:1, cols]
        hs = []
        for k in range(ng):
            hk = b3[k] + a3[k] * h
            hs.append(hk)
            h = hk[SUBLANES - 1:SUBLANES]
        hcarry[0:1, cols] = h
        o_ref[0, :, cols] = (jnp.concatenate(hs, axis=0) * gr_ref[0, :, cols]).astype(o_ref.dtype)


def _rnn(xr, gr, cw, cb, wa, wi, ba, bi, lam):
    b, s, d = xr.shape
    ts = RNN_ROWS
    row = lambda a: _resident(a.shape)
    return pl.pallas_call(
        _rnn_kernel,
        grid=(b, s // ts),
        in_specs=[
            pl.BlockSpec((1, ts, d), lambda i, j: (i, j, 0)),
            pl.BlockSpec((1, ts, d), lambda i, j: (i, j, 0)),
            row(cw), row(cb), row(wa), row(wi), row(ba), row(bi), row(lam),
        ],
        out_specs=pl.BlockSpec((1, ts, d), lambda i, j: (i, j, 0)),
        out_shape=jax.ShapeDtypeStruct((b, s, d), BF16),
        scratch_shapes=[pltpu.VMEM((SUBLANES, d), F32), pltpu.VMEM((SUBLANES, d), F32)],
        compiler_params=_params("parallel", "arbitrary"),
        name="rg_lru",
    )(xr, gr, cw, cb, wa, wi, ba, bi, lam)


def _merge_mlp_kernel(x_ref, attn_ref, rnn_ref, ga_ref, gb_ref, wua_ref, wur_ref, wo_ref,
                      n2_ref, w1_ref, w2_ref, fn_ref, o_ref, *, final):
    up_a = _dot(attn_ref[...], wua_ref[0])
    up_r = _dot(rnn_ref[...], wur_ref[0])
    merged = _sigmoid(ga_ref[...]) * up_a + _sigmoid(gb_ref[...]) * up_r
    x = x_ref[...] + _dot(merged.astype(BF16), wo_ref[0])
    var = jnp.mean(x * x, axis=-1, keepdims=True)
    hn = (x * lax.rsqrt(var + EPS) * n2_ref[...]).astype(BF16)
    d_ff = w1_ref.shape[2]
    acc = jnp.zeros_like(x)
    for c in range(d_ff // MLP_FF_CHUNK):
        lo, hi = c * MLP_FF_CHUNK, (c + 1) * MLP_FF_CHUNK
        h1 = jnp.maximum(_dot(hn, w1_ref[0, :, lo:hi]), 0.0)
        acc = acc + _dot((h1 * h1).astype(BF16), w2_ref[0, lo:hi, :])
    x = x + acc
    if final:
        var = jnp.mean(x * x, axis=-1, keepdims=True)
        x = x * lax.rsqrt(var + EPS) * fn_ref[...]
    o_ref[...] = x


def _merge_mlp(x2, attn, rnn, ga, gb, wua, wur, wo, n2, w1, w2, fn, layer, final):
    t, d = x2.shape
    tm = MLP_ROWS
    rows = lambda w: pl.BlockSpec((tm, w), lambda i: (i, 0))
    return pl.pallas_call(
        functools.partial(_merge_mlp_kernel, final=final),
        grid=(t // tm,),
        in_specs=[
            rows(d), rows(attn.shape[1]), rows(rnn.shape[1]), rows(d), rows(d),
            _resident_layer(wua.shape, layer), _resident_layer(wur.shape, layer), _resident_layer(wo.shape, layer),
            _resident(n2.shape), _resident_layer(w1.shape, layer), _resident_layer(w2.shape, layer),
            _resident(fn.shape),
        ],
        out_specs=rows(d),
        out_shape=jax.ShapeDtypeStruct((t, d), F32),
        compiler_params=_params("parallel"),
        name="merge_mlp",
    )(x2, attn, rnn, ga, gb, wua, wur, wo, n2, w1, w2, fn)


def _block_diag_pack(w, pack):
    nblk, k, _ = w.shape
    per = pack // k
    w4 = w.reshape(nblk // per, per, k, k)
    eye = jnp.eye(per, dtype=w.dtype)
    out = jnp.einsum("cpij,pq->cpiqj", w4, eye)
    return out.reshape(nblk // per, pack, pack)


def _in_proj_layout(d):
    dkv = D_KV
    outputs = ((D_ATTN, F32, True), (2 * dkv, F32, False), (N_GROUPS * GATE_LANES, F32, False),
               (K_AUG, BF16, False), (dkv, BF16, False), (4 * dkv, BF16, False),
               (d, F32, False), (d, F32, False), (d, F32, False), (d, F32, False))
    chunks = (
        (0, 1, 0, D_ATTN, (("transposed", 0, D_ATTN, 0, 0),)),
        (0, 1, D_ATTN, 2 * dkv, (("plain", 0, 2 * dkv, 1, 0),)),
        (2, None, 0, N_GROUPS * GATE_LANES, (("plain", 0, N_GROUPS * GATE_LANES, 2, 0),)),
        (0, None, D_ATTN + 2 * dkv, 2 * dkv, (("keys", 0, dkv, 3, 0), ("values", dkv, dkv, 5, 0))),
        (0, None, D_ATTN + 4 * dkv, 2 * dkv, (("plain", 0, dkv, 4, 0), ("values", dkv, dkv, 5, 2 * dkv))),
        (3, None, 0, d, (("plain", 0, d, 6, 0),)),
        (3, None, d, d, (("gelu", 0, d, 7, 0),)),
        (3, None, 2 * d, d, (("plain", 0, d, 8, 0),)),
        (3, None, 3 * d, d, (("plain", 0, d, 9, 0),)),
    )
    return chunks, outputs


def _in_proj_weights(w_in):
    head = D_ATTN + 6 * D_KV
    n_gates = N_HEADS * 3
    n_precise = D_ATTN + 2 * D_KV
    wt = jnp.swapaxes(w_in, 1, 2)
    scale = jnp.where(jnp.arange(head) < D_ATTN, ATTN_SCALE * LOG2_E, 1.0).astype(F32)[:, None]
    wa = wt[:, :head] * scale
    wa_hi = wa.astype(BF16)
    wa_lo = (wa[:, :n_precise] - wa_hi[:, :n_precise].astype(F32)).astype(BF16)
    g_nsa = wt[:, head:head + n_gates]
    per_group = HEADS_PER_GROUP * 3
    wg = jnp.concatenate(
        [jnp.pad(g_nsa[:, g * per_group:(g + 1) * per_group], ((0, 0), (0, GATE_LANES - per_group), (0, 0)))
         for g in range(N_GROUPS)], axis=1).astype(BF16)
    wb = wt[:, head + n_gates:].astype(BF16)
    return wa_hi, wa_lo, wg, wb


def _compress_weights(w1_k, w1_v, w2_k, w2_v, pos_k, pos_v):
    w1 = jnp.stack([w1_k, w1_v]).reshape(2, CMP_BLOCK, HEAD_DIM, CMP_HIDDEN)
    pos = jnp.stack([pos_k, pos_v])[:, :, None, :]
    pos = jnp.broadcast_to(pos, (2, CMP_BLOCK, SUBLANES, HEAD_DIM))
    w1h, w1l = _split(w1)
    w2h, w2l = _split(jnp.stack([w2_k, w2_v]))
    return pos, w1h, w1l, w2h, w2l


def kernel(x, norm1_w, w_in, cmp_pos_k, cmp_pos_v, cmp_k_w1, cmp_k_w2, cmp_v_w1, cmp_v_w2, conv_w, conv_b, lru_w_a, lru_b_a, lru_w_i, lru_b_i, lru_lambda, w_up_attn, w_up_rnn, w_out, norm2_w, mlp_w1, mlp_w2, final_norm_w):
    b, s, d = x.shape
    depth = w_in.shape[0]
    t = b * s
    ns = s // CMP_STRIDE
    nb = s // SLC_BLOCK
    assert s % ATTN_Q == 0 and s % RNN_ROWS == 0 and t % PROJ_ROWS == 0 and t % MLP_ROWS == 0
    assert ATTN_Q == ATTN_KV and WINDOW == 2 * ATTN_KV and nb >= N_SELECT
    mt = jnp.asarray(_importance_matrix(ns, nb), BF16)
    x2 = x.reshape(t, d)
    fn = final_norm_w.reshape(1, d)
    chunks, proj_outputs = _in_proj_layout(d)
    proj_w = _in_proj_weights(w_in)
    wua, wur, wo, w1, w2 = (w.astype(BF16) for w in (w_up_attn, w_up_rnn, w_out, mlp_w1, mlp_w2))
    for l in range(depth):
        (qt, kv_c, gates, k_s, k_w, vall, xr, gr, g_a, g_b) = _norm_proj(
            x2, norm1_w[l].reshape(1, d), proj_w, l, chunks, proj_outputs)

        cmp_w = _compress_weights(cmp_k_w1[l], cmp_v_w1[l], cmp_k_w2[l], cmp_v_w2[l], cmp_pos_k[l], cmp_pos_v[l])
        kvc = _compress(kv_c.reshape(b, s, 2 * D_KV), *cmp_w, which=0)
        kvct = _compress(kv_c.reshape(b, s, 2 * D_KV), *cmp_w, which=1)

        oc, selx = _select(qt, kvc, kvct, mt, b)
        attn = _attention(qt, k_s.reshape(b, s, K_AUG), k_w.reshape(b, s, D_KV),
                          vall.reshape(b, s, 4 * D_KV), selx,
                          gates.reshape(b, s, N_GROUPS * GATE_LANES), oc)

        rnn = _rnn(xr.reshape(b, s, d), gr.reshape(b, s, d),
                   jnp.pad(conv_w[l], ((0, SUBLANES - CONV_WIDTH), (0, 0))), conv_b[l].reshape(1, d),
                   _block_diag_pack(0.5 * lru_w_a[l], LRU_PACK).astype(BF16),
                   _block_diag_pack(0.5 * lru_w_i[l], LRU_PACK).astype(BF16),
                   0.5 * lru_b_a[l].reshape(1, d), 0.5 * lru_b_i[l].reshape(1, d), lru_lambda[l].reshape(1, d))

        x2 = _merge_mlp(x2, attn.reshape(t, D_ATTN), rnn.reshape(t, d), g_a, g_b, wua, wur, wo,
                        norm2_w[l].reshape(1, d), w1, w2, fn, layer=l, final=(l == depth - 1))
    return x2.reshape(b, s, d)
```
